```python
import jax, jax.numpy as jnp
from jax import lax
import numpy as np

D_MODEL = 2048
BATCH = 2
SEQ = 8192
DEPTH = 4

CHUNK = 64
Q_BLOCK = 128
N_MIXERS = 3
RMS_EPS = 1e-6
SB_HEADS = 16
SB_HEAD_DIM = D_MODEL // SB_HEADS
HG_HEADS = 16
HG_KEY_DIM = D_MODEL // HG_HEADS
HG_VAL_DIM = D_MODEL // HG_HEADS
HG_WIDTH = HG_HEADS * HG_KEY_DIM
MLA_HEADS = 16
MLA_NOPE = 128
MLA_ROPE = 64
MLA_V = 128
MLA_Q_RANK = 768
MLA_KV_RANK = 512
ROPE_THETA = 10000.0
D_FF = 4 * D_MODEL

kernel_name = "hybrid_stickbreak_hgrn2_mla_stream"


def _n_layers_of(m):
    return len(range(m, DEPTH, N_MIXERS))


def rms_norm(x, g):
    xf = x.astype(jnp.float32)
    y = xf * lax.rsqrt(jnp.mean(xf * xf, axis=-1, keepdims=True) + RMS_EPS)
    return (y * g.astype(jnp.float32)).astype(x.dtype)


def _sweep_query_blocks(block_fn, queries):
    B, S = queries[0].shape[:2]
    nb = S // Q_BLOCK
    blocked = tuple(a.reshape(B, nb, Q_BLOCK, *a.shape[2:]).swapaxes(0, 1) for a in queries)
    starts = jnp.arange(nb, dtype=jnp.int32) * Q_BLOCK
    out = lax.map(lambda xs: block_fn(xs[0], *xs[1]), (starts, blocked))
    return out.swapaxes(0, 1).reshape(B, S, *out.shape[3:])


def stick_breaking_mixer(h, w_qkv, w_o):
    B, S, _ = h.shape
    qkv = h @ w_qkv
    q, k, v = [a.reshape(B, S, SB_HEADS, SB_HEAD_DIM) for a in jnp.split(qkv, 3, axis=-1)]
    key_idx = jnp.arange(S)
    scale = SB_HEAD_DIM ** -0.5

    def block(q0, qb):
        z = jnp.einsum('bqhd,bkhd->bhqk', qb, k).astype(jnp.float32) * scale
        t = q0 + jnp.arange(Q_BLOCK)
        mask = key_idx[None, :] < t[:, None]
        log_one_minus = jnp.where(mask, -jax.nn.softplus(z), 0.0)
        after = lax.cumsum(log_one_minus, axis=3, reverse=True) - log_one_minus
        A = jnp.where(mask, jnp.exp(jax.nn.log_sigmoid(z) + after), 0.0)
        return jnp.einsum('bhqk,bkhd->bqhd', A.astype(v.dtype), v)

    o = _sweep_query_blocks(block, (q,))
    return o.reshape(B, S, SB_HEADS * SB_HEAD_DIM) @ w_o


def hgrn2_mixer(h, w_in, g_norm, w_o, lb):
    B, S, _ = h.shape
    N = S // CHUNK
    proj = h @ w_in
    q, fz, i, g = jnp.split(proj, [HG_WIDTH, 2 * HG_WIDTH, 2 * HG_WIDTH + HG_HEADS * HG_VAL_DIM], axis=-1)
    fz = fz.astype(jnp.float32)
    log_f = jnp.logaddexp(jnp.log(lb), jnp.log1p(-lb) + jax.nn.log_sigmoid(fz))
    k = (1.0 - lb) * jax.nn.sigmoid(-fz)

    def to_chunks(a, d):
        return a.astype(jnp.float32).reshape(B, N, CHUNK, HG_HEADS, d).transpose(1, 0, 3, 2, 4)

    xs = (to_chunks(q, HG_KEY_DIM), to_chunks(k, HG_KEY_DIM),
          to_chunks(i, HG_VAL_DIM), to_chunks(log_f, HG_KEY_DIM))
    causal = jnp.tril(jnp.ones((CHUNK, CHUNK), dtype=bool))[None, None, :, :, None]

    def step(state, chunk):
        qc, kc, vc, lfc = chunk
        b = jnp.cumsum(lfc, axis=2)
        b_last = b[:, :, -1:, :]
        o_inter = jnp.einsum('bhtc,bhcv->bhtv', qc * jnp.exp(b), state)
        diff = jnp.where(causal, b[:, :, :, None, :] - b[:, :, None, :, :], -jnp.inf)
        scores = jnp.einsum('bhtc,bhsc,bhtsc->bhts', qc, kc, jnp.exp(diff))
        o_intra = jnp.einsum('bhts,bhsv->bhtv', scores, vc)
        new_state = (jnp.exp(b_last[:, :, 0, :])[..., None] * state
                     + jnp.einsum('bhsc,bhsv->bhcv', kc * jnp.exp(b_last - b), vc))
        return new_state, o_inter + o_intra

    state0 = jnp.zeros((B, HG_HEADS, HG_KEY_DIM, HG_VAL_DIM), jnp.float32)
    _, o = lax.scan(step, state0, xs)
    o = o.transpose(1, 0, 3, 2, 4).reshape(B, S, HG_HEADS, HG_VAL_DIM)
    o = o * lax.rsqrt(jnp.mean(o * o, axis=-1, keepdims=True) + RMS_EPS) * g_norm.astype(jnp.float32)
    o = o.reshape(B, S, HG_HEADS * HG_VAL_DIM) * jax.nn.silu(g.astype(jnp.float32))
    return o.astype(h.dtype) @ w_o


def _rope_cos_sin(positions):
    inv_freq = ROPE_THETA ** (-jnp.arange(0, MLA_ROPE, 2, dtype=jnp.float32) / MLA_ROPE)
    ang = positions.astype(jnp.float32)[..., None] * inv_freq
    return jnp.cos(ang), jnp.sin(ang)


def _apply_rope(x, cos, sin):
    x1, x2 = jnp.split(x.astype(jnp.float32), 2, axis=-1)
    return jnp.concatenate([x1 * cos - x2 * sin, x1 * sin + x2 * cos], axis=-1).astype(x.dtype)


def mla_mixer(h, positions, w_dkv, q_norm, kv_norm, w_uq, w_ukv, w_o):
    B, S, _ = h.shape
    c_q, c_kv, k_pe = jnp.split(h @ w_dkv, [MLA_Q_RANK, MLA_Q_RANK + MLA_KV_RANK], axis=-1)
    c_q = rms_norm(c_q, q_norm)
    c_kv = rms_norm(c_kv, kv_norm)
    q = (c_q @ w_uq).reshape(B, S, MLA_HEADS, MLA_NOPE + MLA_ROPE)
    q_nope, q_pe = q[..., :MLA_NOPE], q[..., MLA_NOPE:]
    kv = (c_kv @ w_ukv).reshape(B, S, MLA_HEADS, MLA_NOPE + MLA_V)
    k_nope, v = kv[..., :MLA_NOPE], kv[..., MLA_NOPE:]
    cos, sin = _rope_cos_sin(positions)
    q_pe = _apply_rope(q_pe, cos[:, :, None, :], sin[:, :, None, :])
    k_pe = _apply_rope(k_pe, cos, sin)
    key_chunk = jnp.arange(S) // CHUNK
    scale = (MLA_NOPE + MLA_ROPE) ** -0.5

    def block(q0, qn, qp):
        s = (jnp.einsum('bqhd,bkhd->bhqk', qn, k_nope)
             + jnp.einsum('bqhr,bkr->bhqk', qp, k_pe)).astype(jnp.float32) * scale
        q_chunk = (q0 + jnp.arange(Q_BLOCK)) // CHUNK
        mask = key_chunk[None, :] <= q_chunk[:, None]
        p = jax.nn.softmax(jnp.where(mask, s, -jnp.inf), axis=-1)
        return jnp.einsum('bhqk,bkhd->bqhd', p.astype(v.dtype), v)

    o = _sweep_query_blocks(block, (q_nope, q_pe))
    return o.reshape(B, S, MLA_HEADS * MLA_V) @ w_o


def sq_relu_mlp(h, w1, w2):
    return jnp.square(jax.nn.relu(h @ w1)) @ w2


def setup_inputs(seed: int = 0) -> dict:
    key = jax.random.key(seed)
    ks = jax.random.split(key, 20)
    n_sb, n_hg, n_mla = _n_layers_of(0), _n_layers_of(1), _n_layers_of(2)

    def w(k, shape, fan_in):
        return jax.random.normal(k, shape, jnp.float32) * (fan_in ** -0.5)

    def gain(k, shape):
        return 1.0 + 0.02 * jax.random.normal(k, shape, jnp.float32)

    return {
        "x": jax.random.normal(ks[0], (BATCH, SEQ, D_MODEL), jnp.float32),
        "positions": jnp.broadcast_to(jnp.arange(SEQ, dtype=jnp.int32), (BATCH, SEQ)),
        "norm_mix": gain(ks[1], (DEPTH, D_MODEL)),
        "norm_mlp": gain(ks[2], (DEPTH, D_MODEL)),
        "final_norm": gain(ks[3], (D_MODEL,)),
        "sb_w_qkv": w(ks[4], (n_sb, D_MODEL, 3 * SB_HEADS * SB_HEAD_DIM), D_MODEL),
        "sb_w_o": w(ks[5], (n_sb, SB_HEADS * SB_HEAD_DIM, D_MODEL), SB_HEADS * SB_HEAD_DIM),
        "hg_w_in": w(ks[6], (n_hg, D_MODEL, 3 * HG_WIDTH + HG_HEADS * HG_VAL_DIM), D_MODEL),
        "hg_lb_logits": 0.5 * jax.random.normal(ks[7], (DEPTH, HG_WIDTH), jnp.float32),
        "hg_g_norm": gain(ks[8], (n_hg, HG_VAL_DIM)),
        "hg_w_o": w(ks[9], (n_hg, HG_HEADS * HG_VAL_DIM, D_MODEL), HG_HEADS * HG_VAL_DIM),
        "mla_w_dkv": w(ks[10], (n_mla, D_MODEL, MLA_Q_RANK + MLA_KV_RANK + MLA_ROPE), D_MODEL),
        "mla_q_norm": gain(ks[11], (n_mla, MLA_Q_RANK)),
        "mla_kv_norm": gain(ks[12], (n_mla, MLA_KV_RANK)),
        "mla_w_uq": w(ks[13], (n_mla, MLA_Q_RANK, MLA_HEADS * (MLA_NOPE + MLA_ROPE)), MLA_Q_RANK),
        "mla_w_ukv": w(ks[14], (n_mla, MLA_KV_RANK, MLA_HEADS * (MLA_NOPE + MLA_V)), MLA_KV_RANK),
        "mla_w_o": w(ks[15], (n_mla, MLA_HEADS * MLA_V, D_MODEL), MLA_HEADS * MLA_V),
        "mlp_w1": w(ks[16], (DEPTH, D_MODEL, D_FF), D_MODEL),
        "mlp_w2": w(ks[17], (DEPTH, D_FF, D_MODEL), D_FF),
    }


def reference(x, positions, norm_mix, norm_mlp, final_norm, sb_w_qkv, sb_w_o, hg_w_in,
              hg_lb_logits, hg_g_norm, hg_w_o, mla_w_dkv, mla_q_norm, mla_kv_norm,
              mla_w_uq, mla_w_ukv, mla_w_o, mlp_w1, mlp_w2):
    p_lb = jax.nn.softmax(hg_lb_logits.astype(jnp.float32), axis=0)
    lb_all = jnp.cumsum(p_lb, axis=0) - p_lb[0]
    h = x
    for i in range(DEPTH):
        m, j = i % N_MIXERS, i // N_MIXERS
        a = rms_norm(h, norm_mix[i])
        if m == 0:
            y = stick_breaking_mixer(a, sb_w_qkv[j], sb_w_o[j])
        elif m == 1:
            y = hgrn2_mixer(a, hg_w_in[j], hg_g_norm[j], hg_w_o[j], lb_all[i])
        else:
            y = mla_mixer(a, positions, mla_w_dkv[j], mla_q_norm[j], mla_kv_norm[j],
                          mla_w_uq[j], mla_w_ukv[j], mla_w_o[j])
        h = h + y
        h = h + sq_relu_mlp(rms_norm(h, norm_mlp[i]), mlp_w1[i], mlp_w2[i])
    return rms_norm(h, final_norm)
```

```python
import functools

import jax
import jax.numpy as jnp
from jax import lax
from jax.experimental import pallas as pl
from jax.experimental.pallas import tpu as pltpu

F32 = jnp.float32
BF16 = jnp.bfloat16

RMS_EPS = 1e-6
N_MIXERS = 3
HEAD_DIM = 128
MLA_NOPE = 128
MLA_ROPE = 64
MLA_Q_RANK = 768
MLA_KV_RANK = 512
MLA_CHUNK = 64
ROPE_THETA = 10000.0
VMEM_LIMIT = 56 * 1024 * 1024


def _params(*sem):
    return pltpu.CompilerParams(dimension_semantics=sem, vmem_limit_bytes=VMEM_LIMIT)


def _dot(a, b):
    return jnp.dot(a, b, preferred_element_type=F32)


def _dot_nt(a, b):
    return lax.dot_general(a, b, (((1,), (1,)), ((), ())), preferred_element_type=F32)


def _dot_tn(a, b):
    return lax.dot_general(a, b, (((0,), (0,)), ((), ())), preferred_element_type=F32)


def _split3(x):
    hi = x.astype(BF16)
    r1 = x - hi.astype(F32)
    mid = r1.astype(BF16)
    lo = (r1 - mid.astype(F32)).astype(BF16)
    return hi, mid, lo


def _linear_kernel(*refs, has_gain, has_res):
    x_ref, w_ref = refs[0], refs[1]
    pos = 2
    g_ref = r_ref = None
    if has_gain:
        g_ref = refs[pos]
        pos += 1
    if has_res:
        r_ref = refs[pos]
        pos += 1
    o_ref, xn_ref = refs[pos], refs[pos + 1]

    @pl.when(pl.program_id(1) == 0)
    def _():
        x = x_ref[...].astype(F32)
        if has_gain:
            ms = jnp.mean(x * x, axis=-1, keepdims=True)
            x = x * lax.rsqrt(ms + RMS_EPS) * g_ref[...]
        xn_ref[...] = x.astype(BF16)

    acc = _dot(xn_ref[...], w_ref[...])
    if has_res:
        acc = acc + r_ref[...]
    o_ref[...] = acc.astype(o_ref.dtype)


def _linear(x, w, *, gain=None, residual=None, out_dtype=F32, x_col=0, tm=512, tn=512):
    M = x.shape[0]
    K, N = w.shape
    tm = min(tm, M)
    tn = min(tn, N)
    assert M % tm == 0 and N % tn == 0
    in_specs = [pl.BlockSpec((tm, K), lambda i, j: (i, x_col)),
                pl.BlockSpec((K, tn), lambda i, j: (0, j))]
    args = [x, w]
    if gain is not None:
        in_specs.append(pl.BlockSpec((1, K), lambda i, j: (0, 0)))
        args.append(gain.reshape(1, K).astype(F32))
    if residual is not None:
        in_specs.append(pl.BlockSpec((tm, tn), lambda i, j: (i, j)))
        args.append(residual)
    return pl.pallas_call(
        functools.partial(_linear_kernel, has_gain=gain is not None, has_res=residual is not None),
        grid=(M // tm, N // tn),
        in_specs=in_specs,
        out_specs=pl.BlockSpec((tm, tn), lambda i, j: (i, j)),
        out_shape=jax.ShapeDtypeStruct((M, N), out_dtype),
        scratch_shapes=[pltpu.VMEM((tm, K), BF16)],
        compiler_params=_params("parallel", "arbitrary"),
    )(*args)


def _mlp_kernel(x_ref, g_ref, w1_ref, w2_ref, o_ref, xn_ref, acc_ref):
    f = pl.program_id(1)

    @pl.when(f == 0)
    def _():
        x = x_ref[...]
        ms = jnp.mean(x * x, axis=-1, keepdims=True)
        xn_ref[...] = (x * lax.rsqrt(ms + RMS_EPS) * g_ref[...]).astype(BF16)
        acc_ref[...] = jnp.zeros_like(acc_ref)

    u = _dot(xn_ref[...], w1_ref[...])
    r = jnp.square(jnp.maximum(u, 0.0)).astype(BF16)
    acc_ref[...] += _dot(r, w2_ref[...])

    @pl.when(f == pl.num_programs(1) - 1)
    def _():
        o_ref[...] = x_ref[...] + acc_ref[...]


def _mlp(h, gain, w1, w2, *, tm=512, tf=512):
    M, D = h.shape
    F = w1.shape[1]
    assert M % tm == 0 and F % tf == 0
    return pl.pallas_call(
        _mlp_kernel,
        grid=(M // tm, F // tf),
        in_specs=[pl.BlockSpec((tm, D), lambda i, f: (i, 0)),
                  pl.BlockSpec((1, D), lambda i, f: (0, 0)),
                  pl.BlockSpec((D, tf), lambda i, f: (0, f)),
                  pl.BlockSpec((tf, D), lambda i, f: (f, 0))],
        out_specs=pl.BlockSpec((tm, D), lambda i, f: (i, 0)),
        out_shape=jax.ShapeDtypeStruct((M, D), F32),
        scratch_shapes=[pltpu.VMEM((tm, D), BF16), pltpu.VMEM((tm, D), F32)],
        compiler_params=_params("parallel", "arbitrary"),
    )(h, gain.reshape(1, D).astype(F32), w1, w2)


SB_SKIP = 110.0


def _sb_kernel(q_ref, k_ref, v_ref, o_ref, *, t, scale):
    i = pl.program_id(2)
    q = q_ref[0]
    row = lax.broadcasted_iota(jnp.int32, (t, t), 0)
    col = lax.broadcasted_iota(jnp.int32, (t, t), 1)
    later = (row > col).astype(BF16)
    causal = col < row

    def block(j, carry, acc, diagonal):
        kb = k_ref[0, pl.ds(pl.multiple_of(j * t, t), t), :]
        vb = v_ref[0, pl.ds(pl.multiple_of(j * t, t), t), :]
        z = _dot_nt(q, kb) * scale
        sp = jnp.maximum(z, 0.0) + jnp.log1p(jnp.exp(-jnp.abs(z)))
        lom = -sp
        if diagonal:
            lom = jnp.where(causal, lom, 0.0)
        hi = lom.astype(BF16)
        lo = (lom - hi.astype(F32)).astype(BF16)
        after = _dot(hi, later) + _dot(lo, later) + carry
        a = jnp.exp((z - sp) + after)
        if diagonal:
            a = jnp.where(causal, a, 0.0)
        acc = acc + _dot(a.astype(BF16), vb)
        carry = carry + jnp.sum(lom, axis=1, keepdims=True)
        return carry, acc

    carry, acc = block(i, jnp.zeros((t, 1), F32), jnp.zeros((t, HEAD_DIM), F32), True)

    def cond(s):
        j, carry, _ = s
        return jnp.logical_and(j >= 0, jnp.max(carry) > -SB_SKIP)

    def body(s):
        j, carry, acc = s
        carry, acc = block(j, carry, acc, False)
        return j - 1, carry, acc

    _, _, acc = lax.while_loop(cond, body, (i - 1, carry, acc))
    o_ref[0] = acc.astype(o_ref.dtype)


def _sb_attention(qkv, n_heads, *, t=256):
    B, S, _ = qkv.shape
    d = HEAD_DIM
    assert S % t == 0
    return pl.pallas_call(
        functools.partial(_sb_kernel, t=t, scale=d ** -0.5),
        grid=(B, n_heads, S // t),
        in_specs=[pl.BlockSpec((1, t, d), lambda b, h, i: (b, i, h)),
                  pl.BlockSpec((1, S, d), lambda b, h, i: (b, 0, n_heads + h)),
                  pl.BlockSpec((1, S, d), lambda b, h, i: (b, 0, 2 * n_heads + h))],
        out_specs=pl.BlockSpec((1, t, d), lambda b, h, i: (b, i, h)),
        out_shape=jax.ShapeDtypeStruct((B, S, n_heads * d), BF16),
        compiler_params=_params("parallel", "parallel", "arbitrary"),
    )(qkv, qkv, qkv)


HG_CHUNK = 64
HG_SUB = 16


def _hg_kernel(q_ref, fz_ref, v_ref, g_ref, lb_ref, gn_ref, o_ref, state_ref, oc_ref, *, n_chunks):
    C, U = HG_CHUNK, HG_SUB

    @pl.when(pl.program_id(2) == 0)
    def _():
        state_ref[...] = jnp.zeros_like(state_ref)

    lb = lb_ref[...]
    gn = gn_ref[...]
    row = lax.broadcasted_iota(jnp.int32, (C, C), 0)
    col = lax.broadcasted_iota(jnp.int32, (C, C), 1)
    tril = (col <= row).astype(BF16)
    sub_row = lax.broadcasted_iota(jnp.int32, (U, 1), 0)

    def chunk(c, _):
        r0 = pl.multiple_of(c * C, C)
        q = q_ref[0, pl.ds(r0, C), :]
        fz = fz_ref[0, pl.ds(r0, C), :]
        v = v_ref[0, pl.ds(r0, C), :]
        e = jnp.exp(-jnp.abs(fz))
        inv = 1.0 / (1.0 + e)
        sig_pos = jnp.where(fz >= 0, inv, e * inv)
        sig_neg = jnp.where(fz >= 0, e * inv, inv)
        log_f = jnp.log(lb + (1.0 - lb) * sig_pos)
        k = (1.0 - lb) * sig_neg
        l_hi, l_mid, l_lo = _split3(log_f)
        b = _dot(tril, l_hi) + _dot(tril, l_mid) + _dot(tril, l_lo)
        state_t = state_ref[...]
        oc_ref[...] = _dot_nt((q * jnp.exp(b)).astype(BF16), state_t.astype(BF16))
        v16 = v.astype(BF16)
        for J in range(C // U - 1):
            lo_r, hi_r = J * U, (J + 1) * U
            ref_b = b[hi_r - 1:hi_r, :]
            kt = (k[lo_r:hi_r] * jnp.exp(ref_b - b[lo_r:hi_r])).astype(BF16)
            qs = (q[hi_r:] * jnp.exp(b[hi_r:] - ref_b)).astype(BF16)
            s = _dot_nt(qs, kt)
            oc_ref[hi_r:, :] += _dot(s.astype(BF16), v16[lo_r:hi_r])
        for J in range(C // U):
            lo_r, hi_r = J * U, (J + 1) * U
            qj, bj, kj, vj = q[lo_r:hi_r], b[lo_r:hi_r], k[lo_r:hi_r], v[lo_r:hi_r]
            acc = jnp.zeros((U, HEAD_DIM), F32)
            for s_i in range(U):
                w = qj * kj[s_i:s_i + 1] * jnp.exp(jnp.minimum(bj - bj[s_i:s_i + 1], 0.0))
                sc = jnp.sum(w, axis=1, keepdims=True)
                sc = jnp.where(sub_row >= s_i, sc, 0.0)
                acc = acc + sc * vj[s_i:s_i + 1]
            oc_ref[lo_r:hi_r, :] += acc
        b_last = b[C - 1:C, :]
        kd = (k * jnp.exp(b_last - b)).astype(BF16)
        state_ref[...] = state_t * jnp.exp(b_last) + _dot_tn(v16, kd)
        o = oc_ref[...]
        o = o * lax.rsqrt(jnp.mean(o * o, axis=-1, keepdims=True) + RMS_EPS) * gn
        g = g_ref[0, pl.ds(r0, C), :]
        o = o * (g / (1.0 + jnp.exp(-g)))
        o_ref[0, pl.ds(r0, C), :] = o.astype(o_ref.dtype)
        return 0

    lax.fori_loop(0, n_chunks, chunk, 0)


def _hgrn2(proj, lb, g_norm, n_heads, *, ts=512):
    B, S, _ = proj.shape
    d = HEAD_DIM
    assert S % ts == 0 and ts % HG_CHUNK == 0
    H = n_heads

    def col(base):
        return pl.BlockSpec((1, ts, d), lambda b, h, s: (b, s, base + h))

    return pl.pallas_call(
        functools.partial(_hg_kernel, n_chunks=ts // HG_CHUNK),
        grid=(B, H, S // ts),
        in_specs=[col(0), col(H), col(2 * H), col(3 * H),
                  pl.BlockSpec((1, d), lambda b, h, s: (0, h)),
                  pl.BlockSpec((1, d), lambda b, h, s: (0, 0))],
        out_specs=pl.BlockSpec((1, ts, d), lambda b, h, s: (b, s, h)),
        out_shape=jax.ShapeDtypeStruct((B, S, H * d), BF16),
        scratch_shapes=[pltpu.VMEM((d, d), F32), pltpu.VMEM((HG_CHUNK, d), F32)],
        compiler_params=_params("parallel", "parallel", "arbitrary"),
    )(proj, proj, proj, proj, lb.reshape(1, H * d), g_norm.reshape(1, d))


def _rope_kernel(pos_ref, invf_ref, x_ref, kpe_ref, cos_ref, sin_ref):
    half = MLA_ROPE // 2
    ang = pos_ref[...].astype(F32) * invf_ref[...]
    lane = lax.broadcasted_iota(jnp.int32, ang.shape, 1)
    c = jnp.where(lane < MLA_ROPE, jnp.cos(ang), 0.0)
    s = jnp.sin(ang)
    s = jnp.where(lane < half, -s, jnp.where(lane < MLA_ROPE, s, 0.0))
    cos_ref[...] = c
    sin_ref[...] = s
    x = x_ref[...]
    swapped = pltpu.roll(x, half, 1) + pltpu.roll(x, HEAD_DIM - half, 1)
    kpe_ref[...] = (x * c + swapped * s).astype(kpe_ref.dtype)


def _rope_tables(positions, dkv, kpe_col, *, tm=512):
    M = dkv.shape[0]
    half = MLA_ROPE // 2
    freq = ROPE_THETA ** (-jnp.arange(0, MLA_ROPE, 2, dtype=F32) / MLA_ROPE)
    invf = jnp.concatenate([freq, freq, jnp.zeros((HEAD_DIM - 2 * half,), F32)]).reshape(1, HEAD_DIM)
    row = pl.BlockSpec((tm, HEAD_DIM), lambda i: (i, 0))
    return pl.pallas_call(
        _rope_kernel,
        grid=(M // tm,),
        in_specs=[pl.BlockSpec((tm, 1), lambda i: (i, 0)),
                  pl.BlockSpec((1, HEAD_DIM), lambda i: (0, 0)),
                  pl.BlockSpec((tm, HEAD_DIM), lambda i: (i, kpe_col))],
        out_specs=[row, row, row],
        out_shape=[jax.ShapeDtypeStruct((M, HEAD_DIM), BF16),
                   jax.ShapeDtypeStruct((M, HEAD_DIM), F32),
                   jax.ShapeDtypeStruct((M, HEAD_DIM), F32)],
        compiler_params=_params("parallel"),
    )(positions.reshape(M, 1), invf, dkv)


def _mla_kernel(q_ref, cos_ref, sin_ref, kn_ref, v_ref, kpe_ref, o_ref, *, t, scale):
    i = pl.program_id(2)
    half = MLA_ROPE // 2
    qn = q_ref[0, :, :MLA_NOPE]
    qr = q_ref[0, :, MLA_NOPE:].astype(F32)
    swapped = pltpu.roll(qr, half, 1) + pltpu.roll(qr, HEAD_DIM - half, 1)
    qp = (qr * cos_ref[0] + swapped * sin_ref[0]).astype(BF16)
    row = lax.broadcasted_iota(jnp.int32, (t, t), 0) // MLA_CHUNK
    col = lax.broadcasted_iota(jnp.int32, (t, t), 1) // MLA_CHUNK
    visible = col <= row

    def block(j, m, l, acc, diagonal):
        r0 = pl.multiple_of(j * t, t)
        s = (_dot_nt(qn, kn_ref[0, pl.ds(r0, t), :]) + _dot_nt(qp, kpe_ref[0, pl.ds(r0, t), :])) * scale
        if diagonal:
            s = jnp.where(visible, s, -jnp.inf)
        m_new = jnp.maximum(m, jnp.max(s, axis=1, keepdims=True))
        alpha = jnp.exp(m - m_new)
        p = jnp.exp(s - m_new)
        l = alpha * l + jnp.sum(p, axis=1, keepdims=True)
        acc = alpha * acc + _dot(p.astype(BF16), v_ref[0, pl.ds(r0, t), :])
        return m_new, l, acc

    def body(j, s):
        return block(j, *s, False)

    init = (jnp.full((t, 1), -jnp.inf, F32), jnp.zeros((t, 1), F32), jnp.zeros((t, HEAD_DIM), F32))
    m, l, acc = lax.fori_loop(0, i, body, init)
    m, l, acc = block(i, m, l, acc, True)
    o_ref[0] = (acc / l).astype(o_ref.dtype)


def _mla_attention(q, cos, sin, kv, kpe, n_heads, *, t=256):
    B, S, _ = q.shape
    d = HEAD_DIM
    assert S % t == 0 and t % MLA_CHUNK == 0
    return pl.pallas_call(
        functools.partial(_mla_kernel, t=t, scale=(MLA_NOPE + MLA_ROPE) ** -0.5),
        grid=(B, n_heads, S // t),
        in_specs=[pl.BlockSpec((1, t, 2 * d), lambda b, h, i: (b, i, h)),
                  pl.BlockSpec((1, t, d), lambda b, h, i: (b, i, 0)),
                  pl.BlockSpec((1, t, d), lambda b, h, i: (b, i, 0)),
                  pl.BlockSpec((1, S, d), lambda b, h, i: (b, 0, 2 * h)),
                  pl.BlockSpec((1, S, d), lambda b, h, i: (b, 0, 2 * h + 1)),
                  pl.BlockSpec((1, S, d), lambda b, h, i: (b, 0, 0))],
        out_specs=pl.BlockSpec((1, t, d), lambda b, h, i: (b, i, h)),
        out_shape=jax.ShapeDtypeStruct((B, S, n_heads * d), BF16),
        compiler_params=_params("parallel", "parallel", "arbitrary"),
    )(q, cos, sin, kv, kv, kpe)


def _sb_layer(h, gain, w_qkv, w_o, B, S):
    M, D = h.shape
    H = w_o.shape[0] // HEAD_DIM
    qkv = _linear(h, w_qkv.astype(BF16), gain=gain, out_dtype=BF16)
    o = _sb_attention(qkv.reshape(B, S, -1), H)
    return _linear(o.reshape(M, -1), w_o.astype(BF16), residual=h)


def _hg_layer(h, gain, w_in, g_norm, w_o, lb, B, S):
    M, D = h.shape
    H = w_o.shape[0] // HEAD_DIM
    proj = _linear(h, w_in.astype(BF16), gain=gain, out_dtype=F32)
    o = _hgrn2(proj.reshape(B, S, -1), lb, g_norm, H)
    return _linear(o.reshape(M, -1), w_o.astype(BF16), residual=h)


def _mla_layer(h, positions, gain, w_dkv, q_norm, kv_norm, w_uq, w_ukv, w_o, B, S):
    M, D = h.shape
    H = w_o.shape[0] // MLA_NOPE
    qr, kvr = MLA_Q_RANK, MLA_KV_RANK
    pad_a = jnp.zeros((D, qr - kvr), F32)
    pad_b = jnp.zeros((D, HEAD_DIM - MLA_ROPE), F32)
    w_d = jnp.concatenate([w_dkv[:, qr:qr + kvr], pad_a, w_dkv[:, :qr], w_dkv[:, qr + kvr:], pad_b], axis=1)
    dkv = _linear(h, w_d.astype(BF16), gain=gain, out_dtype=F32, tn=w_d.shape[1])
    kpe, cos, sin = _rope_tables(positions, dkv, (2 * qr) // HEAD_DIM)
    w_q = w_uq.reshape(qr, H, MLA_NOPE + MLA_ROPE)
    w_q = jnp.concatenate([w_q, jnp.zeros((qr, H, HEAD_DIM - MLA_ROPE), F32)], axis=2).reshape(qr, H * 2 * HEAD_DIM)
    q = _linear(dkv, w_q.astype(BF16), gain=q_norm, out_dtype=BF16, x_col=1)
    kv = _linear(dkv, w_ukv.astype(BF16), gain=kv_norm, out_dtype=BF16, x_col=0)
    o = _mla_attention(q.reshape(B, S, -1), cos.reshape(B, S, -1), sin.reshape(B, S, -1),
                       kv.reshape(B, S, -1), kpe.reshape(B, S, -1), H)
    return _linear(o.reshape(M, -1), w_o.astype(BF16), residual=h)


def _final_norm_kernel(x_ref, g_ref, o_ref):
    x = x_ref[...]
    ms = jnp.mean(x * x, axis=-1, keepdims=True)
    o_ref[...] = x * lax.rsqrt(ms + RMS_EPS) * g_ref[...]


def _final_norm(h, gain, *, tm=512):
    M, D = h.shape
    return pl.pallas_call(
        _final_norm_kernel,
        grid=(M // tm,),
        in_specs=[pl.BlockSpec((tm, D), lambda i: (i, 0)), pl.BlockSpec((1, D), lambda i: (0, 0))],
        out_specs=pl.BlockSpec((tm, D), lambda i: (i, 0)),
        out_shape=jax.ShapeDtypeStruct((M, D), F32),
        compiler_params=_params("parallel"),
    )(h, gain.reshape(1, D))


def kernel(x, positions, norm_mix, norm_mlp, final_norm, sb_w_qkv, sb_w_o, hg_w_in, hg_lb_logits, hg_g_norm, hg_w_o, mla_w_dkv, mla_q_norm, mla_kv_norm, mla_w_uq, mla_w_ukv, mla_w_o, mlp_w1, mlp_w2):
    B, S, D = x.shape
    depth = norm_mix.shape[0]
    p_lb = jax.nn.softmax(hg_lb_logits.astype(F32), axis=0)
    lb_all = jnp.cumsum(p_lb, axis=0) - p_lb[0]
    h = x.reshape(B * S, D)
    for i in range(depth):
        m, j = i % N_MIXERS, i // N_MIXERS
        if m == 0:
            h = _sb_layer(h, norm_mix[i], sb_w_qkv[j], sb_w_o[j], B, S)
        elif m == 1:
            h = _hg_layer(h, norm_mix[i], hg_w_in[j], hg_g_norm[j], hg_w_o[j], lb_all[i], B, S)
        else:
            h = _mla_layer(h, positions, norm_mix[i], mla_w_dkv[j], mla_q_norm[j], mla_kv_norm[j],
                           mla_w_uq[j], mla_w_ukv[j], mla_w_o[j], B, S)
        h = _mlp(h, norm_mlp[i], mlp_w1[i].astype(BF16), mlp_w2[i].astype(BF16))
    return _final_norm(h, final_norm).reshape(B, S, D)
```

```python
import functools

import jax
import jax.numpy as jnp
from jax import lax
from jax.experimental import pallas as pl
from jax.experimental.pallas import tpu as pltpu

F32 = jnp.float32
BF16 = jnp.bfloat16

RMS_EPS = 1e-6
N_MIXERS = 3
HEAD_DIM = 128
MLA_NOPE = 128
MLA_ROPE = 64
MLA_Q_RANK = 768
MLA_KV_RANK = 512
MLA_CHUNK = 64
ROPE_THETA = 10000.0
LOG2E = 1.4426950408889634
VMEM_LIMIT = 56 * 1024 * 1024


def _params(*sem):
    return pltpu.CompilerParams(dimension_semantics=sem, vmem_limit_bytes=VMEM_LIMIT)


def _dot(a, b):
    return jnp.dot(a, b, preferred_element_type=F32)


def _dot_nt(a, b):
    return lax.dot_general(a, b, (((1,), (1,)), ((), ())), preferred_element_type=F32)


def _dot_tn(a, b):
    return lax.dot_general(a, b, (((0,), (0,)), ((), ())), preferred_element_type=F32)


def _split3(x):
    hi = x.astype(BF16)
    r1 = x - hi.astype(F32)
    mid = r1.astype(BF16)
    lo = (r1 - mid.astype(F32)).astype(BF16)
    return hi, mid, lo


def _linear_kernel(*refs, has_gain, has_res):
    x_ref, w_ref = refs[0], refs[1]
    pos = 2
    g_ref = r_ref = None
    if has_gain:
        g_ref = refs[pos]
        pos += 1
    if has_res:
        r_ref = refs[pos]
        pos += 1
    o_ref, xn_ref = refs[pos], refs[pos + 1]

    @pl.when(pl.program_id(1) == 0)
    def _():
        x = x_ref[...].astype(F32)
        if has_gain:
            ms = jnp.mean(x * x, axis=-1, keepdims=True)
            x = x * lax.rsqrt(ms + RMS_EPS) * g_ref[...]
        xn_ref[...] = x.astype(BF16)

    acc = _dot(xn_ref[...], w_ref[...])
    if has_res:
        acc = acc + r_ref[...]
    o_ref[...] = acc.astype(o_ref.dtype)


def _linear(x, w, *, name, gain=None, residual=None, out_dtype=F32, x_col=0, tm=512, tn=512):
    M = x.shape[0]
    K, N = w.shape
    tm = min(tm, M)
    tn = min(tn, N)
    assert M % tm == 0 and N % tn == 0
    in_specs = [pl.BlockSpec((tm, K), lambda i, j: (i, x_col)),
                pl.BlockSpec((K, tn), lambda i, j: (0, j))]
    args = [x, w]
    if gain is not None:
        in_specs.append(pl.BlockSpec((1, K), lambda i, j: (0, 0)))
        args.append(gain.reshape(1, K).astype(F32))
    if residual is not None:
        in_specs.append(pl.BlockSpec((tm, tn), lambda i, j: (i, j)))
        args.append(residual)
    return pl.pallas_call(
        functools.partial(_linear_kernel, has_gain=gain is not None, has_res=residual is not None),
        grid=(M // tm, N // tn),
        in_specs=in_specs,
        out_specs=pl.BlockSpec((tm, tn), lambda i, j: (i, j)),
        out_shape=jax.ShapeDtypeStruct((M, N), out_dtype),
        scratch_shapes=[pltpu.VMEM((tm, K), BF16)],
        compiler_params=_params("parallel", "arbitrary"),
        name=name,
    )(*args)


def _mlp_kernel(x_ref, g_ref, w1_ref, w2_ref, o_ref, xn_ref, acc_ref):
    f = pl.program_id(1)

    @pl.when(f == 0)
    def _():
        x = x_ref[...]
        ms = jnp.mean(x * x, axis=-1, keepdims=True)
        xn_ref[...] = (x * lax.rsqrt(ms + RMS_EPS) * g_ref[...]).astype(BF16)
        acc_ref[...] = jnp.zeros_like(acc_ref)

    u = _dot(xn_ref[...], w1_ref[...])
    r = jnp.square(jnp.maximum(u, 0.0)).astype(BF16)
    acc_ref[...] += _dot(r, w2_ref[...])

    @pl.when(f == pl.num_programs(1) - 1)
    def _():
        o_ref[...] = x_ref[...] + acc_ref[...]


def _mlp(h, gain, w1, w2, *, tm=512, tf=512):
    M, D = h.shape
    F = w1.shape[1]
    assert M % tm == 0 and F % tf == 0
    return pl.pallas_call(
        _mlp_kernel,
        grid=(M // tm, F // tf),
        in_specs=[pl.BlockSpec((tm, D), lambda i, f: (i, 0)),
                  pl.BlockSpec((1, D), lambda i, f: (0, 0)),
                  pl.BlockSpec((D, tf), lambda i, f: (0, f)),
                  pl.BlockSpec((tf, D), lambda i, f: (f, 0))],
        out_specs=pl.BlockSpec((tm, D), lambda i, f: (i, 0)),
        out_shape=jax.ShapeDtypeStruct((M, D), F32),
        scratch_shapes=[pltpu.VMEM((tm, D), BF16), pltpu.VMEM((tm, D), F32)],
        compiler_params=_params("parallel", "arbitrary"),
        name="mlp",
    )(h, gain.reshape(1, D).astype(F32), w1, w2)


SB_SKIP = 110.0


def _sb_kernel(q_ref, k_ref, v_ref, o_ref, *, t):
    i = pl.program_id(2)
    q = q_ref[0]
    row = lax.broadcasted_iota(jnp.int32, (t, t), 0)
    col = lax.broadcasted_iota(jnp.int32, (t, t), 1)
    later = (row > col).astype(BF16)
    causal = col < row

    def gates(j, diagonal):
        kb = k_ref[0, pl.ds(pl.multiple_of(j * t, t), t), :]
        z = _dot_nt(q, kb)
        nz = -z
        lom = jnp.minimum(nz, 0.0) - jnp.log2(1.0 + jnp.exp2(jnp.minimum(z, nz)))
        log_beta = lom + z
        if diagonal:
            lom = jnp.where(causal, lom, 0.0)
        hi = lom.astype(BF16)
        lo = (lom - hi.astype(F32)).astype(BF16)
        within = _dot(hi, later) + _dot(lo, later)
        return log_beta, within, jnp.sum(lom, axis=1, keepdims=True)

    def weigh(j, log_beta, after, diagonal):
        a = jnp.exp2(log_beta + after)
        if diagonal:
            a = jnp.where(causal, a, 0.0)
        return _dot(a.astype(BF16), v_ref[0, pl.ds(pl.multiple_of(j * t, t), t), :])

    @pl.when(i == 0)
    def _():
        log_beta, within, _ = gates(0, True)
        o_ref[0] = weigh(0, log_beta, within, True).astype(o_ref.dtype)

    @pl.when(i > 0)
    def _():
        lb_d, within_d, carry_d = gates(i, True)
        lb_p, within_p, total_p = gates(i - 1, False)
        acc = weigh(i, lb_d, within_d, True) + weigh(i - 1, lb_p, within_p + carry_d, False)

        def cond(s):
            j, carry, _ = s
            return jnp.logical_and(j >= 0, jnp.max(carry) > -SB_SKIP * LOG2E)

        def body(s):
            j, carry, acc = s
            log_beta, within, total = gates(j, False)
            return j - 1, carry + total, acc + weigh(j, log_beta, within + carry, False)

        _, _, acc = lax.while_loop(cond, body, (i - 2, carry_d + total_p, acc))
        o_ref[0] = acc.astype(o_ref.dtype)


def _sb_attention(qkv, n_heads, *, t=256):
    B, S, _ = qkv.shape
    d = HEAD_DIM
    assert S % t == 0
    return pl.pallas_call(
        functools.partial(_sb_kernel, t=t),
        grid=(B, n_heads, S // t),
        in_specs=[pl.BlockSpec((1, t, d), lambda b, h, i: (b, i, h)),
                  pl.BlockSpec((1, S, d), lambda b, h, i: (b, 0, n_heads + h)),
                  pl.BlockSpec((1, S, d), lambda b, h, i: (b, 0, 2 * n_heads + h))],
        out_specs=pl.BlockSpec((1, t, d), lambda b, h, i: (b, i, h)),
        out_shape=jax.ShapeDtypeStruct((B, S, n_heads * d), BF16),
        compiler_params=_params("parallel", "parallel", "arbitrary"),
        name="sb_attn",
    )(qkv, qkv, qkv)


HG_CHUNK = 64
HG_SUB = 16


HG_GROUP = 2


def _hg_kernel(q_ref, fz_ref, v_ref, g_ref, lb_ref, gn_ref, o_ref, state_ref, oc_ref, *, n_chunks):
    C, U, d = HG_CHUNK, HG_SUB, HEAD_DIM

    @pl.when(pl.program_id(2) == 0)
    def _():
        state_ref[...] = jnp.zeros_like(state_ref)

    gn = gn_ref[...]
    row = lax.broadcasted_iota(jnp.int32, (C, C), 0)
    col = lax.broadcasted_iota(jnp.int32, (C, C), 1)
    tril = (col <= row).astype(BF16)
    sub_row = lax.broadcasted_iota(jnp.int32, (U, 1), 0)

    def head_chunk(hd, r0):
        lanes = slice(hd * d, (hd + 1) * d)
        lb = lb_ref[:, lanes]
        q = q_ref[0, pl.ds(r0, C), lanes]
        fz = fz_ref[0, pl.ds(r0, C), lanes]
        v = v_ref[0, pl.ds(r0, C), lanes]
        e = jnp.exp(-jnp.abs(fz))
        inv = 1.0 / (1.0 + e)
        sig_pos = jnp.where(fz >= 0, inv, e * inv)
        sig_neg = jnp.where(fz >= 0, e * inv, inv)
        log_f = jnp.log(lb + (1.0 - lb) * sig_pos)
        k = (1.0 - lb) * sig_neg
        log2_k = (jnp.log(1.0 - lb) + jnp.minimum(-fz, 0.0) - jnp.log(1.0 + e)) * LOG2E
        l_hi, l_mid, l_lo = _split3(log_f)
        b = _dot(tril, l_hi) + _dot(tril, l_mid) + _dot(tril, l_lo)
        state_t = state_ref[hd]
        oc_ref[hd] = _dot_nt((q * jnp.exp(b)).astype(BF16), state_t.astype(BF16))
        v16 = v.astype(BF16)
        for J in range(C // U - 1):
            lo_r, hi_r = J * U, (J + 1) * U
            ref_b = b[hi_r - 1:hi_r, :]
            kt = (k[lo_r:hi_r] * jnp.exp(ref_b - b[lo_r:hi_r])).astype(BF16)
            qs = (q[hi_r:] * jnp.exp(b[hi_r:] - ref_b)).astype(BF16)
            s = _dot_nt(qs, kt)
            oc_ref[hd, hi_r:, :] += _dot(s.astype(BF16), v16[lo_r:hi_r])
        b2 = b * LOG2E
        key2 = log2_k - b2
        for J in range(C // U):
            lo_r, hi_r = J * U, (J + 1) * U
            qj, bj, kj, vj = q[lo_r:hi_r], b2[lo_r:hi_r], key2[lo_r:hi_r], v[lo_r:hi_r]
            acc = jnp.zeros((U, d), F32)
            for s_i in range(U):
                w = qj * jnp.exp2(jnp.minimum(bj + kj[s_i:s_i + 1], 0.0))
                sc = jnp.sum(w, axis=1, keepdims=True)
                sc = jnp.where(sub_row >= s_i, sc, 0.0)
                acc = acc + sc * vj[s_i:s_i + 1]
            oc_ref[hd, lo_r:hi_r, :] += acc
        b_last = b[C - 1:C, :]
        kd = (k * jnp.exp(b_last - b)).astype(BF16)
        state_ref[hd] = state_t * jnp.exp(b_last) + _dot_tn(v16, kd)
        o = oc_ref[hd]
        o = o * lax.rsqrt(jnp.mean(o * o, axis=-1, keepdims=True) + RMS_EPS) * gn
        g = g_ref[0, pl.ds(r0, C), lanes]
        o = o * (g / (1.0 + jnp.exp(-g)))
        o_ref[0, pl.ds(r0, C), lanes] = o.astype(o_ref.dtype)

    def chunk(c, _):
        r0 = pl.multiple_of(c * C, C)
        for hd in range(HG_GROUP):
            head_chunk(hd, r0)
        return 0

    lax.fori_loop(0, n_chunks, chunk, 0)


def _hgrn2(proj, lb, g_norm, n_heads, *, ts=512):
    B, S, _ = proj.shape
    d, G = HEAD_DIM, HG_GROUP
    assert S % ts == 0 and ts % HG_CHUNK == 0 and n_heads % G == 0
    n_groups = n_heads // G

    def col(part):
        return pl.BlockSpec((1, ts, G * d), lambda b, h, s: (b, s, part * n_groups + h))

    return pl.pallas_call(
        functools.partial(_hg_kernel, n_chunks=ts // HG_CHUNK),
        grid=(B, n_groups, S // ts),
        in_specs=[col(0), col(1), col(2), col(3),
                  pl.BlockSpec((1, G * d), lambda b, h, s: (0, h)),
                  pl.BlockSpec((1, d), lambda b, h, s: (0, 0))],
        out_specs=pl.BlockSpec((1, ts, G * d), lambda b, h, s: (b, s, h)),
        out_shape=jax.ShapeDtypeStruct((B, S, n_heads * d), BF16),
        scratch_shapes=[pltpu.VMEM((G, d, d), F32), pltpu.VMEM((G, HG_CHUNK, d), F32)],
        compiler_params=_params("parallel", "parallel", "arbitrary"),
        name="hgrn2",
    )(proj, proj, proj, proj, lb.reshape(1, n_heads * d), g_norm.reshape(1, d))


def _rope_kernel(pos_ref, invf_ref, x_ref, kpe_ref, cos_ref, sin_ref):
    half = MLA_ROPE // 2
    ang = pos_ref[...].astype(F32) * invf_ref[...]
    lane = lax.broadcasted_iota(jnp.int32, ang.shape, 1)
    c = jnp.where(lane < MLA_ROPE, jnp.cos(ang), 0.0)
    s = jnp.sin(ang)
    s = jnp.where(lane < half, -s, jnp.where(lane < MLA_ROPE, s, 0.0))
    cos_ref[...] = c
    sin_ref[...] = s
    x = x_ref[...]
    swapped = pltpu.roll(x, half, 1) + pltpu.roll(x, HEAD_DIM - half, 1)
    kpe_ref[...] = (x * c + swapped * s).astype(kpe_ref.dtype)


def _rope_tables(positions, dkv, kpe_col, *, tm=512):
    M = dkv.shape[0]
    half = MLA_ROPE // 2
    freq = ROPE_THETA ** (-jnp.arange(0, MLA_ROPE, 2, dtype=F32) / MLA_ROPE)
    invf = jnp.concatenate([freq, freq, jnp.zeros((HEAD_DIM - 2 * half,), F32)]).reshape(1, HEAD_DIM)
    row = pl.BlockSpec((tm, HEAD_DIM), lambda i: (i, 0))
    return pl.pallas_call(
        _rope_kernel,
        grid=(M // tm,),
        in_specs=[pl.BlockSpec((tm, 1), lambda i: (i, 0)),
                  pl.BlockSpec((1, HEAD_DIM), lambda i: (0, 0)),
                  pl.BlockSpec((tm, HEAD_DIM), lambda i: (i, kpe_col))],
        out_specs=[row, row, row],
        out_shape=[jax.ShapeDtypeStruct((M, HEAD_DIM), BF16),
                   jax.ShapeDtypeStruct((M, HEAD_DIM), F32),
                   jax.ShapeDtypeStruct((M, HEAD_DIM), F32)],
        compiler_params=_params("parallel"),
        name="mla_rope",
    )(positions.reshape(M, 1), invf, dkv)


def _mla_kernel(q_ref, cos_ref, sin_ref, kn_ref, v_ref, kpe_ref, o_ref, kcat_ref, sa_ref, sb_ref, *, t):
    i = pl.program_id(2)
    half = MLA_ROPE // 2

    @pl.when(i == 0)
    def _():
        kcat_ref[:, :MLA_NOPE] = kn_ref[0]
        kcat_ref[:, MLA_NOPE:] = kpe_ref[0]

    qn = q_ref[0, :, :MLA_NOPE]
    qr = q_ref[0, :, MLA_NOPE:].astype(F32)
    swapped = pltpu.roll(qr, half, 1) + pltpu.roll(qr, HEAD_DIM - half, 1)
    qp = (qr * cos_ref[0] + swapped * sin_ref[0]).astype(BF16)
    q = jnp.concatenate([qn, qp], axis=1)
    row = lax.broadcasted_iota(jnp.int32, (t, t), 0) // MLA_CHUNK
    col = lax.broadcasted_iota(jnp.int32, (t, t), 1) // MLA_CHUNK
    visible = col <= row

    def scores(j, dst_ref):
        dst_ref[...] = _dot_nt(q, kcat_ref[pl.ds(pl.multiple_of(j * t, t), t), :])

    def absorb(j, src_ref, m, l, acc, diagonal):
        s = src_ref[...]
        if diagonal:
            s = jnp.where(visible, s, -jnp.inf)
        m_new = jnp.maximum(m, jnp.max(s, axis=1, keepdims=True))
        alpha = jnp.exp2(m - m_new)
        p = jnp.exp2(s - m_new)
        l = alpha * l + jnp.sum(p, axis=1, keepdims=True)
        acc = alpha * acc + _dot(p.astype(BF16), v_ref[0, pl.ds(pl.multiple_of(j * t, t), t), :])
        return m_new, l, acc

    def pair(p, carry):
        scores(2 * p + 1, sb_ref)
        carry = absorb(2 * p, sa_ref, *carry, False)
        scores(2 * p + 2, sa_ref)
        return absorb(2 * p + 1, sb_ref, *carry, False)

    def finish(m, l, acc):
        o_ref[0] = (acc / l).astype(o_ref.dtype)

    scores(0, sa_ref)
    init = (jnp.full((t, 1), -jnp.inf, F32), jnp.zeros((t, 1), F32), jnp.zeros((t, HEAD_DIM), F32))
    carry = lax.fori_loop(0, i // 2, pair, init)

    @pl.when(i % 2 == 0)
    def _():
        finish(*absorb(i, sa_ref, *carry, True))

    @pl.when(i % 2 == 1)
    def _():
        scores(i, sb_ref)
        finish(*absorb(i, sb_ref, *absorb(i - 1, sa_ref, *carry, False), True))


def _mla_attention(q, cos, sin, kv, kpe, n_heads, *, t=512):
    B, S, _ = q.shape
    d = HEAD_DIM
    assert S % t == 0 and t % MLA_CHUNK == 0
    return pl.pallas_call(
        functools.partial(_mla_kernel, t=t),
        grid=(B, n_heads, S // t),
        in_specs=[pl.BlockSpec((1, t, 2 * d), lambda b, h, i: (b, i, h)),
                  pl.BlockSpec((1, t, d), lambda b, h, i: (b, i, 0)),
                  pl.BlockSpec((1, t, d), lambda b, h, i: (b, i, 0)),
                  pl.BlockSpec((1, S, d), lambda b, h, i: (b, 0, 2 * h)),
                  pl.BlockSpec((1, S, d), lambda b, h, i: (b, 0, 2 * h + 1)),
                  pl.BlockSpec((1, S, d), lambda b, h, i: (b, 0, 0))],
        out_specs=pl.BlockSpec((1, t, d), lambda b, h, i: (b, i, h)),
        out_shape=jax.ShapeDtypeStruct((B, S, n_heads * d), BF16),
        scratch_shapes=[pltpu.VMEM((S, 2 * d), BF16), pltpu.VMEM((t, t), F32), pltpu.VMEM((t, t), F32)],
        compiler_params=_params("parallel", "parallel", "arbitrary"),
        name="mla_attn",
    )(q, cos, sin, kv, kv, kpe)


def _sb_layer(h, gain, w_qkv, w_o, B, S):
    M, D = h.shape
    H = w_o.shape[0] // HEAD_DIM
    col_scale = jnp.where(jnp.arange(w_qkv.shape[1]) < H * HEAD_DIM, HEAD_DIM ** -0.5 * LOG2E, 1.0)
    qkv = _linear(h, (w_qkv * col_scale).astype(BF16), name="sb_qkv", gain=gain, out_dtype=BF16)
    o = _sb_attention(qkv.reshape(B, S, -1), H)
    return _linear(o.reshape(M, -1), w_o.astype(BF16), name="sb_out", residual=h)


def _hg_layer(h, gain, w_in, g_norm, w_o, lb, B, S):
    M, D = h.shape
    H = w_o.shape[0] // HEAD_DIM
    proj = _linear(h, w_in.astype(BF16), name="hg_in", gain=gain, out_dtype=F32)
    o = _hgrn2(proj.reshape(B, S, -1), lb, g_norm, H)
    return _linear(o.reshape(M, -1), w_o.astype(BF16), name="hg_out", residual=h)


def _mla_layer(h, positions, gain, w_dkv, q_norm, kv_norm, w_uq, w_ukv, w_o, B, S):
    M, D = h.shape
    H = w_o.shape[0] // MLA_NOPE
    qr, kvr = MLA_Q_RANK, MLA_KV_RANK
    pad_a = jnp.zeros((D, qr - kvr), F32)
    pad_b = jnp.zeros((D, HEAD_DIM - MLA_ROPE), F32)
    w_d = jnp.concatenate([w_dkv[:, qr:qr + kvr], pad_a, w_dkv[:, :qr], w_dkv[:, qr + kvr:], pad_b], axis=1)
    dkv = _linear(h, w_d.astype(BF16), name="mla_down", gain=gain, out_dtype=F32, tn=w_d.shape[1])
    kpe, cos, sin = _rope_tables(positions, dkv, (2 * qr) // HEAD_DIM)
    w_q = w_uq.reshape(qr, H, MLA_NOPE + MLA_ROPE) * ((MLA_NOPE + MLA_ROPE) ** -0.5 * LOG2E)
    w_q = jnp.concatenate([w_q, jnp.zeros((qr, H, HEAD_DIM - MLA_ROPE), F32)], axis=2).reshape(qr, H * 2 * HEAD_DIM)
    q = _linear(dkv, w_q.astype(BF16), name="mla_q", gain=q_norm, out_dtype=BF16, x_col=1)
    kv = _linear(dkv, w_ukv.astype(BF16), name="mla_kv", gain=kv_norm, out_dtype=BF16, x_col=0)
    o = _mla_attention(q.reshape(B, S, -1), cos.reshape(B, S, -1), sin.reshape(B, S, -1),
                       kv.reshape(B, S, -1), kpe.reshape(B, S, -1), H)
    return _linear(o.reshape(M, -1), w_o.astype(BF16), name="mla_out", residual=h)


def _final_norm_kernel(x_ref, g_ref, o_ref):
    x = x_ref[...]
    ms = jnp.mean(x * x, axis=-1, keepdims=True)
    o_ref[...] = x * lax.rsqrt(ms + RMS_EPS) * g_ref[...]


def _final_norm(h, gain, *, tm=512):
    M, D = h.shape
    return pl.pallas_call(
        _final_norm_kernel,
        grid=(M // tm,),
        in_specs=[pl.BlockSpec((tm, D), lambda i: (i, 0)), pl.BlockSpec((1, D), lambda i: (0, 0))],
        out_specs=pl.BlockSpec((tm, D), lambda i: (i, 0)),
        out_shape=jax.ShapeDtypeStruct((M, D), F32),
        compiler_params=_params("parallel"),
        name="final_norm",
    )(h, gain.reshape(1, D))


def kernel(x, positions, norm_mix, norm_mlp, final_norm, sb_w_qkv, sb_w_o, hg_w_in, hg_lb_logits, hg_g_norm, hg_w_o, mla_w_dkv, mla_q_norm, mla_kv_norm, mla_w_uq, mla_w_ukv, mla_w_o, mlp_w1, mlp_w2):
    B, S, D = x.shape
    depth = norm_mix.shape[0]
    p_lb = jax.nn.softmax(hg_lb_logits.astype(F32), axis=0)
    lb_all = jnp.cumsum(p_lb, axis=0) - p_lb[0]
    h = x.reshape(B * S, D)
    for i in range(depth):
        m, j = i % N_MIXERS, i // N_MIXERS
        if m == 0:
            h = _sb_layer(h, norm_mix[i], sb_w_qkv[j], sb_w_o[j], B, S)
        elif m == 1:
            h = _hg_layer(h, norm_mix[i], hg_w_in[j], hg_g_norm[j], hg_w_o[j], lb_all[i], B, S)
        else:
            h = _mla_layer(h, positions, norm_mix[i], mla_w_dkv[j], mla_q_norm[j], mla_kv_norm[j],
                           mla_w_uq[j], mla_w_ukv[j], mla_w_o[j], B, S)
        h = _mlp(h, norm_mlp[i], mlp_w1[i].astype(BF16), mlp_w2[i].astype(BF16))
    return _final_norm(h, final_norm).reshape(B, S, D)
```

```python
import functools

import jax
import jax.numpy as jnp
from jax import lax
from jax.experimental import pallas as pl
from jax.experimental.pallas import tpu as pltpu

F32 = jnp.float32
BF16 = jnp.bfloat16

RMS_EPS = 1e-6
N_MIXERS = 3
HEAD_DIM = 128
MLA_NOPE = 128
MLA_ROPE = 64
MLA_Q_RANK = 768
MLA_KV_RANK = 512
MLA_CHUNK = 64
ROPE_THETA = 10000.0
LOG2E = 1.4426950408889634
VMEM_LIMIT = 56 * 1024 * 1024


def _params(*sem):
    return pltpu.CompilerParams(dimension_semantics=sem, vmem_limit_bytes=VMEM_LIMIT)


def _dot(a, b):
    return jnp.dot(a, b, preferred_element_type=F32)


def _dot_nt(a, b):
    return lax.dot_general(a, b, (((1,), (1,)), ((), ())), preferred_element_type=F32)


def _dot_tn(a, b):
    return lax.dot_general(a, b, (((0,), (0,)), ((), ())), preferred_element_type=F32)


def _split3(x):
    hi = x.astype(BF16)
    r1 = x - hi.astype(F32)
    mid = r1.astype(BF16)
    lo = (r1 - mid.astype(F32)).astype(BF16)
    return hi, mid, lo


def _linear_kernel(*refs, has_gain, has_res):
    x_ref, w_ref = refs[0], refs[1]
    pos = 2
    g_ref = r_ref = None
    if has_gain:
        g_ref = refs[pos]
        pos += 1
    if has_res:
        r_ref = refs[pos]
        pos += 1
    o_ref, xn_ref = refs[pos], refs[pos + 1]

    @pl.when(pl.program_id(1) == 0)
    def _():
        x = x_ref[...].astype(F32)
        if has_gain:
            ms = jnp.mean(x * x, axis=-1, keepdims=True)
            x = x * lax.rsqrt(ms + RMS_EPS) * g_ref[...]
        xn_ref[...] = x.astype(BF16)

    acc = _dot(xn_ref[...], w_ref[...])
    if has_res:
        acc = acc + r_ref[...]
    o_ref[...] = acc.astype(o_ref.dtype)


def _linear(x, w, *, name, gain=None, residual=None, out_dtype=F32, x_col=0, tm=512, tn=512):
    M = x.shape[0]
    K, N = w.shape
    tm = min(tm, M)
    tn = min(tn, N)
    assert M % tm == 0 and N % tn == 0
    in_specs = [pl.BlockSpec((tm, K), lambda i, j: (i, x_col)),
                pl.BlockSpec((K, tn), lambda i, j: (0, j))]
    args = [x, w]
    if gain is not None:
        in_specs.append(pl.BlockSpec((1, K), lambda i, j: (0, 0)))
        args.append(gain.reshape(1, K).astype(F32))
    if residual is not None:
        in_specs.append(pl.BlockSpec((tm, tn), lambda i, j: (i, j)))
        args.append(residual)
    return pl.pallas_call(
        functools.partial(_linear_kernel, has_gain=gain is not None, has_res=residual is not None),
        grid=(M // tm, N // tn),
        in_specs=in_specs,
        out_specs=pl.BlockSpec((tm, tn), lambda i, j: (i, j)),
        out_shape=jax.ShapeDtypeStruct((M, N), out_dtype),
        scratch_shapes=[pltpu.VMEM((tm, K), BF16)],
        compiler_params=_params("parallel", "arbitrary"),
        name=name,
    )(*args)


def _mlp_kernel(x_ref, g_ref, w1_ref, w2_ref, o_ref, xn_ref, acc_ref):
    f = pl.program_id(1)

    @pl.when(f == 0)
    def _():
        x = x_ref[...]
        ms = jnp.mean(x * x, axis=-1, keepdims=True)
        xn_ref[...] = (x * lax.rsqrt(ms + RMS_EPS) * g_ref[...]).astype(BF16)
        acc_ref[...] = jnp.zeros_like(acc_ref)

    u = _dot(xn_ref[...], w1_ref[...])
    r = jnp.square(jnp.maximum(u, 0.0)).astype(BF16)
    acc_ref[...] += _dot(r, w2_ref[...])

    @pl.when(f == pl.num_programs(1) - 1)
    def _():
        o_ref[...] = x_ref[...] + acc_ref[...]


def _mlp(h, gain, w1, w2, *, tm=512, tf=1024):
    M, D = h.shape
    F = w1.shape[1]
    assert M % tm == 0 and F % tf == 0
    return pl.pallas_call(
        _mlp_kernel,
        grid=(M // tm, F // tf),
        in_specs=[pl.BlockSpec((tm, D), lambda i, f: (i, 0)),
                  pl.BlockSpec((1, D), lambda i, f: (0, 0)),
                  pl.BlockSpec((D, tf), lambda i, f: (0, f)),
                  pl.BlockSpec((tf, D), lambda i, f: (f, 0))],
        out_specs=pl.BlockSpec((tm, D), lambda i, f: (i, 0)),
        out_shape=jax.ShapeDtypeStruct((M, D), F32),
        scratch_shapes=[pltpu.VMEM((tm, D), BF16), pltpu.VMEM((tm, D), F32)],
        compiler_params=_params("parallel", "arbitrary"),
        name="mlp",
    )(h, gain.reshape(1, D).astype(F32), w1, w2)


SB_SKIP = 110.0


def _sb_kernel(q_ref, k_ref, v_ref, o_ref, *, t, n_sub):
    i = pl.program_id(2)
    row = lax.broadcasted_iota(jnp.int32, (t, t), 0)
    col = lax.broadcasted_iota(jnp.int32, (t, t), 1)
    later = (row > col).astype(BF16)
    causal = col < row

    def gates(q, j, diagonal):
        kb = k_ref[0, pl.ds(pl.multiple_of(j * t, t), t), :]
        z = _dot_nt(q, kb)
        nz = -z
        lom = jnp.minimum(nz, 0.0) - jnp.log2(1.0 + jnp.exp2(jnp.minimum(z, nz)))
        log_beta = lom + z
        if diagonal:
            lom = jnp.where(causal, lom, 0.0)
        hi = lom.astype(BF16)
        lo = (lom - hi.astype(F32)).astype(BF16)
        within = _dot(hi, later) + _dot(lo, later)
        return log_beta, within, jnp.sum(lom, axis=1, keepdims=True)

    def weigh(j, log_beta, after, diagonal):
        a = jnp.exp2(log_beta + after)
        if diagonal:
            a = jnp.where(causal, a, 0.0)
        return _dot(a.astype(BF16), v_ref[0, pl.ds(pl.multiple_of(j * t, t), t), :])

    skip = -SB_SKIP * LOG2E
    near = []
    pending = jnp.full((t, 1), skip, F32)
    for a in range(n_sub):
        rows = slice(a * t, (a + 1) * t)
        q = q_ref[0, rows, :]
        jd = i * n_sub + a
        has_prev = jd > 0
        jp = jnp.maximum(jd - 1, 0)
        lb_d, within_d, carry = gates(q, jd, True)
        lb_p, within_p, total_p = gates(q, jp, False)
        acc = weigh(jd, lb_d, within_d, True) + jnp.where(has_prev, weigh(jp, lb_p, within_p + carry, False), 0.0)
        carry = carry + jnp.where(has_prev, total_p, 0.0)
        o_ref[0, rows, :] = acc.astype(o_ref.dtype)
        near.append((rows, q, jd, carry, acc))
        pending = jnp.maximum(pending, jnp.where(jd >= 2, carry, skip))

    def cond(s):
        j, carry, _ = s
        return jnp.logical_and(j >= 0, jnp.max(carry) > skip)

    @pl.when(jnp.max(pending) > skip)
    def _():
        for rows, q, jd, carry, acc in near:
            def body(s, q=q):
                j, carry, acc = s
                log_beta, within, total = gates(q, j, False)
                return j - 1, carry + total, acc + weigh(j, log_beta, within + carry, False)

            _, _, acc = lax.while_loop(cond, body, (jd - 2, carry, acc))
            o_ref[0, rows, :] = acc.astype(o_ref.dtype)


def _sb_attention(qkv, n_heads, *, t=256, n_sub=4):
    B, S, _ = qkv.shape
    d = HEAD_DIM
    tile = t * n_sub
    assert S % tile == 0
    return pl.pallas_call(
        functools.partial(_sb_kernel, t=t, n_sub=n_sub),
        grid=(B, n_heads, S // tile),
        in_specs=[pl.BlockSpec((1, tile, d), lambda b, h, i: (b, i, h)),
                  pl.BlockSpec((1, S, d), lambda b, h, i: (b, 0, n_heads + h)),
                  pl.BlockSpec((1, S, d), lambda b, h, i: (b, 0, 2 * n_heads + h))],
        out_specs=pl.BlockSpec((1, tile, d), lambda b, h, i: (b, i, h)),
        out_shape=jax.ShapeDtypeStruct((B, S, n_heads * d), BF16),
        compiler_params=_params("parallel", "parallel", "arbitrary"),
        name="sb_attn",
    )(qkv, qkv, qkv)


HG_CHUNK = 64
HG_SUB = 16
HG_GROUP = 2
HG_SPAN = 256
HG_SAFE = 80.0


def _hg_kernel(q_ref, fz_ref, v_ref, g_ref, lb_ref, gn_ref, tril_ref, o_ref, state_ref, oc_ref, b_ref, k_ref,
               *, n_chunks):
    C, U, d = HG_CHUNK, HG_SUB, HEAD_DIM

    @pl.when(pl.program_id(2) == 0)
    def _():
        state_ref[...] = jnp.zeros_like(state_ref)

    gn = gn_ref[...]
    row = lax.broadcasted_iota(jnp.int32, (C, C), 0)
    col = lax.broadcasted_iota(jnp.int32, (C, C), 1)
    lower = col <= row
    sub_row = lax.broadcasted_iota(jnp.int32, (U, 1), 0)
    tril = tril_ref[...]

    heads = []
    for hd in range(HG_GROUP):
        lanes = slice(hd * d, (hd + 1) * d)
        lb = lb_ref[:, lanes]
        fz = fz_ref[0, :, lanes]
        e = jnp.exp(-jnp.abs(fz))
        inv = 1.0 / (1.0 + e)
        sig_pos = jnp.where(fz >= 0, inv, e * inv)
        sig_neg = jnp.where(fz >= 0, e * inv, inv)
        log_f = jnp.log(lb + (1.0 - lb) * sig_pos)
        k = (1.0 - lb) * sig_neg
        l_hi = log_f.astype(BF16)
        l_lo = (log_f - l_hi.astype(F32)).astype(BF16)
        span = tril.shape[0]
        b = jnp.concatenate([_dot(tril, l_hi[r:r + span]) + _dot(tril, l_lo[r:r + span])
                             for r in range(0, fz.shape[0], span)], axis=0)
        heads.append((lanes, k, b))
    safe = functools.reduce(jnp.minimum, [jnp.min(b) for _, _, b in heads]) > -HG_SAFE

    def epilogue(o, g):
        o = o * lax.rsqrt(jnp.mean(o * o, axis=-1, keepdims=True) + RMS_EPS) * gn
        return (o * (g / (1.0 + jnp.exp(-g)))).astype(o_ref.dtype)

    @pl.when(safe)
    def _():
        for hd, (lanes, k, b) in enumerate(heads):
            q = q_ref[0, :, lanes]
            v16 = v_ref[0, :, lanes].astype(BF16)
            qe = (q * jnp.exp(b)).astype(BF16)
            ke = (k * jnp.exp(-b)).astype(BF16)
            state_t = state_ref[hd]
            for c in range(n_chunks):
                rows = slice(c * C, (c + 1) * C)
                s = jnp.where(lower, _dot_nt(qe[rows], ke[rows]), 0.0)
                oc_ref[hd, rows, :] = _dot(s.astype(BF16), v16[rows]) + _dot_nt(qe[rows], state_t.astype(BF16))
                b_last = b[(c + 1) * C - 1:(c + 1) * C, :]
                kd = (k[rows] * jnp.exp(b_last - b[rows])).astype(BF16)
                state_t = state_t * jnp.exp(b_last) + _dot_tn(v16[rows], kd)
            state_ref[hd] = state_t
            o_ref[0, :, lanes] = epilogue(oc_ref[hd], g_ref[0, :, lanes])

    def head_chunk(hd, r0):
        lanes = slice(hd * d, (hd + 1) * d)
        q = q_ref[0, pl.ds(r0, C), lanes]
        v = v_ref[0, pl.ds(r0, C), lanes]
        b = b_ref[hd, pl.ds(r0, C), :]
        k = k_ref[hd, pl.ds(r0, C), :]
        log2_k = jnp.log2(k)
        state_t = state_ref[hd]
        rows = pl.ds(r0, C)
        oc_ref[hd, rows, :] = _dot_nt((q * jnp.exp(b)).astype(BF16), state_t.astype(BF16))
        v16 = v.astype(BF16)
        for J in range(C // U - 1):
            lo_r, hi_r = J * U, (J + 1) * U
            ref_b = b[hi_r - 1:hi_r, :]
            kt = (k[lo_r:hi_r] * jnp.exp(ref_b - b[lo_r:hi_r])).astype(BF16)
            qs = (q[hi_r:] * jnp.exp(b[hi_r:] - ref_b)).astype(BF16)
            s = _dot_nt(qs, kt)
            oc_ref[hd, pl.ds(r0 + hi_r, C - hi_r), :] += _dot(s.astype(BF16), v16[lo_r:hi_r])
        b2 = b * LOG2E
        key2 = log2_k - b2
        for J in range(C // U):
            lo_r, hi_r = J * U, (J + 1) * U
            qj, bj, kj, vj = q[lo_r:hi_r], b2[lo_r:hi_r], key2[lo_r:hi_r], v[lo_r:hi_r]
            acc = jnp.zeros((U, d), F32)
            for s_i in range(U):
                w = qj * jnp.exp2(jnp.minimum(bj + kj[s_i:s_i + 1], 0.0))
                sc = jnp.sum(w, axis=1, keepdims=True)
                sc = jnp.where(sub_row >= s_i, sc, 0.0)
                acc = acc + sc * vj[s_i:s_i + 1]
            oc_ref[hd, pl.ds(r0 + lo_r, U), :] += acc
        b_last = b[C - 1:C, :]
        kd = (k * jnp.exp(b_last - b)).astype(BF16)
        state_ref[hd] = state_t * jnp.exp(b_last) + _dot_tn(v16, kd)
        o_ref[0, rows, lanes] = epilogue(oc_ref[hd, rows, :], g_ref[0, rows, lanes])

    @pl.when(jnp.logical_not(safe))
    def _():
        for hd, (_, k, b) in enumerate(heads):
            b_ref[hd] = b
            k_ref[hd] = k

        def chunk(c, _):
            r0 = pl.multiple_of(c * C, C)
            for hd in range(HG_GROUP):
                head_chunk(hd, r0)
            return 0

        lax.fori_loop(0, n_chunks, chunk, 0)


def _hgrn2(proj, lb, g_norm, n_heads, *, ts=512):
    B, S, _ = proj.shape
    d, G = HEAD_DIM, HG_GROUP
    assert S % ts == 0 and ts % HG_CHUNK == 0 and n_heads % G == 0
    n_groups = n_heads // G

    def col(part):
        return pl.BlockSpec((1, ts, G * d), lambda b, h, s: (b, s, part * n_groups + h))

    span = min(ts, HG_SPAN)
    r = jnp.arange(span)
    tril = ((r[None, :] <= r[:, None]) & (r[None, :] // HG_CHUNK == r[:, None] // HG_CHUNK)).astype(BF16)
    return pl.pallas_call(
        functools.partial(_hg_kernel, n_chunks=ts // HG_CHUNK),
        grid=(B, n_groups, S // ts),
        in_specs=[col(0), col(1), col(2), col(3),
                  pl.BlockSpec((1, G * d), lambda b, h, s: (0, h)),
                  pl.BlockSpec((1, d), lambda b, h, s: (0, 0)),
                  pl.BlockSpec((span, span), lambda b, h, s: (0, 0))],
        out_specs=pl.BlockSpec((1, ts, G * d), lambda b, h, s: (b, s, h)),
        out_shape=jax.ShapeDtypeStruct((B, S, n_heads * d), BF16),
        scratch_shapes=[pltpu.VMEM((G, d, d), F32), pltpu.VMEM((G, ts, d), F32),
                        pltpu.VMEM((G, ts, d), F32), pltpu.VMEM((G, ts, d), F32)],
        compiler_params=_params("parallel", "parallel", "arbitrary"),
        name="hgrn2",
    )(proj, proj, proj, proj, lb.reshape(1, n_heads * d), g_norm.reshape(1, d), tril)


def _rope_kernel(pos_ref, invf_ref, x_ref, kpe_ref, cos_ref, sin_ref):
    half = MLA_ROPE // 2
    ang = pos_ref[...].astype(F32) * invf_ref[...]
    lane = lax.broadcasted_iota(jnp.int32, ang.shape, 1)
    c = jnp.where(lane < MLA_ROPE, jnp.cos(ang), 0.0)
    s = jnp.sin(ang)
    s = jnp.where(lane < half, -s, jnp.where(lane < MLA_ROPE, s, 0.0))
    cos_ref[...] = c
    sin_ref[...] = s
    x = x_ref[...]
    swapped = pltpu.roll(x, half, 1) + pltpu.roll(x, HEAD_DIM - half, 1)
    kpe_ref[...] = (x * c + swapped * s).astype(kpe_ref.dtype)


def _rope_tables(positions, dkv, kpe_col, *, tm=512):
    M = dkv.shape[0]
    half = MLA_ROPE // 2
    freq = ROPE_THETA ** (-jnp.arange(0, MLA_ROPE, 2, dtype=F32) / MLA_ROPE)
    invf = jnp.concatenate([freq, freq, jnp.zeros((HEAD_DIM - 2 * half,), F32)]).reshape(1, HEAD_DIM)
    row = pl.BlockSpec((tm, HEAD_DIM), lambda i: (i, 0))
    return pl.pallas_call(
        _rope_kernel,
        grid=(M // tm,),
        in_specs=[pl.BlockSpec((tm, 1), lambda i: (i, 0)),
                  pl.BlockSpec((1, HEAD_DIM), lambda i: (0, 0)),
                  pl.BlockSpec((tm, HEAD_DIM), lambda i: (i, kpe_col))],
        out_specs=[row, row, row],
        out_shape=[jax.ShapeDtypeStruct((M, HEAD_DIM), BF16),
                   jax.ShapeDtypeStruct((M, HEAD_DIM), F32),
                   jax.ShapeDtypeStruct((M, HEAD_DIM), F32)],
        compiler_params=_params("parallel"),
        name="mla_rope",
    )(positions.reshape(M, 1), invf, dkv)


def _mla_kernel(q_ref, cos_ref, sin_ref, kn_ref, v_ref, kpe_ref, o_ref, kcat_ref, sa_ref, sb_ref, *, t):
    i = pl.program_id(2)
    half = MLA_ROPE // 2

    @pl.when(i == 0)
    def _():
        kcat_ref[:, :MLA_NOPE] = kn_ref[0]
        kcat_ref[:, MLA_NOPE:] = kpe_ref[0]

    qn = q_ref[0, :, :MLA_NOPE]
    qr = q_ref[0, :, MLA_NOPE:].astype(F32)
    swapped = pltpu.roll(qr, half, 1) + pltpu.roll(qr, HEAD_DIM - half, 1)
    qp = (qr * cos_ref[0] + swapped * sin_ref[0]).astype(BF16)
    q = jnp.concatenate([qn, qp], axis=1)
    row = lax.broadcasted_iota(jnp.int32, (t, t), 0) // MLA_CHUNK
    col = lax.broadcasted_iota(jnp.int32, (t, t), 1) // MLA_CHUNK
    visible = col <= row

    def scores(j, dst_ref):
        dst_ref[...] = _dot_nt(q, kcat_ref[pl.ds(pl.multiple_of(j * t, t), t), :])

    def absorb(j, src_ref, m, l, acc, diagonal):
        s = src_ref[...]
        if diagonal:
            s = jnp.where(visible, s, -jnp.inf)
        m_new = jnp.maximum(m, jnp.max(s, axis=1, keepdims=True))
        alpha = jnp.exp2(m - m_new)
        p = jnp.exp2(s - m_new)
        l = alpha * l + jnp.sum(p, axis=1, keepdims=True)
        acc = alpha * acc + _dot(p.astype(BF16), v_ref[0, pl.ds(pl.multiple_of(j * t, t), t), :])
        return m_new, l, acc

    def pair(p, carry):
        scores(2 * p + 1, sb_ref)
        carry = absorb(2 * p, sa_ref, *carry, False)
        scores(2 * p + 2, sa_ref)
        return absorb(2 * p + 1, sb_ref, *carry, False)

    def finish(m, l, acc):
        o_ref[0] = (acc / l).astype(o_ref.dtype)

    scores(0, sa_ref)
    init = (jnp.full((t, 1), -jnp.inf, F32), jnp.zeros((t, 1), F32), jnp.zeros((t, HEAD_DIM), F32))
    carry = lax.fori_loop(0, i // 2, pair, init)

    @pl.when(i % 2 == 0)
    def _():
        finish(*absorb(i, sa_ref, *carry, True))

    @pl.when(i % 2 == 1)
    def _():
        scores(i, sb_ref)
        finish(*absorb(i, sb_ref, *absorb(i - 1, sa_ref, *carry, False), True))


def _mla_attention(q, cos, sin, kv, kpe, n_heads, *, t=512):
    B, S, _ = q.shape
    d = HEAD_DIM
    assert S % t == 0 and t % MLA_CHUNK == 0
    return pl.pallas_call(
        functools.partial(_mla_kernel, t=t),
        grid=(B, n_heads, S // t),
        in_specs=[pl.BlockSpec((1, t, 2 * d), lambda b, h, i: (b, i, h)),
                  pl.BlockSpec((1, t, d), lambda b, h, i: (b, i, 0)),
                  pl.BlockSpec((1, t, d), lambda b, h, i: (b, i, 0)),
                  pl.BlockSpec((1, S, d), lambda b, h, i: (b, 0, 2 * h)),
                  pl.BlockSpec((1, S, d), lambda b, h, i: (b, 0, 2 * h + 1)),
                  pl.BlockSpec((1, S, d), lambda b, h, i: (b, 0, 0))],
        out_specs=pl.BlockSpec((1, t, d), lambda b, h, i: (b, i, h)),
        out_shape=jax.ShapeDtypeStruct((B, S, n_heads * d), BF16),
        scratch_shapes=[pltpu.VMEM((S, 2 * d), BF16), pltpu.VMEM((t, t), F32), pltpu.VMEM((t, t), F32)],
        compiler_params=_params("parallel", "parallel", "arbitrary"),
        name="mla_attn",
    )(q, cos, sin, kv, kv, kpe)


def _sb_layer(h, gain, w_qkv, w_o, B, S):
    M, D = h.shape
    H = w_o.shape[0] // HEAD_DIM
    col_scale = jnp.where(jnp.arange(w_qkv.shape[1]) < H * HEAD_DIM, HEAD_DIM ** -0.5 * LOG2E, 1.0)
    qkv = _linear(h, (w_qkv * col_scale).astype(BF16), name="sb_qkv", gain=gain, out_dtype=BF16)
    o = _sb_attention(qkv.reshape(B, S, -1), H)
    return _linear(o.reshape(M, -1), w_o.astype(BF16), name="sb_out", residual=h)


def _hg_layer(h, gain, w_in, g_norm, w_o, lb, B, S):
    M, D = h.shape
    H = w_o.shape[0] // HEAD_DIM
    proj = _linear(h, w_in.astype(BF16), name="hg_in", gain=gain, out_dtype=F32)
    o = _hgrn2(proj.reshape(B, S, -1), lb, g_norm, H)
    return _linear(o.reshape(M, -1), w_o.astype(BF16), name="hg_out", residual=h)


def _mla_layer(h, positions, gain, w_dkv, q_norm, kv_norm, w_uq, w_ukv, w_o, B, S):
    M, D = h.shape
    H = w_o.shape[0] // MLA_NOPE
    qr, kvr = MLA_Q_RANK, MLA_KV_RANK
    pad_a = jnp.zeros((D, qr - kvr), F32)
    pad_b = jnp.zeros((D, HEAD_DIM - MLA_ROPE), F32)
    w_d = jnp.concatenate([w_dkv[:, qr:qr + kvr], pad_a, w_dkv[:, :qr], w_dkv[:, qr + kvr:], pad_b], axis=1)
    dkv = _linear(h, w_d.astype(BF16), name="mla_down", gain=gain, out_dtype=F32, tn=w_d.shape[1])
    kpe, cos, sin = _rope_tables(positions, dkv, (2 * qr) // HEAD_DIM)
    w_q = w_uq.reshape(qr, H, MLA_NOPE + MLA_ROPE) * ((MLA_NOPE + MLA_ROPE) ** -0.5 * LOG2E)
    w_q = jnp.concatenate([w_q, jnp.zeros((qr, H, HEAD_DIM - MLA_ROPE), F32)], axis=2).reshape(qr, H * 2 * HEAD_DIM)
    q = _linear(dkv, w_q.astype(BF16), name="mla_q", gain=q_norm, out_dtype=BF16, x_col=1)
    kv = _linear(dkv, w_ukv.astype(BF16), name="mla_kv", gain=kv_norm, out_dtype=BF16, x_col=0)
    o = _mla_attention(q.reshape(B, S, -1), cos.reshape(B, S, -1), sin.reshape(B, S, -1),
                       kv.reshape(B, S, -1), kpe.reshape(B, S, -1), H)
    return _linear(o.reshape(M, -1), w_o.astype(BF16), name="mla_out", residual=h)


def _final_norm_kernel(x_ref, g_ref, o_ref):
    x = x_ref[...]
    ms = jnp.mean(x * x, axis=-1, keepdims=True)
    o_ref[...] = x * lax.rsqrt(ms + RMS_EPS) * g_ref[...]


def _final_norm(h, gain, *, tm=512):
    M, D = h.shape
    return pl.pallas_call(
        _final_norm_kernel,
        grid=(M // tm,),
        in_specs=[pl.BlockSpec((tm, D), lambda i: (i, 0)), pl.BlockSpec((1, D), lambda i: (0, 0))],
        out_specs=pl.BlockSpec((tm, D), lambda i: (i, 0)),
        out_shape=jax.ShapeDtypeStruct((M, D), F32),
        compiler_params=_params("parallel"),
        name="final_norm",
    )(h, gain.reshape(1, D))


def kernel(x, positions, norm_mix, norm_mlp, final_norm, sb_w_qkv, sb_w_o, hg_w_in, hg_lb_logits, hg_g_norm, hg_w_o, mla_w_dkv, mla_q_norm, mla_kv_norm, mla_w_uq, mla_w_ukv, mla_w_o, mlp_w1, mlp_w2):
    B, S, D = x.shape
    depth = norm_mix.shape[0]
    p_lb = jax.nn.softmax(hg_lb_logits.astype(F32), axis=0)
    lb_all = jnp.cumsum(p_lb, axis=0) - p_lb[0]
    h = x.reshape(B * S, D)
    for i in range(depth):
        m, j = i % N_MIXERS, i // N_MIXERS
        if m == 0:
            h = _sb_layer(h, norm_mix[i], sb_w_qkv[j], sb_w_o[j], B, S)
        elif m == 1:
            h = _hg_layer(h, norm_mix[i], hg_w_in[j], hg_g_norm[j], hg_w_o[j], lb_all[i], B, S)
        else:
            h = _mla_layer(h, positions, norm_mix[i], mla_w_dkv[j], mla_q_norm[j], mla_kv_norm[j],
                           mla_w_uq[j], mla_w_ukv[j], mla_w_o[j], B, S)
        h = _mlp(h, norm_mlp[i], mlp_w1[i].astype(BF16), mlp_w2[i].astype(BF16))
    return _final_norm(h, final_norm).reshape(B, S, D)
```

```python
import functools

import jax
import jax.numpy as jnp
from jax import lax
from jax.experimental import pallas as pl
from jax.experimental.pallas import tpu as pltpu

F32 = jnp.float32
BF16 = jnp.bfloat16

RMS_EPS = 1e-6
N_MIXERS = 3
HEAD_DIM = 128
MLA_NOPE = 128
MLA_ROPE = 64
MLA_Q_RANK = 768
MLA_KV_RANK = 512
MLA_CHUNK = 64
ROPE_THETA = 10000.0
LOG2E = 1.4426950408889634
VMEM_LIMIT = 56 * 1024 * 1024


def _params(*sem):
    return pltpu.CompilerParams(dimension_semantics=sem, vmem_limit_bytes=VMEM_LIMIT)


def _dot(a, b):
    return jnp.dot(a, b, preferred_element_type=F32)


def _dot_nt(a, b):
    return lax.dot_general(a, b, (((1,), (1,)), ((), ())), preferred_element_type=F32)


def _dot_tn(a, b):
    return lax.dot_general(a, b, (((0,), (0,)), ((), ())), preferred_element_type=F32)


def _split3(x):
    hi = x.astype(BF16)
    r1 = x - hi.astype(F32)
    mid = r1.astype(BF16)
    lo = (r1 - mid.astype(F32)).astype(BF16)
    return hi, mid, lo


def _linear_kernel(*refs, has_gain, has_res):
    x_ref, w_ref = refs[0], refs[1]
    pos = 2
    g_ref = r_ref = None
    if has_gain:
        g_ref = refs[pos]
        pos += 1
    if has_res:
        r_ref = refs[pos]
        pos += 1
    o_ref = refs[pos]

    if has_gain:
        xn_ref = refs[pos + 1]

        @pl.when(pl.program_id(1) == 0)
        def _():
            x = x_ref[...].astype(F32)
            ms = jnp.mean(x * x, axis=-1, keepdims=True)
            xn_ref[...] = (x * lax.rsqrt(ms + RMS_EPS) * g_ref[...]).astype(BF16)

        lhs = xn_ref[...]
    else:
        lhs = x_ref[...]

    acc = _dot(lhs, w_ref[...])
    if has_res:
        acc = acc + r_ref[...]
    o_ref[...] = acc.astype(o_ref.dtype)


LINEAR_VMEM_BUDGET = 44 * 1024 * 1024
RESIDENT_WEIGHT_BYTES = 8 * 1024 * 1024


def _linear_tiles(M, K, N, x_bytes, out_bytes, has_gain, has_res):
    tn = N if K * N * 2 <= RESIDENT_WEIGHT_BYTES or N % 1024 else 1024
    for tm in (1024, 512, 256, 128):
        if M % tm:
            continue
        need = 2 * tm * K * x_bytes + 2 * K * tn * 2 + 2 * tm * tn * out_bytes
        need += tm * K * 2 if has_gain else 0
        need += 2 * tm * tn * 4 if has_res else 0
        if need <= LINEAR_VMEM_BUDGET:
            return tm, tn
    raise ValueError("no linear tiling fits VMEM")


def _linear(x, w, *, name, gain=None, residual=None, out_dtype=F32, x_col=0):
    M = x.shape[0]
    K, N = w.shape
    assert gain is not None or x.dtype == BF16
    tm, tn = _linear_tiles(M, K, N, x.dtype.itemsize, jnp.dtype(out_dtype).itemsize,
                           gain is not None, residual is not None)
    in_specs = [pl.BlockSpec((tm, K), lambda i, j: (i, x_col)),
                pl.BlockSpec((K, tn), lambda i, j: (0, j))]
    args = [x, w]
    if gain is not None:
        in_specs.append(pl.BlockSpec((1, K), lambda i, j: (0, 0)))
        args.append(gain.reshape(1, K).astype(F32))
    if residual is not None:
        in_specs.append(pl.BlockSpec((tm, tn), lambda i, j: (i, j)))
        args.append(residual)
    return pl.pallas_call(
        functools.partial(_linear_kernel, has_gain=gain is not None, has_res=residual is not None),
        grid=(M // tm, N // tn),
        in_specs=in_specs,
        out_specs=pl.BlockSpec((tm, tn), lambda i, j: (i, j)),
        out_shape=jax.ShapeDtypeStruct((M, N), out_dtype),
        scratch_shapes=[pltpu.VMEM((tm, K), BF16)] if gain is not None else [],
        compiler_params=_params("parallel", "arbitrary"),
        name=name,
    )(*args)


def _mlp_kernel(x_ref, g_ref, w1_ref, w2_ref, o_ref, xn_ref, acc_ref):
    f = pl.program_id(1)

    @pl.when(f == 0)
    def _():
        x = x_ref[...]
        ms = jnp.mean(x * x, axis=-1, keepdims=True)
        xn_ref[...] = (x * lax.rsqrt(ms + RMS_EPS) * g_ref[...]).astype(BF16)
        acc_ref[...] = jnp.zeros_like(acc_ref)

    u = _dot(xn_ref[...], w1_ref[...])
    r = jnp.square(jnp.maximum(u, 0.0)).astype(BF16)
    acc_ref[...] += _dot(r, w2_ref[...])

    @pl.when(f == pl.num_programs(1) - 1)
    def _():
        o_ref[...] = x_ref[...] + acc_ref[...]


def _mlp(h, gain, w1, w2, layer, *, tm=512, tf=1024):
    M, D = h.shape
    F = w1.shape[2]
    assert M % tm == 0 and F % tf == 0
    return pl.pallas_call(
        _mlp_kernel,
        grid=(M // tm, F // tf),
        in_specs=[pl.BlockSpec((tm, D), lambda i, f: (i, 0)),
                  pl.BlockSpec((1, D), lambda i, f: (0, 0)),
                  pl.BlockSpec((None, D, tf), lambda i, f: (layer, 0, f)),
                  pl.BlockSpec((None, tf, D), lambda i, f: (layer, f, 0))],
        out_specs=pl.BlockSpec((tm, D), lambda i, f: (i, 0)),
        out_shape=jax.ShapeDtypeStruct((M, D), F32),
        scratch_shapes=[pltpu.VMEM((tm, D), BF16), pltpu.VMEM((tm, D), F32)],
        compiler_params=_params("parallel", "arbitrary"),
        name="mlp",
    )(h, gain.reshape(1, D).astype(F32), w1, w2)


SB_SKIP = 110.0


def _sb_kernel(q_ref, k_ref, v_ref, o_ref, *, t, n_sub):
    i = pl.program_id(2)
    row = lax.broadcasted_iota(jnp.int32, (t, t), 0)
    col = lax.broadcasted_iota(jnp.int32, (t, t), 1)
    later = (row > col).astype(BF16)
    causal = col < row

    def gates(q, j, diagonal):
        kb = k_ref[0, pl.ds(pl.multiple_of(j * t, t), t), :]
        z = _dot_nt(q, kb)
        nz = -z
        lom = jnp.minimum(nz, 0.0) - jnp.log2(1.0 + jnp.exp2(jnp.minimum(z, nz)))
        log_beta = lom + z
        if diagonal:
            lom = jnp.where(causal, lom, 0.0)
        hi = lom.astype(BF16)
        lo = (lom - hi.astype(F32)).astype(BF16)
        within = _dot(hi, later) + _dot(lo, later)
        return log_beta, within, jnp.sum(lom, axis=1, keepdims=True)

    def weigh(j, log_beta, after, diagonal):
        a = jnp.exp2(log_beta + after)
        if diagonal:
            a = jnp.where(causal, a, 0.0)
        return _dot(a.astype(BF16), v_ref[0, pl.ds(pl.multiple_of(j * t, t), t), :])

    skip = -SB_SKIP * LOG2E
    near = []
    pending = jnp.full((t, 1), skip, F32)
    for a in range(n_sub):
        rows = slice(a * t, (a + 1) * t)
        q = q_ref[0, rows, :]
        jd = i * n_sub + a
        has_prev = jd > 0
        jp = jnp.maximum(jd - 1, 0)
        lb_d, within_d, carry = gates(q, jd, True)
        lb_p, within_p, total_p = gates(q, jp, False)
        acc = weigh(jd, lb_d, within_d, True) + jnp.where(has_prev, weigh(jp, lb_p, within_p + carry, False), 0.0)
        carry = carry + jnp.where(has_prev, total_p, 0.0)
        o_ref[0, rows, :] = acc.astype(o_ref.dtype)
        near.append((rows, q, jd, carry, acc))
        pending = jnp.maximum(pending, jnp.where(jd >= 2, carry, skip))

    def cond(s):
        j, carry, _ = s
        return jnp.logical_and(j >= 0, jnp.max(carry) > skip)

    @pl.when(jnp.max(pending) > skip)
    def _():
        for rows, q, jd, carry, acc in near:
            def body(s, q=q):
                j, carry, acc = s
                log_beta, within, total = gates(q, j, False)
                return j - 1, carry + total, acc + weigh(j, log_beta, within + carry, False)

            _, _, acc = lax.while_loop(cond, body, (jd - 2, carry, acc))
            o_ref[0, rows, :] = acc.astype(o_ref.dtype)


def _sb_attention(qkv, n_heads, *, t=256, n_sub=4):
    B, S, _ = qkv.shape
    d = HEAD_DIM
    tile = t * n_sub
    assert S % tile == 0
    return pl.pallas_call(
        functools.partial(_sb_kernel, t=t, n_sub=n_sub),
        grid=(B, n_heads, S // tile),
        in_specs=[pl.BlockSpec((1, tile, d), lambda b, h, i: (b, i, h)),
                  pl.BlockSpec((1, S, d), lambda b, h, i: (b, 0, n_heads + h)),
                  pl.BlockSpec((1, S, d), lambda b, h, i: (b, 0, 2 * n_heads + h))],
        out_specs=pl.BlockSpec((1, tile, d), lambda b, h, i: (b, i, h)),
        out_shape=jax.ShapeDtypeStruct((B, S, n_heads * d), BF16),
        compiler_params=_params("parallel", "parallel", "arbitrary"),
        name="sb_attn",
    )(qkv, qkv, qkv)


HG_CHUNK = 64
HG_SUB = 16
HG_GROUP = 2
HG_SPAN = 256
HG_SAFE = 80.0


def _hg_kernel(q_ref, fz_ref, v_ref, g_ref, lb_ref, gn_ref, tril_ref, o_ref, state_ref, oc_ref, b_ref, k_ref,
               *, n_chunks):
    C, U, d = HG_CHUNK, HG_SUB, HEAD_DIM

    @pl.when(pl.program_id(2) == 0)
    def _():
        state_ref[...] = jnp.zeros_like(state_ref)

    gn = gn_ref[...]
    row = lax.broadcasted_iota(jnp.int32, (C, C), 0)
    col = lax.broadcasted_iota(jnp.int32, (C, C), 1)
    lower = col <= row
    sub_row = lax.broadcasted_iota(jnp.int32, (U, 1), 0)
    tril = tril_ref[...]

    heads = []
    for hd in range(HG_GROUP):
        lanes = slice(hd * d, (hd + 1) * d)
        lb = lb_ref[:, lanes]
        fz = fz_ref[0, :, lanes]
        e = jnp.exp(-jnp.abs(fz))
        inv = 1.0 / (1.0 + e)
        sig_pos = jnp.where(fz >= 0, inv, e * inv)
        sig_neg = jnp.where(fz >= 0, e * inv, inv)
        log_f = jnp.log(lb + (1.0 - lb) * sig_pos)
        k = (1.0 - lb) * sig_neg
        l_hi = log_f.astype(BF16)
        l_lo = (log_f - l_hi.astype(F32)).astype(BF16)
        span = tril.shape[0]
        b = jnp.concatenate([_dot(tril, l_hi[r:r + span]) + _dot(tril, l_lo[r:r + span])
                             for r in range(0, fz.shape[0], span)], axis=0)
        heads.append((lanes, k, b))
    safe = functools.reduce(jnp.minimum, [jnp.min(b) for _, _, b in heads]) > -HG_SAFE

    def epilogue(o, g):
        o = o * lax.rsqrt(jnp.mean(o * o, axis=-1, keepdims=True) + RMS_EPS) * gn
        return (o * (g / (1.0 + jnp.exp(-g)))).astype(o_ref.dtype)

    @pl.when(safe)
    def _():
        for hd, (lanes, k, b) in enumerate(heads):
            q = q_ref[0, :, lanes]
            v16 = v_ref[0, :, lanes].astype(BF16)
            qe = (q * jnp.exp(b)).astype(BF16)
            ke = (k * jnp.exp(-b)).astype(BF16)
            state_t = state_ref[hd]
            for c in range(n_chunks):
                rows = slice(c * C, (c + 1) * C)
                s = jnp.where(lower, _dot_nt(qe[rows], ke[rows]), 0.0)
                oc_ref[hd, rows, :] = _dot(s.astype(BF16), v16[rows]) + _dot_nt(qe[rows], state_t.astype(BF16))
                b_last = b[(c + 1) * C - 1:(c + 1) * C, :]
                kd = (k[rows] * jnp.exp(b_last - b[rows])).astype(BF16)
                state_t = state_t * jnp.exp(b_last) + _dot_tn(v16[rows], kd)
            state_ref[hd] = state_t
            o_ref[0, :, lanes] = epilogue(oc_ref[hd], g_ref[0, :, lanes])

    def head_chunk(hd, r0):
        lanes = slice(hd * d, (hd + 1) * d)
        q = q_ref[0, pl.ds(r0, C), lanes]
        v = v_ref[0, pl.ds(r0, C), lanes]
        b = b_ref[hd, pl.ds(r0, C), :]
        k = k_ref[hd, pl.ds(r0, C), :]
        log2_k = jnp.log2(k)
        state_t = state_ref[hd]
        rows = pl.ds(r0, C)
        oc_ref[hd, rows, :] = _dot_nt((q * jnp.exp(b)).astype(BF16), state_t.astype(BF16))
        v16 = v.astype(BF16)
        for J in range(C // U - 1):
            lo_r, hi_r = J * U, (J + 1) * U
            ref_b = b[hi_r - 1:hi_r, :]
            kt = (k[lo_r:hi_r] * jnp.exp(ref_b - b[lo_r:hi_r])).astype(BF16)
            qs = (q[hi_r:] * jnp.exp(b[hi_r:] - ref_b)).astype(BF16)
            s = _dot_nt(qs, kt)
            oc_ref[hd, pl.ds(r0 + hi_r, C - hi_r), :] += _dot(s.astype(BF16), v16[lo_r:hi_r])
        b2 = b * LOG2E
        key2 = log2_k - b2
        for J in range(C // U):
            lo_r, hi_r = J * U, (J + 1) * U
            qj, bj, kj, vj = q[lo_r:hi_r], b2[lo_r:hi_r], key2[lo_r:hi_r], v[lo_r:hi_r]
            acc = jnp.zeros((U, d), F32)
            for s_i in range(U):
                w = qj * jnp.exp2(jnp.minimum(bj + kj[s_i:s_i + 1], 0.0))
                sc = jnp.sum(w, axis=1, keepdims=True)
                sc = jnp.where(sub_row >= s_i, sc, 0.0)
                acc = acc + sc * vj[s_i:s_i + 1]
            oc_ref[hd, pl.ds(r0 + lo_r, U), :] += acc
        b_last = b[C - 1:C, :]
        kd = (k * jnp.exp(b_last - b)).astype(BF16)
        state_ref[hd] = state_t * jnp.exp(b_last) + _dot_tn(v16, kd)
        o_ref[0, rows, lanes] = epilogue(oc_ref[hd, rows, :], g_ref[0, rows, lanes])

    @pl.when(jnp.logical_not(safe))
    def _():
        for hd, (_, k, b) in enumerate(heads):
            b_ref[hd] = b
            k_ref[hd] = k

        def chunk(c, _):
            r0 = pl.multiple_of(c * C, C)
            for hd in range(HG_GROUP):
                head_chunk(hd, r0)
            return 0

        lax.fori_loop(0, n_chunks, chunk, 0)


def _hgrn2(proj, lb, g_norm, n_heads, *, ts=512):
    B, S, _ = proj.shape
    d, G = HEAD_DIM, HG_GROUP
    assert S % ts == 0 and ts % HG_CHUNK == 0 and n_heads % G == 0
    n_groups = n_heads // G

    def col(part):
        return pl.BlockSpec((1, ts, G * d), lambda b, h, s: (b, s, part * n_groups + h))

    span = min(ts, HG_SPAN)
    r = jnp.arange(span)
    tril = ((r[None, :] <= r[:, None]) & (r[None, :] // HG_CHUNK == r[:, None] // HG_CHUNK)).astype(BF16)
    return pl.pallas_call(
        functools.partial(_hg_kernel, n_chunks=ts // HG_CHUNK),
        grid=(B, n_groups, S // ts),
        in_specs=[col(0), col(1), col(2), col(3),
                  pl.BlockSpec((1, G * d), lambda b, h, s: (0, h)),
                  pl.BlockSpec((1, d), lambda b, h, s: (0, 0)),
                  pl.BlockSpec((span, span), lambda b, h, s: (0, 0))],
        out_specs=pl.BlockSpec((1, ts, G * d), lambda b, h, s: (b, s, h)),
        out_shape=jax.ShapeDtypeStruct((B, S, n_heads * d), BF16),
        scratch_shapes=[pltpu.VMEM((G, d, d), F32), pltpu.VMEM((G, ts, d), F32),
                        pltpu.VMEM((G, ts, d), F32), pltpu.VMEM((G, ts, d), F32)],
        compiler_params=_params("parallel", "parallel", "arbitrary"),
        name="hgrn2",
    )(proj, proj, proj, proj, lb.reshape(1, n_heads * d), g_norm.reshape(1, d), tril)


def _rope_kernel(pos_ref, invf_ref, x_ref, kpe_ref, cos_ref, sin_ref):
    half = MLA_ROPE // 2
    ang = pos_ref[...].astype(F32) * invf_ref[...]
    lane = lax.broadcasted_iota(jnp.int32, ang.shape, 1)
    c = jnp.where(lane < MLA_ROPE, jnp.cos(ang), 0.0)
    s = jnp.sin(ang)
    s = jnp.where(lane < half, -s, jnp.where(lane < MLA_ROPE, s, 0.0))
    cos_ref[...] = c
    sin_ref[...] = s
    x = x_ref[...]
    swapped = pltpu.roll(x, half, 1) + pltpu.roll(x, HEAD_DIM - half, 1)
    kpe_ref[...] = (x * c + swapped * s).astype(kpe_ref.dtype)


def _rope_tables(positions, dkv, kpe_col, *, tm=512):
    M = dkv.shape[0]
    half = MLA_ROPE // 2
    freq = ROPE_THETA ** (-jnp.arange(0, MLA_ROPE, 2, dtype=F32) / MLA_ROPE)
    invf = jnp.concatenate([freq, freq, jnp.zeros((HEAD_DIM - 2 * half,), F32)]).reshape(1, HEAD_DIM)
    row = pl.BlockSpec((tm, HEAD_DIM), lambda i: (i, 0))
    return pl.pallas_call(
        _rope_kernel,
        grid=(M // tm,),
        in_specs=[pl.BlockSpec((tm, 1), lambda i: (i, 0)),
                  pl.BlockSpec((1, HEAD_DIM), lambda i: (0, 0)),
                  pl.BlockSpec((tm, HEAD_DIM), lambda i: (i, kpe_col))],
        out_specs=[row, row, row],
        out_shape=[jax.ShapeDtypeStruct((M, HEAD_DIM), BF16),
                   jax.ShapeDtypeStruct((M, HEAD_DIM), F32),
                   jax.ShapeDtypeStruct((M, HEAD_DIM), F32)],
        compiler_params=_params("parallel"),
        name="mla_rope",
    )(positions.reshape(M, 1), invf, dkv)


MLA_ROW_CHUNK = 256


def _mla_kernel(q_ref, cos_ref, sin_ref, kn_ref, v_ref, kpe_ref, o_ref, kcat_ref, sa_ref, sb_ref,
                m_ref, l_ref, acc_ref, *, t):
    i = pl.program_id(2)
    half = MLA_ROPE // 2

    @pl.when(i == 0)
    def _():
        kcat_ref[:, :MLA_NOPE] = kn_ref[0]
        kcat_ref[:, MLA_NOPE:] = kpe_ref[0]

    qn = q_ref[0, :, :MLA_NOPE]
    qr = q_ref[0, :, MLA_NOPE:].astype(F32)
    swapped = pltpu.roll(qr, half, 1) + pltpu.roll(qr, HEAD_DIM - half, 1)
    qp = (qr * cos_ref[0] + swapped * sin_ref[0]).astype(BF16)
    q = jnp.concatenate([qn, qp], axis=1)
    rc = MLA_ROW_CHUNK
    row = lax.broadcasted_iota(jnp.int32, (rc, t), 0) // MLA_CHUNK
    col = lax.broadcasted_iota(jnp.int32, (rc, t), 1) // MLA_CHUNK

    def scores(j, dst_ref):
        dst_ref[...] = _dot_nt(q, kcat_ref[pl.ds(pl.multiple_of(j * t, t), t), :])

    def absorb(j, src_ref, diagonal):
        vb = v_ref[0, pl.ds(pl.multiple_of(j * t, t), t), :]
        for r in range(t // rc):
            rows = slice(r * rc, (r + 1) * rc)
            s = src_ref[rows, :]
            if diagonal:
                s = jnp.where(col <= row + (r * rc) // MLA_CHUNK, s, -jnp.inf)
            m_old = m_ref[rows, :]
            lane_groups = [s[:, c:c + HEAD_DIM] for c in range(0, t, HEAD_DIM)]
            m_col = jnp.max(functools.reduce(jnp.maximum, lane_groups + [m_old]), axis=1, keepdims=True)
            m_new = jnp.broadcast_to(m_col, m_old.shape)
            alpha = jnp.exp2(m_old - m_new)
            p = jnp.exp2(s - m_col)
            l_ref[rows, :] = alpha * l_ref[rows, :] + jnp.sum(p, axis=1, keepdims=True)
            acc_ref[rows, :] = alpha * acc_ref[rows, :] + _dot(p.astype(BF16), vb)
            m_ref[rows, :] = m_new

    def pair(p, _):
        scores(2 * p + 1, sb_ref)
        absorb(2 * p, sa_ref, False)
        scores(2 * p + 2, sa_ref)
        absorb(2 * p + 1, sb_ref, False)
        return 0

    def finish():
        o_ref[0] = (acc_ref[...] / l_ref[...]).astype(o_ref.dtype)

    m_ref[...] = jnp.full_like(m_ref, -jnp.inf)
    l_ref[...] = jnp.zeros_like(l_ref)
    acc_ref[...] = jnp.zeros_like(acc_ref)
    scores(0, sa_ref)
    lax.fori_loop(0, i // 2, pair, 0)

    @pl.when(i % 2 == 0)
    def _():
        absorb(i, sa_ref, True)
        finish()

    @pl.when(i % 2 == 1)
    def _():
        scores(i, sb_ref)
        absorb(i - 1, sa_ref, False)
        absorb(i, sb_ref, True)
        finish()


def _mla_attention(q, cos, sin, kv, kpe, n_heads, *, t=512):
    B, S, _ = q.shape
    d = HEAD_DIM
    assert S % t == 0 and t % MLA_CHUNK == 0
    return pl.pallas_call(
        functools.partial(_mla_kernel, t=t),
        grid=(B, n_heads, S // t),
        in_specs=[pl.BlockSpec((1, t, 2 * d), lambda b, h, i: (b, i, h)),
                  pl.BlockSpec((1, t, d), lambda b, h, i: (b, i, 0)),
                  pl.BlockSpec((1, t, d), lambda b, h, i: (b, i, 0)),
                  pl.BlockSpec((1, S, d), lambda b, h, i: (b, 0, 2 * h)),
                  pl.BlockSpec((1, S, d), lambda b, h, i: (b, 0, 2 * h + 1)),
                  pl.BlockSpec((1, S, d), lambda b, h, i: (b, 0, 0))],
        out_specs=pl.BlockSpec((1, t, d), lambda b, h, i: (b, i, h)),
        out_shape=jax.ShapeDtypeStruct((B, S, n_heads * d), BF16),
        scratch_shapes=[pltpu.VMEM((S, 2 * d), BF16), pltpu.VMEM((t, t), F32), pltpu.VMEM((t, t), F32),
                        pltpu.VMEM((t, d), F32), pltpu.VMEM((t, d), F32), pltpu.VMEM((t, d), F32)],
        compiler_params=_params("parallel", "parallel", "arbitrary"),
        name="mla_attn",
    )(q, cos, sin, kv, kv, kpe)


def _sb_layer(h, gain, w_qkv, w_o, B, S):
    M, D = h.shape
    H = w_o.shape[0] // HEAD_DIM
    col_scale = jnp.where(jnp.arange(w_qkv.shape[1]) < H * HEAD_DIM, HEAD_DIM ** -0.5 * LOG2E, 1.0)
    qkv = _linear(h, (w_qkv * col_scale).astype(BF16), name="sb_qkv", gain=gain, out_dtype=BF16)
    o = _sb_attention(qkv.reshape(B, S, -1), H)
    return _linear(o.reshape(M, -1), w_o.astype(BF16), name="sb_out", residual=h)


def _hg_layer(h, gain, w_in, g_norm, w_o, lb, B, S):
    M, D = h.shape
    H = w_o.shape[0] // HEAD_DIM
    proj = _linear(h, w_in.astype(BF16), name="hg_in", gain=gain, out_dtype=F32)
    o = _hgrn2(proj.reshape(B, S, -1), lb, g_norm, H)
    return _linear(o.reshape(M, -1), w_o.astype(BF16), name="hg_out", residual=h)


def _mla_layer(h, positions, gain, w_dkv, q_norm, kv_norm, w_uq, w_ukv, w_o, B, S):
    M, D = h.shape
    H = w_o.shape[0] // MLA_NOPE
    qr, kvr = MLA_Q_RANK, MLA_KV_RANK
    pad_a = jnp.zeros((D, qr - kvr), F32)
    pad_b = jnp.zeros((D, HEAD_DIM - MLA_ROPE), F32)
    w_d = jnp.concatenate([w_dkv[:, qr:qr + kvr], pad_a, w_dkv[:, :qr], w_dkv[:, qr + kvr:], pad_b], axis=1)
    dkv = _linear(h, w_d.astype(BF16), name="mla_down", gain=gain, out_dtype=F32)
    kpe, cos, sin = _rope_tables(positions, dkv, (2 * qr) // HEAD_DIM)
    w_q = w_uq.reshape(qr, H, MLA_NOPE + MLA_ROPE) * ((MLA_NOPE + MLA_ROPE) ** -0.5 * LOG2E)
    w_q = jnp.concatenate([w_q, jnp.zeros((qr, H, HEAD_DIM - MLA_ROPE), F32)], axis=2).reshape(qr, H * 2 * HEAD_DIM)
    q = _linear(dkv, w_q.astype(BF16), name="mla_q", gain=q_norm, out_dtype=BF16, x_col=1)
    kv = _linear(dkv, w_ukv.astype(BF16), name="mla_kv", gain=kv_norm, out_dtype=BF16, x_col=0)
    o = _mla_attention(q.reshape(B, S, -1), cos.reshape(B, S, -1), sin.reshape(B, S, -1),
                       kv.reshape(B, S, -1), kpe.reshape(B, S, -1), H)
    return _linear(o.reshape(M, -1), w_o.astype(BF16), name="mla_out", residual=h)


def _final_norm_kernel(x_ref, g_ref, o_ref):
    x = x_ref[...]
    ms = jnp.mean(x * x, axis=-1, keepdims=True)
    o_ref[...] = x * lax.rsqrt(ms + RMS_EPS) * g_ref[...]


def _final_norm(h, gain, *, tm=512):
    M, D = h.shape
    return pl.pallas_call(
        _final_norm_kernel,
        grid=(M // tm,),
        in_specs=[pl.BlockSpec((tm, D), lambda i: (i, 0)), pl.BlockSpec((1, D), lambda i: (0, 0))],
        out_specs=pl.BlockSpec((tm, D), lambda i: (i, 0)),
        out_shape=jax.ShapeDtypeStruct((M, D), F32),
        compiler_params=_params("parallel"),
        name="final_norm",
    )(h, gain.reshape(1, D))


def kernel(x, positions, norm_mix, norm_mlp, final_norm, sb_w_qkv, sb_w_o, hg_w_in, hg_lb_logits, hg_g_norm, hg_w_o, mla_w_dkv, mla_q_norm, mla_kv_norm, mla_w_uq, mla_w_ukv, mla_w_o, mlp_w1, mlp_w2):
    B, S, D = x.shape
    depth = norm_mix.shape[0]
    p_lb = jax.nn.softmax(hg_lb_logits.astype(F32), axis=0)
    lb_all = jnp.cumsum(p_lb, axis=0) - p_lb[0]
    h = x.reshape(B * S, D)
    w1, w2 = mlp_w1.astype(BF16), mlp_w2.astype(BF16)
    for i in range(depth):
        m, j = i % N_MIXERS, i // N_MIXERS
        if m == 0:
            h = _sb_layer(h, norm_mix[i], sb_w_qkv[j], sb_w_o[j], B, S)
        elif m == 1:
            h = _hg_layer(h, norm_mix[i], hg_w_in[j], hg_g_norm[j], hg_w_o[j], lb_all[i], B, S)
        else:
            h = _mla_layer(h, positions, norm_mix[i], mla_w_dkv[j], mla_q_norm[j], mla_kv_norm[j],
                           mla_w_uq[j], mla_w_ukv[j], mla_w_o[j], B, S)
        h = _mlp(h, norm_mlp[i], w1, w2, i)
    return _final_norm(h, final_norm).reshape(B, S, D)
```

```python
import functools

import jax
import jax.numpy as jnp
from jax import lax
from jax.experimental import pallas as pl
from jax.experimental.pallas import tpu as pltpu

F32 = jnp.float32
BF16 = jnp.bfloat16

RMS_EPS = 1e-6
N_MIXERS = 3
HEAD_DIM = 128
MLA_NOPE = 128
MLA_ROPE = 64
MLA_Q_RANK = 768
MLA_KV_RANK = 512
MLA_CHUNK = 64
ROPE_THETA = 10000.0
LOG2E = 1.4426950408889634
VMEM_LIMIT = 56 * 1024 * 1024


def _params(*sem):
    return pltpu.CompilerParams(dimension_semantics=sem, vmem_limit_bytes=VMEM_LIMIT)


def _dot(a, b):
    return jnp.dot(a, b, preferred_element_type=F32)


def _dot_nt(a, b):
    return lax.dot_general(a, b, (((1,), (1,)), ((), ())), preferred_element_type=F32)


def _dot_tn(a, b):
    return lax.dot_general(a, b, (((0,), (0,)), ((), ())), preferred_element_type=F32)


def _split3(x):
    hi = x.astype(BF16)
    r1 = x - hi.astype(F32)
    mid = r1.astype(BF16)
    lo = (r1 - mid.astype(F32)).astype(BF16)
    return hi, mid, lo


def _rotate_half(x, cos, sin):
    half = MLA_ROPE // 2
    swapped = pltpu.roll(x, half, 1) + pltpu.roll(x, HEAD_DIM - half, 1)
    return x * cos + swapped * sin


def _linear_kernel(*refs, has_gain, has_res, has_rope):
    x_ref, w_ref = refs[0], refs[1]
    pos = 2
    g_ref = r_ref = cos_ref = sin_ref = None
    if has_gain:
        g_ref = refs[pos]
        pos += 1
    if has_res:
        r_ref = refs[pos]
        pos += 1
    if has_rope:
        cos_ref, sin_ref = refs[pos], refs[pos + 1]
        pos += 2
    o_ref = refs[pos]

    if has_gain:
        xn_ref = refs[pos + 1]

        @pl.when(pl.program_id(1) == 0)
        def _():
            x = x_ref[...].astype(F32)
            ms = jnp.mean(x * x, axis=-1, keepdims=True)
            xn_ref[...] = (x * lax.rsqrt(ms + RMS_EPS) * g_ref[...]).astype(BF16)

        lhs = xn_ref[...]
    else:
        lhs = x_ref[...]

    acc = _dot(lhs, w_ref[...])
    if has_res:
        acc = acc + r_ref[...]
    if has_rope:
        cos, sin = cos_ref[...], sin_ref[...]
        for c in range(0, acc.shape[1], 2 * HEAD_DIM):
            o_ref[:, c:c + HEAD_DIM] = acc[:, c:c + HEAD_DIM].astype(o_ref.dtype)
            rot = _rotate_half(acc[:, c + HEAD_DIM:c + 2 * HEAD_DIM], cos, sin)
            o_ref[:, c + HEAD_DIM:c + 2 * HEAD_DIM] = rot.astype(o_ref.dtype)
    else:
        o_ref[...] = acc.astype(o_ref.dtype)


LINEAR_VMEM_BUDGET = 44 * 1024 * 1024
RESIDENT_WEIGHT_BYTES = 8 * 1024 * 1024


def _linear_tiles(M, K, N, x_bytes, out_bytes, has_gain, has_res, has_rope):
    tn = N if K * N * 2 <= RESIDENT_WEIGHT_BYTES or N % 1024 else 1024
    for tm in (1024, 512, 256, 128):
        if M % tm:
            continue
        need = 2 * tm * K * x_bytes + 2 * K * tn * 2 + 2 * tm * tn * out_bytes
        need += tm * K * 2 if has_gain else 0
        need += 2 * tm * tn * 4 if has_res else 0
        need += tm * tn * 4 + 4 * tm * HEAD_DIM * 4 if has_rope else 0
        if need <= LINEAR_VMEM_BUDGET:
            return tm, tn
    raise ValueError("no linear tiling fits VMEM")


def _linear(x, w, *, name, gain=None, residual=None, rope=None, out_dtype=F32, x_col=0):
    M = x.shape[0]
    K, N = w.shape
    assert gain is not None or x.dtype == BF16
    tm, tn = _linear_tiles(M, K, N, x.dtype.itemsize, jnp.dtype(out_dtype).itemsize,
                           gain is not None, residual is not None, rope is not None)
    assert rope is None or tn % (2 * HEAD_DIM) == 0
    in_specs = [pl.BlockSpec((tm, K), lambda i, j: (i, x_col)),
                pl.BlockSpec((K, tn), lambda i, j: (0, j))]
    args = [x, w]
    if gain is not None:
        in_specs.append(pl.BlockSpec((1, K), lambda i, j: (0, 0)))
        args.append(gain.reshape(1, K).astype(F32))
    if residual is not None:
        in_specs.append(pl.BlockSpec((tm, tn), lambda i, j: (i, j)))
        args.append(residual)
    if rope is not None:
        in_specs += [pl.BlockSpec((tm, HEAD_DIM), lambda i, j: (i, 0))] * 2
        args += list(rope)
    return pl.pallas_call(
        functools.partial(_linear_kernel, has_gain=gain is not None, has_res=residual is not None,
                          has_rope=rope is not None),
        grid=(M // tm, N // tn),
        in_specs=in_specs,
        out_specs=pl.BlockSpec((tm, tn), lambda i, j: (i, j)),
        out_shape=jax.ShapeDtypeStruct((M, N), out_dtype),
        scratch_shapes=[pltpu.VMEM((tm, K), BF16)] if gain is not None else [],
        compiler_params=_params("parallel", "arbitrary"),
        name=name,
    )(*args)


def _mlp_kernel(x_ref, g_ref, w1_ref, w2_ref, o_ref, xn_ref, acc_ref):
    f = pl.program_id(1)

    @pl.when(f == 0)
    def _():
        x = x_ref[...]
        ms = jnp.mean(x * x, axis=-1, keepdims=True)
        xn_ref[...] = (x * lax.rsqrt(ms + RMS_EPS) * g_ref[...]).astype(BF16)
        acc_ref[...] = jnp.zeros_like(acc_ref)

    u = _dot(xn_ref[...], w1_ref[...])
    r = jnp.square(jnp.maximum(u, 0.0)).astype(BF16)
    acc_ref[...] += _dot(r, w2_ref[...])

    @pl.when(f == pl.num_programs(1) - 1)
    def _():
        o_ref[...] = x_ref[...] + acc_ref[...]


def _mlp(h, gain, w1, w2, layer, *, tm=512, tf=1024):
    M, D = h.shape
    F = w1.shape[2]
    assert M % tm == 0 and F % tf == 0
    return pl.pallas_call(
        _mlp_kernel,
        grid=(M // tm, F // tf),
        in_specs=[pl.BlockSpec((tm, D), lambda i, f: (i, 0)),
                  pl.BlockSpec((1, D), lambda i, f: (0, 0)),
                  pl.BlockSpec((None, D, tf), lambda i, f: (layer, 0, f)),
                  pl.BlockSpec((None, tf, D), lambda i, f: (layer, f, 0))],
        out_specs=pl.BlockSpec((tm, D), lambda i, f: (i, 0)),
        out_shape=jax.ShapeDtypeStruct((M, D), F32),
        scratch_shapes=[pltpu.VMEM((tm, D), BF16), pltpu.VMEM((tm, D), F32)],
        compiler_params=_params("parallel", "arbitrary"),
        name="mlp",
    )(h, gain.reshape(1, D).astype(F32), w1, w2)


SB_SKIP = 110.0


def _sb_kernel(q_ref, k_ref, v_ref, o_ref, *, t, n_sub):
    i = pl.program_id(2)
    row = lax.broadcasted_iota(jnp.int32, (t, t), 0)
    col = lax.broadcasted_iota(jnp.int32, (t, t), 1)
    later = (row > col).astype(BF16)
    causal = col < row

    def gates(q, j, diagonal):
        kb = k_ref[0, pl.ds(pl.multiple_of(j * t, t), t), :]
        z = _dot_nt(q, kb)
        nz = -z
        lom = jnp.minimum(nz, 0.0) - jnp.log2(1.0 + jnp.exp2(jnp.minimum(z, nz)))
        log_beta = lom + z
        if diagonal:
            lom = jnp.where(causal, lom, 0.0)
        hi = lom.astype(BF16)
        lo = (lom - hi.astype(F32)).astype(BF16)
        within = _dot(hi, later) + _dot(lo, later)
        return log_beta, within, jnp.sum(lom, axis=1, keepdims=True)

    def weigh(j, log_beta, after, diagonal):
        a = jnp.exp2(log_beta + after)
        if diagonal:
            a = jnp.where(causal, a, 0.0)
        return _dot(a.astype(BF16), v_ref[0, pl.ds(pl.multiple_of(j * t, t), t), :])

    skip = -SB_SKIP * LOG2E
    near = []
    pending = jnp.full((t, 1), skip, F32)
    for a in range(n_sub):
        rows = slice(a * t, (a + 1) * t)
        q = q_ref[0, rows, :]
        jd = i * n_sub + a
        has_prev = jd > 0
        jp = jnp.maximum(jd - 1, 0)
        lb_d, within_d, carry = gates(q, jd, True)
        lb_p, within_p, total_p = gates(q, jp, False)
        acc = weigh(jd, lb_d, within_d, True) + jnp.where(has_prev, weigh(jp, lb_p, within_p + carry, False), 0.0)
        carry = carry + jnp.where(has_prev, total_p, 0.0)
        o_ref[0, rows, :] = acc.astype(o_ref.dtype)
        near.append((rows, q, jd, carry, acc))
        pending = jnp.maximum(pending, jnp.where(jd >= 2, carry, skip))

    def cond(s):
        j, carry, _ = s
        return jnp.logical_and(j >= 0, jnp.max(carry) > skip)

    @pl.when(jnp.max(pending) > skip)
    def _():
        for rows, q, jd, carry, acc in near:
            def body(s, q=q):
                j, carry, acc = s
                log_beta, within, total = gates(q, j, False)
                return j - 1, carry + total, acc + weigh(j, log_beta, within + carry, False)

            _, _, acc = lax.while_loop(cond, body, (jd - 2, carry, acc))
            o_ref[0, rows, :] = acc.astype(o_ref.dtype)


def _sb_attention(qkv, n_heads, *, t=256, n_sub=4):
    B, S, _ = qkv.shape
    d = HEAD_DIM
    tile = t * n_sub
    assert S % tile == 0
    return pl.pallas_call(
        functools.partial(_sb_kernel, t=t, n_sub=n_sub),
        grid=(B, n_heads, S // tile),
        in_specs=[pl.BlockSpec((1, tile, d), lambda b, h, i: (b, i, h)),
                  pl.BlockSpec((1, S, d), lambda b, h, i: (b, 0, n_heads + h)),
                  pl.BlockSpec((1, S, d), lambda b, h, i: (b, 0, 2 * n_heads + h))],
        out_specs=pl.BlockSpec((1, tile, d), lambda b, h, i: (b, i, h)),
        out_shape=jax.ShapeDtypeStruct((B, S, n_heads * d), BF16),
        compiler_params=_params("parallel", "parallel", "arbitrary"),
        name="sb_attn",
    )(qkv, qkv, qkv)


HG_CHUNK = 64
HG_SUB = 16
HG_GROUP = 2
HG_SPAN = 256
HG_SAFE = 80.0


def _hg_kernel(q_ref, fz_ref, v_ref, g_ref, lb_ref, gn_ref, tril_ref, o_ref, state_ref, oc_ref, b_ref, k_ref,
               *, n_chunks):
    C, U, d = HG_CHUNK, HG_SUB, HEAD_DIM

    @pl.when(pl.program_id(2) == 0)
    def _():
        state_ref[...] = jnp.zeros_like(state_ref)

    gn = gn_ref[...]
    row = lax.broadcasted_iota(jnp.int32, (C, C), 0)
    col = lax.broadcasted_iota(jnp.int32, (C, C), 1)
    lower = col <= row
    sub_row = lax.broadcasted_iota(jnp.int32, (U, 1), 0)
    tril = tril_ref[...]

    heads = []
    for hd in range(HG_GROUP):
        lanes = slice(hd * d, (hd + 1) * d)
        lb = lb_ref[:, lanes]
        fz = fz_ref[0, :, lanes]
        e = jnp.exp(-jnp.abs(fz))
        inv = 1.0 / (1.0 + e)
        sig_pos = jnp.where(fz >= 0, inv, e * inv)
        sig_neg = jnp.where(fz >= 0, e * inv, inv)
        log_f = jnp.log(lb + (1.0 - lb) * sig_pos)
        k = (1.0 - lb) * sig_neg
        l_hi = log_f.astype(BF16)
        l_lo = (log_f - l_hi.astype(F32)).astype(BF16)
        span = tril.shape[0]
        b = jnp.concatenate([_dot(tril, l_hi[r:r + span]) + _dot(tril, l_lo[r:r + span])
                             for r in range(0, fz.shape[0], span)], axis=0)
        heads.append((lanes, k, b))
    safe = functools.reduce(jnp.minimum, [jnp.min(b) for _, _, b in heads]) > -HG_SAFE

    def epilogue(o, g):
        o = o * lax.rsqrt(jnp.mean(o * o, axis=-1, keepdims=True) + RMS_EPS) * gn
        return (o * (g / (1.0 + jnp.exp(-g)))).astype(o_ref.dtype)

    @pl.when(safe)
    def _():
        for hd, (lanes, k, b) in enumerate(heads):
            q = q_ref[0, :, lanes]
            v16 = v_ref[0, :, lanes].astype(BF16)
            qe = (q * jnp.exp(b)).astype(BF16)
            ke = (k * jnp.exp(-b)).astype(BF16)
            state_t = state_ref[hd]
            for c in range(n_chunks):
                rows = slice(c * C, (c + 1) * C)
                s = jnp.where(lower, _dot_nt(qe[rows], ke[rows]), 0.0)
                oc_ref[hd, rows, :] = _dot(s.astype(BF16), v16[rows]) + _dot_nt(qe[rows], state_t.astype(BF16))
                b_last = b[(c + 1) * C - 1:(c + 1) * C, :]
                kd = (k[rows] * jnp.exp(b_last - b[rows])).astype(BF16)
                state_t = state_t * jnp.exp(b_last) + _dot_tn(v16[rows], kd)
            state_ref[hd] = state_t
            o_ref[0, :, lanes] = epilogue(oc_ref[hd], g_ref[0, :, lanes])

    def head_chunk(hd, r0):
        lanes = slice(hd * d, (hd + 1) * d)
        q = q_ref[0, pl.ds(r0, C), lanes]
        v = v_ref[0, pl.ds(r0, C), lanes]
        b = b_ref[hd, pl.ds(r0, C), :]
        k = k_ref[hd, pl.ds(r0, C), :]
        log2_k = jnp.log2(k)
        state_t = state_ref[hd]
        rows = pl.ds(r0, C)
        oc_ref[hd, rows, :] = _dot_nt((q * jnp.exp(b)).astype(BF16), state_t.astype(BF16))
        v16 = v.astype(BF16)
        for J in range(C // U - 1):
            lo_r, hi_r = J * U, (J + 1) * U
            ref_b = b[hi_r - 1:hi_r, :]
            kt = (k[lo_r:hi_r] * jnp.exp(ref_b - b[lo_r:hi_r])).astype(BF16)
            qs = (q[hi_r:] * jnp.exp(b[hi_r:] - ref_b)).astype(BF16)
            s = _dot_nt(qs, kt)
            oc_ref[hd, pl.ds(r0 + hi_r, C - hi_r), :] += _dot(s.astype(BF16), v16[lo_r:hi_r])
        b2 = b * LOG2E
        key2 = log2_k - b2
        for J in range(C // U):
            lo_r, hi_r = J * U, (J + 1) * U
            qj, bj, kj, vj = q[lo_r:hi_r], b2[lo_r:hi_r], key2[lo_r:hi_r], v[lo_r:hi_r]
            acc = jnp.zeros((U, d), F32)
            for s_i in range(U):
                w = qj * jnp.exp2(jnp.minimum(bj + kj[s_i:s_i + 1], 0.0))
                sc = jnp.sum(w, axis=1, keepdims=True)
                sc = jnp.where(sub_row >= s_i, sc, 0.0)
                acc = acc + sc * vj[s_i:s_i + 1]
            oc_ref[hd, pl.ds(r0 + lo_r, U), :] += acc
        b_last = b[C - 1:C, :]
        kd = (k * jnp.exp(b_last - b)).astype(BF16)
        state_ref[hd] = state_t * jnp.exp(b_last) + _dot_tn(v16, kd)
        o_ref[0, rows, lanes] = epilogue(oc_ref[hd, rows, :], g_ref[0, rows, lanes])

    @pl.when(jnp.logical_not(safe))
    def _():
        for hd, (_, k, b) in enumerate(heads):
            b_ref[hd] = b
            k_ref[hd] = k

        def chunk(c, _):
            r0 = pl.multiple_of(c * C, C)
            for hd in range(HG_GROUP):
                head_chunk(hd, r0)
            return 0

        lax.fori_loop(0, n_chunks, chunk, 0)


def _hgrn2(proj, lb, g_norm, n_heads, *, ts=512):
    B, S, _ = proj.shape
    d, G = HEAD_DIM, HG_GROUP
    assert S % ts == 0 and ts % HG_CHUNK == 0 and n_heads % G == 0
    n_groups = n_heads // G

    def col(part):
        return pl.BlockSpec((1, ts, G * d), lambda b, h, s: (b, s, part * n_groups + h))

    span = min(ts, HG_SPAN)
    r = jnp.arange(span)
    tril = ((r[None, :] <= r[:, None]) & (r[None, :] // HG_CHUNK == r[:, None] // HG_CHUNK)).astype(BF16)
    return pl.pallas_call(
        functools.partial(_hg_kernel, n_chunks=ts // HG_CHUNK),
        grid=(B, n_groups, S // ts),
        in_specs=[col(0), col(1), col(2), col(3),
                  pl.BlockSpec((1, G * d), lambda b, h, s: (0, h)),
                  pl.BlockSpec((1, d), lambda b, h, s: (0, 0)),
                  pl.BlockSpec((span, span), lambda b, h, s: (0, 0))],
        out_specs=pl.BlockSpec((1, ts, G * d), lambda b, h, s: (b, s, h)),
        out_shape=jax.ShapeDtypeStruct((B, S, n_heads * d), BF16),
        scratch_shapes=[pltpu.VMEM((G, d, d), F32), pltpu.VMEM((G, ts, d), F32),
                        pltpu.VMEM((G, ts, d), F32), pltpu.VMEM((G, ts, d), F32)],
        compiler_params=_params("parallel", "parallel", "arbitrary"),
        name="hgrn2",
    )(proj, proj, proj, proj, lb.reshape(1, n_heads * d), g_norm.reshape(1, d), tril)


def _rope_kernel(pos_ref, invf_ref, x_ref, kpe_ref, cos_ref, sin_ref):
    half = MLA_ROPE // 2
    ang = pos_ref[...].astype(F32) * invf_ref[...]
    lane = lax.broadcasted_iota(jnp.int32, ang.shape, 1)
    c = jnp.where(lane < MLA_ROPE, jnp.cos(ang), 0.0)
    s = jnp.sin(ang)
    s = jnp.where(lane < half, -s, jnp.where(lane < MLA_ROPE, s, 0.0))
    cos_ref[...] = c
    sin_ref[...] = s
    kpe_ref[...] = _rotate_half(x_ref[...], c, s).astype(kpe_ref.dtype)


def _rope_tables(positions, dkv, kpe_col, *, tm=512):
    M = dkv.shape[0]
    half = MLA_ROPE // 2
    freq = ROPE_THETA ** (-jnp.arange(0, MLA_ROPE, 2, dtype=F32) / MLA_ROPE)
    invf = jnp.concatenate([freq, freq, jnp.zeros((HEAD_DIM - 2 * half,), F32)]).reshape(1, HEAD_DIM)
    row = pl.BlockSpec((tm, HEAD_DIM), lambda i: (i, 0))
    return pl.pallas_call(
        _rope_kernel,
        grid=(M // tm,),
        in_specs=[pl.BlockSpec((tm, 1), lambda i: (i, 0)),
                  pl.BlockSpec((1, HEAD_DIM), lambda i: (0, 0)),
                  pl.BlockSpec((tm, HEAD_DIM), lambda i: (i, kpe_col))],
        out_specs=[row, row, row],
        out_shape=[jax.ShapeDtypeStruct((M, HEAD_DIM), BF16),
                   jax.ShapeDtypeStruct((M, HEAD_DIM), F32),
                   jax.ShapeDtypeStruct((M, HEAD_DIM), F32)],
        compiler_params=_params("parallel"),
        name="mla_rope",
    )(positions.reshape(M, 1), invf, dkv)


MLA_ROW_CHUNK = 256


def _mla_kernel(q_ref, kn_ref, v_ref, kpe_ref, o_ref, kcat_ref, sa_ref, sb_ref, m_ref, l_ref, acc_ref, *, t):
    n_q = q_ref.shape[1] // t
    n_pairs = n_q * (n_q + 1) // 2
    kcat_ref[:, :MLA_NOPE] = kn_ref[0]
    kcat_ref[:, MLA_NOPE:] = kpe_ref[0]
    rc = MLA_ROW_CHUNK
    row = lax.broadcasted_iota(jnp.int32, (rc, t), 0) // MLA_CHUNK
    col = lax.broadcasted_iota(jnp.int32, (rc, t), 1) // MLA_CHUNK

    def tile(n):
        return pl.ds(pl.multiple_of(n * t, t), t)

    def scores(i, j, dst_ref):
        dst_ref[...] = _dot_nt(q_ref[0, tile(i), :], kcat_ref[tile(j), :])

    def reset():
        m_ref[...] = jnp.full_like(m_ref, -jnp.inf)
        l_ref[...] = jnp.zeros_like(l_ref)
        acc_ref[...] = jnp.zeros_like(acc_ref)

    def absorb(j, src_ref, diagonal):
        vb = v_ref[0, tile(j), :]
        for r in range(t // rc):
            rows = slice(r * rc, (r + 1) * rc)
            s = src_ref[rows, :]
            if diagonal:
                s = jnp.where(col <= row + (r * rc) // MLA_CHUNK, s, -jnp.inf)
            m_old = m_ref[rows, :]
            lane_groups = [s[:, c:c + HEAD_DIM] for c in range(0, t, HEAD_DIM)]
            m_col = jnp.max(functools.reduce(jnp.maximum, lane_groups + [m_old]), axis=1, keepdims=True)
            m_new = jnp.broadcast_to(m_col, m_old.shape)
            alpha = jnp.exp2(m_old - m_new)
            p = jnp.exp2(s - m_col)
            l_ref[rows, :] = alpha * l_ref[rows, :] + jnp.sum(p, axis=1, keepdims=True)
            acc_ref[rows, :] = alpha * acc_ref[rows, :] + _dot(p.astype(BF16), vb)
            m_ref[rows, :] = m_new

    def successor(i, j):
        last = j == i
        return jnp.where(last, i + 1, i), jnp.where(last, 0, j + 1)

    def step(cur_ref, nxt_ref, i, j):
        ni, nj = successor(i, j)
        ni = jnp.minimum(ni, n_q - 1)

        @pl.when(j < i)
        def _():
            scores(ni, nj, nxt_ref)
            absorb(j, cur_ref, False)

        @pl.when(j == i)
        def _():
            scores(ni, nj, nxt_ref)
            absorb(j, cur_ref, True)
            o_ref[0, tile(i), :] = (acc_ref[...] / l_ref[...]).astype(o_ref.dtype)
            reset()

    def two_steps(_, ij):
        i, j = ij
        i1, j1 = successor(i, j)

        @pl.when(j + 1 < i)
        def _():
            scores(i, j + 1, sb_ref)
            absorb(j, sa_ref, False)
            scores(i, j + 2, sa_ref)
            absorb(j + 1, sb_ref, False)

        @pl.when(j + 1 >= i)
        def _():
            step(sa_ref, sb_ref, i, j)
            step(sb_ref, sa_ref, i1, j1)

        return successor(i1, j1)

    reset()
    scores(0, 0, sa_ref)
    ij = lax.fori_loop(0, n_pairs // 2, two_steps, (jnp.int32(0), jnp.int32(0)))
    if n_pairs % 2:
        step(sa_ref, sb_ref, *ij)


def _mla_attention(q, kv, kpe, n_heads, *, t=512):
    B, S, _ = q.shape
    d = HEAD_DIM
    assert S % t == 0 and t % MLA_CHUNK == 0
    return pl.pallas_call(
        functools.partial(_mla_kernel, t=t),
        grid=(B, n_heads),
        in_specs=[pl.BlockSpec((1, S, 2 * d), lambda b, h: (b, 0, h)),
                  pl.BlockSpec((1, S, d), lambda b, h: (b, 0, 2 * h)),
                  pl.BlockSpec((1, S, d), lambda b, h: (b, 0, 2 * h + 1)),
                  pl.BlockSpec((1, S, d), lambda b, h: (b, 0, 0))],
        out_specs=pl.BlockSpec((1, S, d), lambda b, h: (b, 0, h)),
        out_shape=jax.ShapeDtypeStruct((B, S, n_heads * d), BF16),
        scratch_shapes=[pltpu.VMEM((S, 2 * d), BF16), pltpu.VMEM((t, t), F32), pltpu.VMEM((t, t), F32),
                        pltpu.VMEM((t, d), F32), pltpu.VMEM((t, d), F32), pltpu.VMEM((t, d), F32)],
        compiler_params=_params("parallel", "parallel"),
        name="mla_attn",
    )(q, kv, kv, kpe)


def _sb_layer(h, gain, w_qkv, w_o, B, S):
    M, D = h.shape
    H = w_o.shape[0] // HEAD_DIM
    col_scale = jnp.where(jnp.arange(w_qkv.shape[1]) < H * HEAD_DIM, HEAD_DIM ** -0.5 * LOG2E, 1.0)
    qkv = _linear(h, (w_qkv * col_scale).astype(BF16), name="sb_qkv", gain=gain, out_dtype=BF16)
    o = _sb_attention(qkv.reshape(B, S, -1), H)
    return _linear(o.reshape(M, -1), w_o.astype(BF16), name="sb_out", residual=h)


def _hg_layer(h, gain, w_in, g_norm, w_o, lb, B, S):
    M, D = h.shape
    H = w_o.shape[0] // HEAD_DIM
    proj = _linear(h, w_in.astype(BF16), name="hg_in", gain=gain, out_dtype=F32)
    o = _hgrn2(proj.reshape(B, S, -1), lb, g_norm, H)
    return _linear(o.reshape(M, -1), w_o.astype(BF16), name="hg_out", residual=h)


def _mla_layer(h, positions, gain, w_dkv, q_norm, kv_norm, w_uq, w_ukv, w_o, B, S):
    M, D = h.shape
    H = w_o.shape[0] // MLA_NOPE
    qr, kvr = MLA_Q_RANK, MLA_KV_RANK
    pad_a = jnp.zeros((D, qr - kvr), F32)
    pad_b = jnp.zeros((D, HEAD_DIM - MLA_ROPE), F32)
    w_d = jnp.concatenate([w_dkv[:, qr:qr + kvr], pad_a, w_dkv[:, :qr], w_dkv[:, qr + kvr:], pad_b], axis=1)
    dkv = _linear(h, w_d.astype(BF16), name="mla_down", gain=gain, out_dtype=F32)
    kpe, cos, sin = _rope_tables(positions, dkv, (2 * qr) // HEAD_DIM)
    w_q = w_uq.reshape(qr, H, MLA_NOPE + MLA_ROPE) * ((MLA_NOPE + MLA_ROPE) ** -0.5 * LOG2E)
    w_q = jnp.concatenate([w_q, jnp.zeros((qr, H, HEAD_DIM - MLA_ROPE), F32)], axis=2).reshape(qr, H * 2 * HEAD_DIM)
    q = _linear(dkv, w_q.astype(BF16), name="mla_q", gain=q_norm, rope=(cos, sin), out_dtype=BF16, x_col=1)
    kv = _linear(dkv, w_ukv.astype(BF16), name="mla_kv", gain=kv_norm, out_dtype=BF16, x_col=0)
    o = _mla_attention(q.reshape(B, S, -1), kv.reshape(B, S, -1), kpe.reshape(B, S, -1), H)
    return _linear(o.reshape(M, -1), w_o.astype(BF16), name="mla_out", residual=h)


def _final_norm_kernel(x_ref, g_ref, o_ref):
    x = x_ref[...]
    ms = jnp.mean(x * x, axis=-1, keepdims=True)
    o_ref[...] = x * lax.rsqrt(ms + RMS_EPS) * g_ref[...]


def _final_norm(h, gain, *, tm=512):
    M, D = h.shape
    return pl.pallas_call(
        _final_norm_kernel,
        grid=(M // tm,),
        in_specs=[pl.BlockSpec((tm, D), lambda i: (i, 0)), pl.BlockSpec((1, D), lambda i: (0, 0))],
        out_specs=pl.BlockSpec((tm, D), lambda i: (i, 0)),
        out_shape=jax.ShapeDtypeStruct((M, D), F32),
        compiler_params=_params("parallel"),
        name="final_norm",
    )(h, gain.reshape(1, D))


def kernel(x, positions, norm_mix, norm_mlp, final_norm, sb_w_qkv, sb_w_o, hg_w_in, hg_lb_logits, hg_g_norm, hg_w_o, mla_w_dkv, mla_q_norm, mla_kv_norm, mla_w_uq, mla_w_ukv, mla_w_o, mlp_w1, mlp_w2):
    B, S, D = x.shape
    depth = norm_mix.shape[0]
    p_lb = jax.nn.softmax(hg_lb_logits.astype(F32), axis=0)
    lb_all = jnp.cumsum(p_lb, axis=0) - p_lb[0]
    h = x.reshape(B * S, D)
    w1, w2 = mlp_w1.astype(BF16), mlp_w2.astype(BF16)
    for i in range(depth):
        m, j = i % N_MIXERS, i // N_MIXERS
        if m == 0:
            h = _sb_layer(h, norm_mix[i], sb_w_qkv[j], sb_w_o[j], B, S)
        elif m == 1:
            h = _hg_layer(h, norm_mix[i], hg_w_in[j], hg_g_norm[j], hg_w_o[j], lb_all[i], B, S)
        else:
            h = _mla_layer(h, positions, norm_mix[i], mla_w_dkv[j], mla_q_norm[j], mla_kv_norm[j],
                           mla_w_uq[j], mla_w_ukv[j], mla_w_o[j], B, S)
        h = _mlp(h, norm_mlp[i], w1, w2, i)
    return _final_norm(h, final_norm).reshape(B, S, D)
```

```python
import functools

import jax
import jax.numpy as jnp
from jax import lax
from jax.experimental import pallas as pl
from jax.experimental.pallas import tpu as pltpu

F32 = jnp.float32
BF16 = jnp.bfloat16

RMS_EPS = 1e-6
N_MIXERS = 3
HEAD_DIM = 128
MLA_NOPE = 128
MLA_ROPE = 64
MLA_Q_RANK = 768
MLA_KV_RANK = 512
MLA_CHUNK = 64
ROPE_THETA = 10000.0
LOG2E = 1.4426950408889634
VMEM_LIMIT = 56 * 1024 * 1024


def _params(*sem):
    return pltpu.CompilerParams(dimension_semantics=sem, vmem_limit_bytes=VMEM_LIMIT)


def _dot(a, b):
    return jnp.dot(a, b, preferred_element_type=F32)


def _dot_nt(a, b):
    return lax.dot_general(a, b, (((1,), (1,)), ((), ())), preferred_element_type=F32)


def _dot_tn(a, b):
    return lax.dot_general(a, b, (((0,), (0,)), ((), ())), preferred_element_type=F32)


def _split3(x):
    hi = x.astype(BF16)
    r1 = x - hi.astype(F32)
    mid = r1.astype(BF16)
    lo = (r1 - mid.astype(F32)).astype(BF16)
    return hi, mid, lo


def _rotate_half(x, cos, sin):
    half = MLA_ROPE // 2
    swapped = pltpu.roll(x, half, 1) + pltpu.roll(x, HEAD_DIM - half, 1)
    return x * cos + swapped * sin


def _linear_kernel(*refs, has_gain, has_res, has_rope):
    x_ref, w_ref = refs[0], refs[1]
    pos = 2
    g_ref = r_ref = cos_ref = sin_ref = None
    if has_gain:
        g_ref = refs[pos]
        pos += 1
    if has_res:
        r_ref = refs[pos]
        pos += 1
    if has_rope:
        cos_ref, sin_ref = refs[pos], refs[pos + 1]
        pos += 2
    o_ref = refs[pos]

    if has_gain:
        xn_ref = refs[pos + 1]

        @pl.when(pl.program_id(1) == 0)
        def _():
            x = x_ref[...].astype(F32)
            ms = jnp.mean(x * x, axis=-1, keepdims=True)
            xn_ref[...] = (x * lax.rsqrt(ms + RMS_EPS) * g_ref[...]).astype(BF16)

        lhs = xn_ref[...]
    else:
        lhs = x_ref[...]

    acc = _dot(lhs, w_ref[...])
    if has_res:
        acc = acc + r_ref[...]
    if has_rope:
        cos, sin = cos_ref[...], sin_ref[...]
        for c in range(0, acc.shape[1], 2 * HEAD_DIM):
            o_ref[:, c:c + HEAD_DIM] = acc[:, c:c + HEAD_DIM].astype(o_ref.dtype)
            rot = _rotate_half(acc[:, c + HEAD_DIM:c + 2 * HEAD_DIM], cos, sin)
            o_ref[:, c + HEAD_DIM:c + 2 * HEAD_DIM] = rot.astype(o_ref.dtype)
    else:
        o_ref[...] = acc.astype(o_ref.dtype)


LINEAR_VMEM_BUDGET = 44 * 1024 * 1024
RESIDENT_WEIGHT_BYTES = 8 * 1024 * 1024


def _linear_tiles(M, K, N, x_bytes, out_bytes, has_gain, has_res, has_rope):
    tn = N if K * N * 2 <= RESIDENT_WEIGHT_BYTES or N % 1024 else 1024
    for tm in (1024, 512, 256, 128):
        if M % tm:
            continue
        need = 2 * tm * K * x_bytes + 2 * K * tn * 2 + 2 * tm * tn * out_bytes
        need += tm * K * 2 if has_gain else 0
        need += 2 * tm * tn * 4 if has_res else 0
        need += tm * tn * 4 + 4 * tm * HEAD_DIM * 4 if has_rope else 0
        if need <= LINEAR_VMEM_BUDGET:
            return tm, tn
    raise ValueError("no linear tiling fits VMEM")


def _linear(x, w, *, name, gain=None, residual=None, rope=None, out_dtype=F32, x_col=0):
    M = x.shape[0]
    K, N = w.shape
    assert gain is not None or x.dtype == BF16
    tm, tn = _linear_tiles(M, K, N, x.dtype.itemsize, jnp.dtype(out_dtype).itemsize,
                           gain is not None, residual is not None, rope is not None)
    assert rope is None or tn % (2 * HEAD_DIM) == 0
    in_specs = [pl.BlockSpec((tm, K), lambda i, j: (i, x_col)),
                pl.BlockSpec((K, tn), lambda i, j: (0, j))]
    args = [x, w]
    if gain is not None:
        in_specs.append(pl.BlockSpec((1, K), lambda i, j: (0, 0)))
        args.append(gain.reshape(1, K).astype(F32))
    if residual is not None:
        in_specs.append(pl.BlockSpec((tm, tn), lambda i, j: (i, j)))
        args.append(residual)
    if rope is not None:
        in_specs += [pl.BlockSpec((tm, HEAD_DIM), lambda i, j: (i, 0))] * 2
        args += list(rope)
    return pl.pallas_call(
        functools.partial(_linear_kernel, has_gain=gain is not None, has_res=residual is not None,
                          has_rope=rope is not None),
        grid=(M // tm, N // tn),
        in_specs=in_specs,
        out_specs=pl.BlockSpec((tm, tn), lambda i, j: (i, j)),
        out_shape=jax.ShapeDtypeStruct((M, N), out_dtype),
        scratch_shapes=[pltpu.VMEM((tm, K), BF16)] if gain is not None else [],
        compiler_params=_params("parallel", "arbitrary"),
        name=name,
    )(*args)


def _mlp_kernel(*refs, has_out_norm):
    if has_out_norm:
        x_ref, g_ref, w1_ref, w2_ref, go_ref, o_ref, xn_ref = refs
    else:
        x_ref, g_ref, w1_ref, w2_ref, o_ref, xn_ref = refs
    f = pl.program_id(1)

    @pl.when(f == 0)
    def _():
        x = x_ref[...]
        ms = jnp.mean(x * x, axis=-1, keepdims=True)
        xn_ref[...] = (x * lax.rsqrt(ms + RMS_EPS) * g_ref[...]).astype(BF16)
        o_ref[...] = x

    u = _dot(xn_ref[...], w1_ref[...])
    r = jnp.square(jnp.maximum(u, 0.0)).astype(BF16)
    o_ref[...] += _dot(r, w2_ref[...])

    if has_out_norm:
        @pl.when(f == pl.num_programs(1) - 1)
        def _():
            y = o_ref[...]
            ms = jnp.mean(y * y, axis=-1, keepdims=True)
            o_ref[...] = y * lax.rsqrt(ms + RMS_EPS) * go_ref[...]


def _mlp(h, gain, w1, w2, layer, *, out_gain=None, tm=512, tf=1024):
    M, D = h.shape
    F = w1.shape[2]
    assert M % tm == 0 and F % tf == 0
    vec = pl.BlockSpec((1, D), lambda i, f: (0, 0))
    in_specs = [pl.BlockSpec((tm, D), lambda i, f: (i, 0)), vec,
                pl.BlockSpec((None, D, tf), lambda i, f: (layer, 0, f)),
                pl.BlockSpec((None, tf, D), lambda i, f: (layer, f, 0))]
    args = [h, gain.reshape(1, D).astype(F32), w1, w2]
    if out_gain is not None:
        in_specs.append(vec)
        args.append(out_gain.reshape(1, D).astype(F32))
    return pl.pallas_call(
        functools.partial(_mlp_kernel, has_out_norm=out_gain is not None),
        grid=(M // tm, F // tf),
        in_specs=in_specs,
        out_specs=pl.BlockSpec((tm, D), lambda i, f: (i, 0)),
        out_shape=jax.ShapeDtypeStruct((M, D), F32),
        scratch_shapes=[pltpu.VMEM((tm, D), BF16)],
        compiler_params=_params("parallel", "arbitrary"),
        name="mlp",
    )(*args)


SB_SKIP = 110.0


def _sb_kernel(q_ref, k_ref, v_ref, o_ref, kt_ref, *, t, n_sub):
    i = pl.program_id(2)

    @pl.when(i == 0)
    def _():
        span = t * n_sub
        for r in range(0, k_ref.shape[1], span):
            kt_ref[:, r:r + span] = k_ref[0, r:r + span, :].T

    row = lax.broadcasted_iota(jnp.int32, (t, t), 0)
    col = lax.broadcasted_iota(jnp.int32, (t, t), 1)
    later = (row > col).astype(BF16)
    causal = col < row

    def gates(q, j, diagonal):
        z = _dot(q, kt_ref[:, pl.ds(pl.multiple_of(j * t, t), t)])
        nz = -z
        lom = jnp.minimum(nz, 0.0) - jnp.log2(1.0 + jnp.exp2(jnp.minimum(z, nz)))
        log_beta = lom + z
        if diagonal:
            lom = jnp.where(causal, lom, 0.0)
        hi = lom.astype(BF16)
        lo = (lom - hi.astype(F32)).astype(BF16)
        within = _dot(hi, later) + _dot(lo, later)
        return log_beta, within, jnp.sum(lom, axis=1, keepdims=True)

    def weigh(j, log_beta, after, diagonal):
        a = jnp.exp2(log_beta + after)
        if diagonal:
            a = jnp.where(causal, a, 0.0)
        return _dot(a.astype(BF16), v_ref[0, pl.ds(pl.multiple_of(j * t, t), t), :])

    skip = -SB_SKIP * LOG2E
    near = []
    pending = jnp.full((t, 1), skip, F32)
    for a in range(n_sub):
        rows = slice(a * t, (a + 1) * t)
        q = q_ref[0, rows, :]
        jd = i * n_sub + a
        has_prev = jd > 0
        jp = jnp.maximum(jd - 1, 0)
        lb_d, within_d, carry = gates(q, jd, True)
        lb_p, within_p, total_p = gates(q, jp, False)
        acc = weigh(jd, lb_d, within_d, True) + jnp.where(has_prev, weigh(jp, lb_p, within_p + carry, False), 0.0)
        carry = carry + jnp.where(has_prev, total_p, 0.0)
        o_ref[0, rows, :] = acc.astype(o_ref.dtype)
        near.append((rows, q, jd, carry, acc))
        pending = jnp.maximum(pending, jnp.where(jd >= 2, carry, skip))

    def cond(s):
        j, carry, _ = s
        return jnp.logical_and(j >= 0, jnp.max(carry) > skip)

    @pl.when(jnp.max(pending) > skip)
    def _():
        for rows, q, jd, carry, acc in near:
            def body(s, q=q):
                j, carry, acc = s
                log_beta, within, total = gates(q, j, False)
                return j - 1, carry + total, acc + weigh(j, log_beta, within + carry, False)

            _, _, acc = lax.while_loop(cond, body, (jd - 2, carry, acc))
            o_ref[0, rows, :] = acc.astype(o_ref.dtype)


def _sb_attention(qkv, n_heads, *, t=256, n_sub=4):
    B, S, _ = qkv.shape
    d = HEAD_DIM
    tile = t * n_sub
    assert S % tile == 0
    return pl.pallas_call(
        functools.partial(_sb_kernel, t=t, n_sub=n_sub),
        grid=(B, n_heads, S // tile),
        in_specs=[pl.BlockSpec((1, tile, d), lambda b, h, i: (b, i, h)),
                  pl.BlockSpec((1, S, d), lambda b, h, i: (b, 0, n_heads + h)),
                  pl.BlockSpec((1, S, d), lambda b, h, i: (b, 0, 2 * n_heads + h))],
        out_specs=pl.BlockSpec((1, tile, d), lambda b, h, i: (b, i, h)),
        out_shape=jax.ShapeDtypeStruct((B, S, n_heads * d), BF16),
        scratch_shapes=[pltpu.VMEM((d, S), BF16)],
        compiler_params=_params("parallel", "parallel", "arbitrary"),
        name="sb_attn",
    )(qkv, qkv, qkv)


HG_CHUNK = 64
HG_SUB = 16
HG_GROUP = 2
HG_SPAN = 256
HG_SAFE = 80.0


def _hg_kernel(q_ref, fz_ref, v_ref, g_ref, lb_ref, gn_ref, tril_ref, o_ref, state_ref, oc_ref, b_ref, k_ref,
               *, n_chunks):
    C, U, d = HG_CHUNK, HG_SUB, HEAD_DIM

    @pl.when(pl.program_id(2) == 0)
    def _():
        state_ref[...] = jnp.zeros_like(state_ref)

    gn = gn_ref[...]
    row = lax.broadcasted_iota(jnp.int32, (C, C), 0)
    col = lax.broadcasted_iota(jnp.int32, (C, C), 1)
    lower = col <= row
    sub_row = lax.broadcasted_iota(jnp.int32, (U, 1), 0)
    tril = tril_ref[...]

    heads = []
    for hd in range(HG_GROUP):
        lanes = slice(hd * d, (hd + 1) * d)
        lb = lb_ref[:, lanes]
        fz = fz_ref[0, :, lanes]
        e = jnp.exp(-jnp.abs(fz))
        inv = 1.0 / (1.0 + e)
        sig_pos = jnp.where(fz >= 0, inv, e * inv)
        sig_neg = jnp.where(fz >= 0, e * inv, inv)
        log_f = jnp.log(lb + (1.0 - lb) * sig_pos)
        k = (1.0 - lb) * sig_neg
        l_hi = log_f.astype(BF16)
        l_lo = (log_f - l_hi.astype(F32)).astype(BF16)
        span = tril.shape[0]
        b = jnp.concatenate([_dot(tril, l_hi[r:r + span]) + _dot(tril, l_lo[r:r + span])
                             for r in range(0, fz.shape[0], span)], axis=0)
        heads.append((lanes, k, b))
    safe = functools.reduce(jnp.minimum, [jnp.min(b) for _, _, b in heads]) > -HG_SAFE

    def epilogue(o, g):
        o = o * lax.rsqrt(jnp.mean(o * o, axis=-1, keepdims=True) + RMS_EPS) * gn
        return (o * (g / (1.0 + jnp.exp(-g)))).astype(o_ref.dtype)

    @pl.when(safe)
    def _():
        for hd, (lanes, k, b) in enumerate(heads):
            q = q_ref[0, :, lanes]
            v16 = v_ref[0, :, lanes].astype(BF16)
            qe = (q * jnp.exp(b)).astype(BF16)
            ke = (k * jnp.exp(-b)).astype(BF16)
            state_t = state_ref[hd]
            for c in range(n_chunks):
                rows = slice(c * C, (c + 1) * C)
                s = jnp.where(lower, _dot_nt(qe[rows], ke[rows]), 0.0)
                oc_ref[hd, rows, :] = _dot(s.astype(BF16), v16[rows]) + _dot_nt(qe[rows], state_t.astype(BF16))
                b_last = b[(c + 1) * C - 1:(c + 1) * C, :]
                kd = (k[rows] * jnp.exp(b_last - b[rows])).astype(BF16)
                state_t = state_t * jnp.exp(b_last) + _dot_tn(v16[rows], kd)
            state_ref[hd] = state_t
            o_ref[0, :, lanes] = epilogue(oc_ref[hd], g_ref[0, :, lanes])

    def head_chunk(hd, r0):
        lanes = slice(hd * d, (hd + 1) * d)
        q = q_ref[0, pl.ds(r0, C), lanes]
        v = v_ref[0, pl.ds(r0, C), lanes]
        b = b_ref[hd, pl.ds(r0, C), :]
        k = k_ref[hd, pl.ds(r0, C), :]
        log2_k = jnp.log2(k)
        state_t = state_ref[hd]
        rows = pl.ds(r0, C)
        oc_ref[hd, rows, :] = _dot_nt((q * jnp.exp(b)).astype(BF16), state_t.astype(BF16))
        v16 = v.astype(BF16)
        for J in range(C // U - 1):
            lo_r, hi_r = J * U, (J + 1) * U
            ref_b = b[hi_r - 1:hi_r, :]
            kt = (k[lo_r:hi_r] * jnp.exp(ref_b - b[lo_r:hi_r])).astype(BF16)
            qs = (q[hi_r:] * jnp.exp(b[hi_r:] - ref_b)).astype(BF16)
            s = _dot_nt(qs, kt)
            oc_ref[hd, pl.ds(r0 + hi_r, C - hi_r), :] += _dot(s.astype(BF16), v16[lo_r:hi_r])
        b2 = b * LOG2E
        key2 = log2_k - b2
        for J in range(C // U):
            lo_r, hi_r = J * U, (J + 1) * U
            qj, bj, kj, vj = q[lo_r:hi_r], b2[lo_r:hi_r], key2[lo_r:hi_r], v[lo_r:hi_r]
            acc = jnp.zeros((U, d), F32)
            for s_i in range(U):
                w = qj * jnp.exp2(jnp.minimum(bj + kj[s_i:s_i + 1], 0.0))
                sc = jnp.sum(w, axis=1, keepdims=True)
                sc = jnp.where(sub_row >= s_i, sc, 0.0)
                acc = acc + sc * vj[s_i:s_i + 1]
            oc_ref[hd, pl.ds(r0 + lo_r, U), :] += acc
        b_last = b[C - 1:C, :]
        kd = (k * jnp.exp(b_last - b)).astype(BF16)
        state_ref[hd] = state_t * jnp.exp(b_last) + _dot_tn(v16, kd)
        o_ref[0, rows, lanes] = epilogue(oc_ref[hd, rows, :], g_ref[0, rows, lanes])

    @pl.when(jnp.logical_not(safe))
    def _():
        for hd, (_, k, b) in enumerate(heads):
            b_ref[hd] = b
            k_ref[hd] = k

        def chunk(c, _):
            r0 = pl.multiple_of(c * C, C)
            for hd in range(HG_GROUP):
                head_chunk(hd, r0)
            return 0

        lax.fori_loop(0, n_chunks, chunk, 0)


def _hgrn2(proj, lb, g_norm, n_heads, *, ts=512):
    B, S, _ = proj.shape
    d, G = HEAD_DIM, HG_GROUP
    assert S % ts == 0 and ts % HG_CHUNK == 0 and n_heads % G == 0
    n_groups = n_heads // G

    def col(part):
        return pl.BlockSpec((1, ts, G * d), lambda b, h, s: (b, s, part * n_groups + h))

    span = min(ts, HG_SPAN)
    r = jnp.arange(span)
    tril = ((r[None, :] <= r[:, None]) & (r[None, :] // HG_CHUNK == r[:, None] // HG_CHUNK)).astype(BF16)
    return pl.pallas_call(
        functools.partial(_hg_kernel, n_chunks=ts // HG_CHUNK),
        grid=(B, n_groups, S // ts),
        in_specs=[col(0), col(1), col(2), col(3),
                  pl.BlockSpec((1, G * d), lambda b, h, s: (0, h)),
                  pl.BlockSpec((1, d), lambda b, h, s: (0, 0)),
                  pl.BlockSpec((span, span), lambda b, h, s: (0, 0))],
        out_specs=pl.BlockSpec((1, ts, G * d), lambda b, h, s: (b, s, h)),
        out_shape=jax.ShapeDtypeStruct((B, S, n_heads * d), BF16),
        scratch_shapes=[pltpu.VMEM((G, d, d), F32), pltpu.VMEM((G, ts, d), F32),
                        pltpu.VMEM((G, ts, d), F32), pltpu.VMEM((G, ts, d), F32)],
        compiler_params=_params("parallel", "parallel", "arbitrary"),
        name="hgrn2",
    )(proj, proj, proj, proj, lb.reshape(1, n_heads * d), g_norm.reshape(1, d), tril)


def _rope_kernel(pos_ref, invf_ref, x_ref, kpe_ref, cos_ref, sin_ref):
    half = MLA_ROPE // 2
    ang = pos_ref[...].astype(F32) * invf_ref[...]
    lane = lax.broadcasted_iota(jnp.int32, ang.shape, 1)
    c = jnp.where(lane < MLA_ROPE, jnp.cos(ang), 0.0)
    s = jnp.sin(ang)
    s = jnp.where(lane < half, -s, jnp.where(lane < MLA_ROPE, s, 0.0))
    cos_ref[...] = c
    sin_ref[...] = s
    kpe_ref[...] = _rotate_half(x_ref[...], c, s).astype(kpe_ref.dtype)


def _rope_tables(positions, dkv, kpe_col, *, tm=512):
    M = dkv.shape[0]
    half = MLA_ROPE // 2
    freq = ROPE_THETA ** (-jnp.arange(0, MLA_ROPE, 2, dtype=F32) / MLA_ROPE)
    invf = jnp.concatenate([freq, freq, jnp.zeros((HEAD_DIM - 2 * half,), F32)]).reshape(1, HEAD_DIM)
    row = pl.BlockSpec((tm, HEAD_DIM), lambda i: (i, 0))
    return pl.pallas_call(
        _rope_kernel,
        grid=(M // tm,),
        in_specs=[pl.BlockSpec((tm, 1), lambda i: (i, 0)),
                  pl.BlockSpec((1, HEAD_DIM), lambda i: (0, 0)),
                  pl.BlockSpec((tm, HEAD_DIM), lambda i: (i, kpe_col))],
        out_specs=[row, row, row],
        out_shape=[jax.ShapeDtypeStruct((M, HEAD_DIM), BF16),
                   jax.ShapeDtypeStruct((M, HEAD_DIM), F32),
                   jax.ShapeDtypeStruct((M, HEAD_DIM), F32)],
        compiler_params=_params("parallel"),
        name="mla_rope",
    )(positions.reshape(M, 1), invf, dkv)


MLA_ROW_CHUNK = 256


def _mla_kernel(q_ref, kn_ref, v_ref, kpe_ref, o_ref, kcat_ref, v1_ref, sa_ref, sb_ref, m_ref, acc_ref, *, t):
    n_q = q_ref.shape[1] // t
    n_pairs = n_q * (n_q + 1) // 2
    for n in range(n_q):
        rows = slice(n * t, (n + 1) * t)
        kcat_ref[:MLA_NOPE, rows] = kn_ref[0, rows, :].T
        kcat_ref[MLA_NOPE:, rows] = kpe_ref[0, rows, :].T
    rc = MLA_ROW_CHUNK
    row = lax.broadcasted_iota(jnp.int32, (rc, t), 0) // MLA_CHUNK
    col = lax.broadcasted_iota(jnp.int32, (rc, t), 1) // MLA_CHUNK

    def tile(n):
        return pl.ds(pl.multiple_of(n * t, t), t)

    def scores(i, j, dst_ref):
        dst_ref[...] = _dot(q_ref[0, tile(i), :], kcat_ref[:, tile(j)])

    v1_ref[:, :HEAD_DIM] = v_ref[0]
    v1_ref[:, HEAD_DIM:] = jnp.ones((v1_ref.shape[0], HEAD_DIM), BF16)

    def reset():
        m_ref[...] = jnp.full_like(m_ref, -jnp.inf)
        acc_ref[...] = jnp.zeros_like(acc_ref)

    def absorb(j, src_ref, diagonal):
        vb = v1_ref[tile(j), :]
        for r in range(t // rc):
            rows = slice(r * rc, (r + 1) * rc)
            s = src_ref[rows, :]
            if diagonal:
                s = jnp.where(col <= row + (r * rc) // MLA_CHUNK, s, -jnp.inf)
            m_old = m_ref[rows, :]
            lane_groups = [s[:, c:c + HEAD_DIM] for c in range(0, t, HEAD_DIM)]
            m_col = jnp.max(functools.reduce(jnp.maximum, lane_groups + [m_old]), axis=1, keepdims=True)
            m_new = jnp.broadcast_to(m_col, m_old.shape)
            alpha = jnp.exp2(m_old - m_new)
            p = jnp.exp2((s - m_col).astype(BF16))
            acc_ref[rows, :] = jnp.concatenate([alpha, alpha], axis=1) * acc_ref[rows, :] + _dot(p, vb)
            m_ref[rows, :] = m_new

    def successor(i, j):
        last = j == i
        return jnp.where(last, i + 1, i), jnp.where(last, 0, j + 1)

    def step(cur_ref, nxt_ref, i, j):
        ni, nj = successor(i, j)
        ni = jnp.minimum(ni, n_q - 1)

        @pl.when(j < i)
        def _():
            scores(ni, nj, nxt_ref)
            absorb(j, cur_ref, False)

        @pl.when(j == i)
        def _():
            scores(ni, nj, nxt_ref)
            absorb(j, cur_ref, True)
            o_ref[0, tile(i), :] = (acc_ref[:, :HEAD_DIM] / acc_ref[:, HEAD_DIM:]).astype(o_ref.dtype)
            reset()

    def two_steps(_, ij):
        i, j = ij
        i1, j1 = successor(i, j)

        @pl.when(j + 1 < i)
        def _():
            scores(i, j + 1, sb_ref)
            absorb(j, sa_ref, False)
            scores(i, j + 2, sa_ref)
            absorb(j + 1, sb_ref, False)

        @pl.when(j + 1 >= i)
        def _():
            step(sa_ref, sb_ref, i, j)
            step(sb_ref, sa_ref, i1, j1)

        return successor(i1, j1)

    reset()
    scores(0, 0, sa_ref)
    ij = lax.fori_loop(0, n_pairs // 2, two_steps, (jnp.int32(0), jnp.int32(0)))
    if n_pairs % 2:
        step(sa_ref, sb_ref, *ij)


def _mla_attention(q, kv, kpe, n_heads, *, t=512):
    B, S, _ = q.shape
    d = HEAD_DIM
    assert S % t == 0 and t % MLA_CHUNK == 0
    return pl.pallas_call(
        functools.partial(_mla_kernel, t=t),
        grid=(B, n_heads),
        in_specs=[pl.BlockSpec((1, S, 2 * d), lambda b, h: (b, 0, h)),
                  pl.BlockSpec((1, S, d), lambda b, h: (b, 0, 2 * h)),
                  pl.BlockSpec((1, S, d), lambda b, h: (b, 0, 2 * h + 1)),
                  pl.BlockSpec((1, S, d), lambda b, h: (b, 0, 0))],
        out_specs=pl.BlockSpec((1, S, d), lambda b, h: (b, 0, h)),
        out_shape=jax.ShapeDtypeStruct((B, S, n_heads * d), BF16),
        scratch_shapes=[pltpu.VMEM((2 * d, S), BF16), pltpu.VMEM((S, 2 * d), BF16),
                        pltpu.VMEM((t, t), F32), pltpu.VMEM((t, t), F32),
                        pltpu.VMEM((t, d), F32), pltpu.VMEM((t, 2 * d), F32)],
        compiler_params=_params("parallel", "parallel"),
        name="mla_attn",
    )(q, kv, kv, kpe)


def _sb_layer(h, gain, w_qkv, w_o, B, S):
    M, D = h.shape
    H = w_o.shape[0] // HEAD_DIM
    col_scale = jnp.where(jnp.arange(w_qkv.shape[1]) < H * HEAD_DIM, HEAD_DIM ** -0.5 * LOG2E, 1.0)
    qkv = _linear(h, (w_qkv * col_scale).astype(BF16), name="sb_qkv", gain=gain, out_dtype=BF16)
    o = _sb_attention(qkv.reshape(B, S, -1), H)
    return _linear(o.reshape(M, -1), w_o.astype(BF16), name="sb_out", residual=h)


def _hg_layer(h, gain, w_in, g_norm, w_o, lb, B, S):
    M, D = h.shape
    H = w_o.shape[0] // HEAD_DIM
    proj = _linear(h, w_in.astype(BF16), name="hg_in", gain=gain, out_dtype=F32)
    o = _hgrn2(proj.reshape(B, S, -1), lb, g_norm, H)
    return _linear(o.reshape(M, -1), w_o.astype(BF16), name="hg_out", residual=h)


def _mla_layer(h, positions, gain, w_dkv, q_norm, kv_norm, w_uq, w_ukv, w_o, B, S):
    M, D = h.shape
    H = w_o.shape[0] // MLA_NOPE
    qr, kvr = MLA_Q_RANK, MLA_KV_RANK
    pad_a = jnp.zeros((D, qr - kvr), F32)
    pad_b = jnp.zeros((D, HEAD_DIM - MLA_ROPE), F32)
    w_d = jnp.concatenate([w_dkv[:, qr:qr + kvr], pad_a, w_dkv[:, :qr], w_dkv[:, qr + kvr:], pad_b], axis=1)
    dkv = _linear(h, w_d.astype(BF16), name="mla_down", gain=gain, out_dtype=F32)
    kpe, cos, sin = _rope_tables(positions, dkv, (2 * qr) // HEAD_DIM)
    w_q = w_uq.reshape(qr, H, MLA_NOPE + MLA_ROPE) * ((MLA_NOPE + MLA_ROPE) ** -0.5 * LOG2E)
    w_q = jnp.concatenate([w_q, jnp.zeros((qr, H, HEAD_DIM - MLA_ROPE), F32)], axis=2).reshape(qr, H * 2 * HEAD_DIM)
    q = _linear(dkv, w_q.astype(BF16), name="mla_q", gain=q_norm, rope=(cos, sin), out_dtype=BF16, x_col=1)
    kv = _linear(dkv, w_ukv.astype(BF16), name="mla_kv", gain=kv_norm, out_dtype=BF16, x_col=0)
    o = _mla_attention(q.reshape(B, S, -1), kv.reshape(B, S, -1), kpe.reshape(B, S, -1), H)
    return _linear(o.reshape(M, -1), w_o.astype(BF16), name="mla_out", residual=h)


def kernel(x, positions, norm_mix, norm_mlp, final_norm, sb_w_qkv, sb_w_o, hg_w_in, hg_lb_logits, hg_g_norm, hg_w_o, mla_w_dkv, mla_q_norm, mla_kv_norm, mla_w_uq, mla_w_ukv, mla_w_o, mlp_w1, mlp_w2):
    B, S, D = x.shape
    depth = norm_mix.shape[0]
    assert depth >= 1
    p_lb = jax.nn.softmax(hg_lb_logits.astype(F32), axis=0)
    lb_all = jnp.cumsum(p_lb, axis=0) - p_lb[0]
    h = x.reshape(B * S, D)
    w1, w2 = mlp_w1.astype(BF16), mlp_w2.astype(BF16)
    for i in range(depth):
        m, j = i % N_MIXERS, i // N_MIXERS
        if m == 0:
            h = _sb_layer(h, norm_mix[i], sb_w_qkv[j], sb_w_o[j], B, S)
        elif m == 1:
            h = _hg_layer(h, norm_mix[i], hg_w_in[j], hg_g_norm[j], hg_w_o[j], lb_all[i], B, S)
        else:
            h = _mla_layer(h, positions, norm_mix[i], mla_w_dkv[j], mla_q_norm[j], mla_kv_norm[j],
                           mla_w_uq[j], mla_w_ukv[j], mla_w_o[j], B, S)
        h = _mlp(h, norm_mlp[i], w1, w2, i, out_gain=final_norm if i == depth - 1 else None)
    return h.reshape(B, S, D)
```

```python
import functools

import jax
import jax.numpy as jnp
from jax import lax
from jax.experimental import pallas as pl
from jax.experimental.pallas import tpu as pltpu

F32 = jnp.float32
BF16 = jnp.bfloat16

RMS_EPS = 1e-6
N_MIXERS = 3
HEAD_DIM = 128
MLA_NOPE = 128
MLA_ROPE = 64
MLA_Q_RANK = 768
MLA_KV_RANK = 512
MLA_CHUNK = 64
ROPE_THETA = 10000.0
LOG2E = 1.4426950408889634
VMEM_LIMIT = 56 * 1024 * 1024


def _params(*sem):
    return pltpu.CompilerParams(dimension_semantics=sem, vmem_limit_bytes=VMEM_LIMIT)


def _dot(a, b):
    return jnp.dot(a, b, preferred_element_type=F32)


def _dot_nt(a, b):
    return lax.dot_general(a, b, (((1,), (1,)), ((), ())), preferred_element_type=F32)


def _dot_tn(a, b):
    return lax.dot_general(a, b, (((0,), (0,)), ((), ())), preferred_element_type=F32)


def _split3(x):
    hi = x.astype(BF16)
    r1 = x - hi.astype(F32)
    mid = r1.astype(BF16)
    lo = (r1 - mid.astype(F32)).astype(BF16)
    return hi, mid, lo


def _rotate_half(x, cos, sin):
    half = MLA_ROPE // 2
    swapped = pltpu.roll(x, half, 1) + pltpu.roll(x, HEAD_DIM - half, 1)
    return x * cos + swapped * sin


def _linear_kernel(*refs, has_gain, has_res, has_rope):
    x_ref, w_ref = refs[0], refs[1]
    pos = 2
    g_ref = r_ref = cos_ref = sin_ref = None
    if has_gain:
        g_ref = refs[pos]
        pos += 1
    if has_res:
        r_ref = refs[pos]
        pos += 1
    if has_rope:
        cos_ref, sin_ref = refs[pos], refs[pos + 1]
        pos += 2
    o_ref = refs[pos]

    if has_gain:
        xn_ref = refs[pos + 1]

        @pl.when(pl.program_id(1) == 0)
        def _():
            x = x_ref[...].astype(F32)
            ms = jnp.mean(x * x, axis=-1, keepdims=True)
            xn_ref[...] = (x * lax.rsqrt(ms + RMS_EPS) * g_ref[...]).astype(BF16)

        lhs = xn_ref[...]
    else:
        lhs = x_ref[...]

    acc = _dot(lhs, w_ref[...])
    if has_res:
        acc = acc + r_ref[...]
    if has_rope:
        cos, sin = cos_ref[...], sin_ref[...]
        for c in range(0, acc.shape[1], 2 * HEAD_DIM):
            o_ref[:, c:c + HEAD_DIM] = acc[:, c:c + HEAD_DIM].astype(o_ref.dtype)
            rot = _rotate_half(acc[:, c + HEAD_DIM:c + 2 * HEAD_DIM], cos, sin)
            o_ref[:, c + HEAD_DIM:c + 2 * HEAD_DIM] = rot.astype(o_ref.dtype)
    else:
        o_ref[...] = acc.astype(o_ref.dtype)


LINEAR_VMEM_BUDGET = 44 * 1024 * 1024
RESIDENT_WEIGHT_BYTES = 8 * 1024 * 1024


def _linear_tiles(M, K, N, x_bytes, out_bytes, has_gain, has_res, has_rope):
    tn = N if K * N * 2 <= RESIDENT_WEIGHT_BYTES or N % 1024 else 1024
    for tm in (1024, 512, 256, 128):
        if M % tm:
            continue
        need = 2 * tm * K * x_bytes + 2 * K * tn * 2 + 2 * tm * tn * out_bytes
        need += tm * K * 2 if has_gain else 0
        need += 2 * tm * tn * 4 if has_res else 0
        need += tm * tn * 4 + 4 * tm * HEAD_DIM * 4 if has_rope else 0
        if need <= LINEAR_VMEM_BUDGET:
            return tm, tn
    raise ValueError("no linear tiling fits VMEM")


def _linear(x, w, *, name, gain=None, residual=None, rope=None, out_dtype=F32, x_col=0):
    M = x.shape[0]
    K, N = w.shape
    assert gain is not None or x.dtype == BF16
    tm, tn = _linear_tiles(M, K, N, x.dtype.itemsize, jnp.dtype(out_dtype).itemsize,
                           gain is not None, residual is not None, rope is not None)
    assert rope is None or tn % (2 * HEAD_DIM) == 0
    in_specs = [pl.BlockSpec((tm, K), lambda i, j: (i, x_col)),
                pl.BlockSpec((K, tn), lambda i, j: (0, j))]
    args = [x, w]
    if gain is not None:
        in_specs.append(pl.BlockSpec((1, K), lambda i, j: (0, 0)))
        args.append(gain.reshape(1, K).astype(F32))
    if residual is not None:
        in_specs.append(pl.BlockSpec((tm, tn), lambda i, j: (i, j)))
        args.append(residual)
    if rope is not None:
        in_specs += [pl.BlockSpec((tm, HEAD_DIM), lambda i, j: (i, 0))] * 2
        args += list(rope)
    return pl.pallas_call(
        functools.partial(_linear_kernel, has_gain=gain is not None, has_res=residual is not None,
                          has_rope=rope is not None),
        grid=(M // tm, N // tn),
        in_specs=in_specs,
        out_specs=pl.BlockSpec((tm, tn), lambda i, j: (i, j)),
        out_shape=jax.ShapeDtypeStruct((M, N), out_dtype),
        scratch_shapes=[pltpu.VMEM((tm, K), BF16)] if gain is not None else [],
        compiler_params=_params("parallel", "arbitrary"),
        name=name,
    )(*args)


def _mlp_kernel(*refs, has_out_norm):
    if has_out_norm:
        x_ref, g_ref, w1_ref, w2_ref, go_ref, o_ref, xn_ref = refs
    else:
        x_ref, g_ref, w1_ref, w2_ref, o_ref, xn_ref = refs
    f = pl.program_id(1)

    @pl.when(f == 0)
    def _():
        x = x_ref[...]
        ms = jnp.mean(x * x, axis=-1, keepdims=True)
        xn_ref[...] = (x * lax.rsqrt(ms + RMS_EPS) * g_ref[...]).astype(BF16)
        o_ref[...] = x

    u = _dot(xn_ref[...], w1_ref[...])
    r = jnp.square(jnp.maximum(u, 0.0)).astype(BF16)
    o_ref[...] += _dot(r, w2_ref[...])

    if has_out_norm:
        @pl.when(f == pl.num_programs(1) - 1)
        def _():
            y = o_ref[...]
            ms = jnp.mean(y * y, axis=-1, keepdims=True)
            o_ref[...] = y * lax.rsqrt(ms + RMS_EPS) * go_ref[...]


def _mlp(h, gain, w1, w2, layer, *, out_gain=None, tm=512, tf=1024):
    M, D = h.shape
    F = w1.shape[2]
    assert M % tm == 0 and F % tf == 0
    vec = pl.BlockSpec((1, D), lambda i, f: (0, 0))
    in_specs = [pl.BlockSpec((tm, D), lambda i, f: (i, 0)), vec,
                pl.BlockSpec((None, D, tf), lambda i, f: (layer, 0, f)),
                pl.BlockSpec((None, tf, D), lambda i, f: (layer, f, 0))]
    args = [h, gain.reshape(1, D).astype(F32), w1, w2]
    if out_gain is not None:
        in_specs.append(vec)
        args.append(out_gain.reshape(1, D).astype(F32))
    return pl.pallas_call(
        functools.partial(_mlp_kernel, has_out_norm=out_gain is not None),
        grid=(M // tm, F // tf),
        in_specs=in_specs,
        out_specs=pl.BlockSpec((tm, D), lambda i, f: (i, 0)),
        out_shape=jax.ShapeDtypeStruct((M, D), F32),
        scratch_shapes=[pltpu.VMEM((tm, D), BF16)],
        compiler_params=_params("parallel", "arbitrary"),
        name="mlp",
    )(*args)


SB_SKIP = 110.0


def _sb_kernel(q_ref, k_ref, v_ref, o_ref, *, t, n_sub):
    i = pl.program_id(2)
    row = lax.broadcasted_iota(jnp.int32, (t, t), 0)
    col = lax.broadcasted_iota(jnp.int32, (t, t), 1)
    later = (row > col).astype(BF16)
    causal = col < row

    def gates(q, j, diagonal):
        z = _dot_nt(q, k_ref[0, pl.ds(pl.multiple_of(j * t, t), t), :])
        nz = -z
        lom = jnp.minimum(nz, 0.0) - jnp.log2(1.0 + jnp.exp2(jnp.minimum(z, nz)))
        log_beta = lom + z
        if diagonal:
            lom = jnp.where(causal, lom, 0.0)
        within = _dot(lom.astype(BF16), later)
        return log_beta, within, jnp.sum(lom, axis=1, keepdims=True)

    def weigh(j, log_beta, after, diagonal):
        a = jnp.exp2(log_beta + after)
        if diagonal:
            a = jnp.where(causal, a, 0.0)
        return _dot(a.astype(BF16), v_ref[0, pl.ds(pl.multiple_of(j * t, t), t), :])

    skip = -SB_SKIP * LOG2E
    near = []
    pending = jnp.full((t, 1), skip, F32)
    for a in range(n_sub):
        rows = slice(a * t, (a + 1) * t)
        q = q_ref[0, rows, :]
        jd = i * n_sub + a
        has_prev = jd > 0
        jp = jnp.maximum(jd - 1, 0)
        lb_d, within_d, carry = gates(q, jd, True)
        lb_p, within_p, total_p = gates(q, jp, False)
        acc = weigh(jd, lb_d, within_d, True) + jnp.where(has_prev, weigh(jp, lb_p, within_p + carry, False), 0.0)
        carry = carry + jnp.where(has_prev, total_p, 0.0)
        o_ref[0, rows, :] = acc.astype(o_ref.dtype)
        near.append((rows, q, jd, carry, acc))
        pending = jnp.maximum(pending, jnp.where(jd >= 2, carry, skip))

    def cond(s):
        j, carry, _ = s
        return jnp.logical_and(j >= 0, jnp.max(carry) > skip)

    @pl.when(jnp.max(pending) > skip)
    def _():
        for rows, q, jd, carry, acc in near:
            def body(s, q=q):
                j, carry, acc = s
                log_beta, within, total = gates(q, j, False)
                return j - 1, carry + total, acc + weigh(j, log_beta, within + carry, False)

            _, _, acc = lax.while_loop(cond, body, (jd - 2, carry, acc))
            o_ref[0, rows, :] = acc.astype(o_ref.dtype)


def _sb_attention(qkv, n_heads, *, t=256, n_sub=4):
    B, S, _ = qkv.shape
    d = HEAD_DIM
    tile = t * n_sub
    assert S % tile == 0
    return pl.pallas_call(
        functools.partial(_sb_kernel, t=t, n_sub=n_sub),
        grid=(B, n_heads, S // tile),
        in_specs=[pl.BlockSpec((1, tile, d), lambda b, h, i: (b, i, h)),
                  pl.BlockSpec((1, S, d), lambda b, h, i: (b, 0, n_heads + h)),
                  pl.BlockSpec((1, S, d), lambda b, h, i: (b, 0, 2 * n_heads + h))],
        out_specs=pl.BlockSpec((1, tile, d), lambda b, h, i: (b, i, h)),
        out_shape=jax.ShapeDtypeStruct((B, S, n_heads * d), BF16),
        compiler_params=_params("parallel", "parallel", "arbitrary"),
        name="sb_attn",
    )(qkv, qkv, qkv)


HG_CHUNK = 64
HG_SUB = 16
HG_GROUP = 4
HG_SPAN = 256
HG_SAFE = 80.0


def _hg_kernel(q_ref, fz_ref, v_ref, g_ref, lb_ref, gn_ref, tril_ref, o_ref, state_ref, oc_ref, b_ref, k_ref,
               *, n_chunks):
    C, U, d = HG_CHUNK, HG_SUB, HEAD_DIM

    @pl.when(pl.program_id(2) == 0)
    def _():
        state_ref[...] = jnp.zeros_like(state_ref)

    gn = gn_ref[...]
    row = lax.broadcasted_iota(jnp.int32, (C, C), 0)
    col = lax.broadcasted_iota(jnp.int32, (C, C), 1)
    lower = col <= row
    sub_row = lax.broadcasted_iota(jnp.int32, (U, 1), 0)
    tril = tril_ref[...]

    heads = []
    for hd in range(HG_GROUP):
        lanes = slice(hd * d, (hd + 1) * d)
        lb = lb_ref[:, lanes]
        fz = fz_ref[0, :, lanes]
        e = jnp.exp(-jnp.abs(fz))
        inv = 1.0 / (1.0 + e)
        sig_pos = jnp.where(fz >= 0, inv, e * inv)
        sig_neg = jnp.where(fz >= 0, e * inv, inv)
        log_f = jnp.log(lb + (1.0 - lb) * sig_pos)
        k = (1.0 - lb) * sig_neg
        l_hi = log_f.astype(BF16)
        l_lo = (log_f - l_hi.astype(F32)).astype(BF16)
        span = tril.shape[0]
        b = jnp.concatenate([_dot(tril, l_hi[r:r + span]) + _dot(tril, l_lo[r:r + span])
                             for r in range(0, fz.shape[0], span)], axis=0)
        heads.append((lanes, k, b))
    safe = functools.reduce(jnp.minimum, [jnp.min(b) for _, _, b in heads]) > -HG_SAFE

    def epilogue(o, g):
        o = o * lax.rsqrt(jnp.mean(o * o, axis=-1, keepdims=True) + RMS_EPS) * gn
        return (o * (g / (1.0 + jnp.exp(-g)))).astype(o_ref.dtype)

    @pl.when(safe)
    def _():
        for hd, (lanes, k, b) in enumerate(heads):
            q = q_ref[0, :, lanes]
            v16 = v_ref[0, :, lanes].astype(BF16)
            qe = (q * jnp.exp(b)).astype(BF16)
            ke = (k * jnp.exp(-b)).astype(BF16)
            state_t = state_ref[hd]
            for c in range(n_chunks):
                rows = slice(c * C, (c + 1) * C)
                s = jnp.where(lower, _dot_nt(qe[rows], ke[rows]), 0.0)
                oc_ref[hd, rows, :] = _dot(s.astype(BF16), v16[rows]) + _dot_nt(qe[rows], state_t.astype(BF16))
                b_last = b[(c + 1) * C - 1:(c + 1) * C, :]
                kd = (k[rows] * jnp.exp(b_last - b[rows])).astype(BF16)
                state_t = state_t * jnp.exp(b_last) + _dot_tn(v16[rows], kd)
            state_ref[hd] = state_t
            o_ref[0, :, lanes] = epilogue(oc_ref[hd], g_ref[0, :, lanes])

    def head_chunk(hd, r0):
        lanes = slice(hd * d, (hd + 1) * d)
        q = q_ref[0, pl.ds(r0, C), lanes]
        v = v_ref[0, pl.ds(r0, C), lanes]
        b = b_ref[hd, pl.ds(r0, C), :]
        k = k_ref[hd, pl.ds(r0, C), :]
        log2_k = jnp.log2(k)
        state_t = state_ref[hd]
        rows = pl.ds(r0, C)
        oc_ref[hd, rows, :] = _dot_nt((q * jnp.exp(b)).astype(BF16), state_t.astype(BF16))
        v16 = v.astype(BF16)
        for J in range(C // U - 1):
            lo_r, hi_r = J * U, (J + 1) * U
            ref_b = b[hi_r - 1:hi_r, :]
            kt = (k[lo_r:hi_r] * jnp.exp(ref_b - b[lo_r:hi_r])).astype(BF16)
            qs = (q[hi_r:] * jnp.exp(b[hi_r:] - ref_b)).astype(BF16)
            s = _dot_nt(qs, kt)
            oc_ref[hd, pl.ds(r0 + hi_r, C - hi_r), :] += _dot(s.astype(BF16), v16[lo_r:hi_r])
        b2 = b * LOG2E
        key2 = log2_k - b2
        for J in range(C // U):
            lo_r, hi_r = J * U, (J + 1) * U
            qj, bj, kj, vj = q[lo_r:hi_r], b2[lo_r:hi_r], key2[lo_r:hi_r], v[lo_r:hi_r]
            acc = jnp.zeros((U, d), F32)
            for s_i in range(U):
                w = qj * jnp.exp2(jnp.minimum(bj + kj[s_i:s_i + 1], 0.0))
                sc = jnp.sum(w, axis=1, keepdims=True)
                sc = jnp.where(sub_row >= s_i, sc, 0.0)
                acc = acc + sc * vj[s_i:s_i + 1]
            oc_ref[hd, pl.ds(r0 + lo_r, U), :] += acc
        b_last = b[C - 1:C, :]
        kd = (k * jnp.exp(b_last - b)).astype(BF16)
        state_ref[hd] = state_t * jnp.exp(b_last) + _dot_tn(v16, kd)
        o_ref[0, rows, lanes] = epilogue(oc_ref[hd, rows, :], g_ref[0, rows, lanes])

    @pl.when(jnp.logical_not(safe))
    def _():
        for hd, (_, k, b) in enumerate(heads):
            b_ref[hd] = b
            k_ref[hd] = k

        def chunk(c, _):
            r0 = pl.multiple_of(c * C, C)
            for hd in range(HG_GROUP):
                head_chunk(hd, r0)
            return 0

        lax.fori_loop(0, n_chunks, chunk, 0)


def _hgrn2(proj, lb, g_norm, n_heads, *, ts=512):
    B, S, _ = proj.shape
    d, G = HEAD_DIM, HG_GROUP
    assert S % ts == 0 and ts % HG_CHUNK == 0 and n_heads % G == 0
    n_groups = n_heads // G

    def col(part):
        return pl.BlockSpec((1, ts, G * d), lambda b, h, s: (b, s, part * n_groups + h))

    span = min(ts, HG_SPAN)
    r = jnp.arange(span)
    tril = ((r[None, :] <= r[:, None]) & (r[None, :] // HG_CHUNK == r[:, None] // HG_CHUNK)).astype(BF16)
    return pl.pallas_call(
        functools.partial(_hg_kernel, n_chunks=ts // HG_CHUNK),
        grid=(B, n_groups, S // ts),
        in_specs=[col(0), col(1), col(2), col(3),
                  pl.BlockSpec((1, G * d), lambda b, h, s: (0, h)),
                  pl.BlockSpec((1, d), lambda b, h, s: (0, 0)),
                  pl.BlockSpec((span, span), lambda b, h, s: (0, 0))],
        out_specs=pl.BlockSpec((1, ts, G * d), lambda b, h, s: (b, s, h)),
        out_shape=jax.ShapeDtypeStruct((B, S, n_heads * d), BF16),
        scratch_shapes=[pltpu.VMEM((G, d, d), F32), pltpu.VMEM((G, ts, d), F32),
                        pltpu.VMEM((G, ts, d), F32), pltpu.VMEM((G, ts, d), F32)],
        compiler_params=_params("parallel", "parallel", "arbitrary"),
        name="hgrn2",
    )(proj, proj, proj, proj, lb.reshape(1, n_heads * d), g_norm.reshape(1, d), tril)


def _rope_kernel(pos_ref, invf_ref, x_ref, kpe_ref, cos_ref, sin_ref):
    half = MLA_ROPE // 2
    ang = pos_ref[...].astype(F32) * invf_ref[...]
    lane = lax.broadcasted_iota(jnp.int32, ang.shape, 1)
    c = jnp.where(lane < MLA_ROPE, jnp.cos(ang), 0.0)
    s = jnp.sin(ang)
    s = jnp.where(lane < half, -s, jnp.where(lane < MLA_ROPE, s, 0.0))
    cos_ref[...] = c
    sin_ref[...] = s
    kpe_ref[...] = _rotate_half(x_ref[...], c, s).astype(kpe_ref.dtype)


def _rope_tables(positions, dkv, kpe_col, *, tm=512):
    M = dkv.shape[0]
    half = MLA_ROPE // 2
    freq = ROPE_THETA ** (-jnp.arange(0, MLA_ROPE, 2, dtype=F32) / MLA_ROPE)
    invf = jnp.concatenate([freq, freq, jnp.zeros((HEAD_DIM - 2 * half,), F32)]).reshape(1, HEAD_DIM)
    row = pl.BlockSpec((tm, HEAD_DIM), lambda i: (i, 0))
    return pl.pallas_call(
        _rope_kernel,
        grid=(M // tm,),
        in_specs=[pl.BlockSpec((tm, 1), lambda i: (i, 0)),
                  pl.BlockSpec((1, HEAD_DIM), lambda i: (0, 0)),
                  pl.BlockSpec((tm, HEAD_DIM), lambda i: (i, kpe_col))],
        out_specs=[row, row, row],
        out_shape=[jax.ShapeDtypeStruct((M, HEAD_DIM), BF16),
                   jax.ShapeDtypeStruct((M, HEAD_DIM), F32),
                   jax.ShapeDtypeStruct((M, HEAD_DIM), F32)],
        compiler_params=_params("parallel"),
        name="mla_rope",
    )(positions.reshape(M, 1), invf, dkv)


MLA_ROW_CHUNK = 256


def _mla_kernel(q_ref, kn_ref, v_ref, kpe_ref, o_ref, kcat_ref, v1_ref, sa_ref, sb_ref, m_ref, acc_ref, *, t):
    n_q = q_ref.shape[1] // t
    n_pairs = n_q * (n_q + 1) // 2
    for n in range(n_q):
        rows = slice(n * t, (n + 1) * t)
        kcat_ref[:MLA_NOPE, rows] = kn_ref[0, rows, :].T
        kcat_ref[MLA_NOPE:, rows] = kpe_ref[0, rows, :].T
    rc = MLA_ROW_CHUNK
    row = lax.broadcasted_iota(jnp.int32, (rc, t), 0) // MLA_CHUNK
    col = lax.broadcasted_iota(jnp.int32, (rc, t), 1) // MLA_CHUNK

    def tile(n):
        return pl.ds(pl.multiple_of(n * t, t), t)

    def scores(i, j, dst_ref):
        dst_ref[...] = _dot(q_ref[0, tile(i), :], kcat_ref[:, tile(j)])

    v1_ref[:, :HEAD_DIM] = v_ref[0]
    v1_ref[:, HEAD_DIM:] = jnp.ones((v1_ref.shape[0], HEAD_DIM), BF16)

    def reset():
        m_ref[...] = jnp.full_like(m_ref, -jnp.inf)
        acc_ref[...] = jnp.zeros_like(acc_ref)

    def absorb(j, src_ref, diagonal):
        vb = v1_ref[tile(j), :]
        for r in range(t // rc):
            rows = slice(r * rc, (r + 1) * rc)
            s = src_ref[rows, :]
            if diagonal:
                s = jnp.where(col <= row + (r * rc) // MLA_CHUNK, s, -jnp.inf)
            m_old = m_ref[rows, :]
            lane_groups = [s[:, c:c + HEAD_DIM] for c in range(0, t, HEAD_DIM)]
            m_col = jnp.max(functools.reduce(jnp.maximum, lane_groups + [m_old]), axis=1, keepdims=True)
            m_new = jnp.broadcast_to(m_col, m_old.shape)
            alpha = jnp.exp2(m_old - m_new)
            p = jnp.exp2((s - m_col).astype(BF16))
            acc_ref[rows, :] = jnp.concatenate([alpha, alpha], axis=1) * acc_ref[rows, :] + _dot(p, vb)
            m_ref[rows, :] = m_new

    def successor(i, j):
        last = j == i
        return jnp.where(last, i + 1, i), jnp.where(last, 0, j + 1)

    def step(cur_ref, nxt_ref, i, j):
        ni, nj = successor(i, j)
        ni = jnp.minimum(ni, n_q - 1)

        @pl.when(j < i)
        def _():
            scores(ni, nj, nxt_ref)
            absorb(j, cur_ref, False)

        @pl.when(j == i)
        def _():
            scores(ni, nj, nxt_ref)
            absorb(j, cur_ref, True)
            o_ref[0, tile(i), :] = (acc_ref[:, :HEAD_DIM] / acc_ref[:, HEAD_DIM:]).astype(o_ref.dtype)
            reset()

    def two_steps(_, ij):
        i, j = ij
        i1, j1 = successor(i, j)

        @pl.when(j + 1 < i)
        def _():
            scores(i, j + 1, sb_ref)
            absorb(j, sa_ref, False)
            scores(i, j + 2, sa_ref)
            absorb(j + 1, sb_ref, False)

        @pl.when(j + 1 >= i)
        def _():
            step(sa_ref, sb_ref, i, j)
            step(sb_ref, sa_ref, i1, j1)

        return successor(i1, j1)

    reset()
    scores(0, 0, sa_ref)
    ij = lax.fori_loop(0, n_pairs // 2, two_steps, (jnp.int32(0), jnp.int32(0)))
    if n_pairs % 2:
        step(sa_ref, sb_ref, *ij)


def _mla_attention(q, kv, kpe, n_heads, *, t=512):
    B, S, _ = q.shape
    d = HEAD_DIM
    assert S % t == 0 and t % MLA_CHUNK == 0
    return pl.pallas_call(
        functools.partial(_mla_kernel, t=t),
        grid=(B, n_heads),
        in_specs=[pl.BlockSpec((1, S, 2 * d), lambda b, h: (b, 0, h)),
                  pl.BlockSpec((1, S, d), lambda b, h: (b, 0, 2 * h)),
                  pl.BlockSpec((1, S, d), lambda b, h: (b, 0, 2 * h + 1)),
                  pl.BlockSpec((1, S, d), lambda b, h: (b, 0, 0))],
        out_specs=pl.BlockSpec((1, S, d), lambda b, h: (b, 0, h)),
        out_shape=jax.ShapeDtypeStruct((B, S, n_heads * d), BF16),
        scratch_shapes=[pltpu.VMEM((2 * d, S), BF16), pltpu.VMEM((S, 2 * d), BF16),
                        pltpu.VMEM((t, t), F32), pltpu.VMEM((t, t), F32),
                        pltpu.VMEM((t, d), F32), pltpu.VMEM((t, 2 * d), F32)],
        compiler_params=_params("parallel", "parallel"),
        name="mla_attn",
    )(q, kv, kv, kpe)


def _sb_layer(h, gain, w_qkv, w_o, B, S):
    M, D = h.shape
    H = w_o.shape[0] // HEAD_DIM
    col_scale = jnp.where(jnp.arange(w_qkv.shape[1]) < H * HEAD_DIM, HEAD_DIM ** -0.5 * LOG2E, 1.0)
    qkv = _linear(h, (w_qkv * col_scale).astype(BF16), name="sb_qkv", gain=gain, out_dtype=BF16)
    o = _sb_attention(qkv.reshape(B, S, -1), H)
    return _linear(o.reshape(M, -1), w_o.astype(BF16), name="sb_out", residual=h)


def _hg_layer(h, gain, w_in, g_norm, w_o, lb, B, S):
    M, D = h.shape
    H = w_o.shape[0] // HEAD_DIM
    proj = _linear(h, w_in.astype(BF16), name="hg_in", gain=gain, out_dtype=F32)
    o = _hgrn2(proj.reshape(B, S, -1), lb, g_norm, H)
    return _linear(o.reshape(M, -1), w_o.astype(BF16), name="hg_out", residual=h)


def _mla_layer(h, positions, gain, w_dkv, q_norm, kv_norm, w_uq, w_ukv, w_o, B, S):
    M, D = h.shape
    H = w_o.shape[0] // MLA_NOPE
    qr, kvr = MLA_Q_RANK, MLA_KV_RANK
    pad_a = jnp.zeros((D, qr - kvr), F32)
    pad_b = jnp.zeros((D, HEAD_DIM - MLA_ROPE), F32)
    w_d = jnp.concatenate([w_dkv[:, qr:qr + kvr], pad_a, w_dkv[:, :qr], w_dkv[:, qr + kvr:], pad_b], axis=1)
    dkv = _linear(h, w_d.astype(BF16), name="mla_down", gain=gain, out_dtype=F32)
    kpe, cos, sin = _rope_tables(positions, dkv, (2 * qr) // HEAD_DIM)
    w_q = w_uq.reshape(qr, H, MLA_NOPE + MLA_ROPE) * ((MLA_NOPE + MLA_ROPE) ** -0.5 * LOG2E)
    w_q = jnp.concatenate([w_q, jnp.zeros((qr, H, HEAD_DIM - MLA_ROPE), F32)], axis=2).reshape(qr, H * 2 * HEAD_DIM)
    q = _linear(dkv, w_q.astype(BF16), name="mla_q", gain=q_norm, rope=(cos, sin), out_dtype=BF16, x_col=1)
    kv = _linear(dkv, w_ukv.astype(BF16), name="mla_kv", gain=kv_norm, out_dtype=BF16, x_col=0)
    o = _mla_attention(q.reshape(B, S, -1), kv.reshape(B, S, -1), kpe.reshape(B, S, -1), H)
    return _linear(o.reshape(M, -1), w_o.astype(BF16), name="mla_out", residual=h)


def kernel(x, positions, norm_mix, norm_mlp, final_norm, sb_w_qkv, sb_w_o, hg_w_in, hg_lb_logits, hg_g_norm, hg_w_o, mla_w_dkv, mla_q_norm, mla_kv_norm, mla_w_uq, mla_w_ukv, mla_w_o, mlp_w1, mlp_w2):
    B, S, D = x.shape
    depth = norm_mix.shape[0]
    assert depth >= 1
    p_lb = jax.nn.softmax(hg_lb_logits.astype(F32), axis=0)
    lb_all = jnp.cumsum(p_lb, axis=0) - p_lb[0]
    h = x.reshape(B * S, D)
    w1, w2 = mlp_w1.astype(BF16), mlp_w2.astype(BF16)
    for i in range(depth):
        m, j = i % N_MIXERS, i // N_MIXERS
        if m == 0:
            h = _sb_layer(h, norm_mix[i], sb_w_qkv[j], sb_w_o[j], B, S)
        elif m == 1:
            h = _hg_layer(h, norm_mix[i], hg_w_in[j], hg_g_norm[j], hg_w_o[j], lb_all[i], B, S)
        else:
            h = _mla_layer(h, positions, norm_mix[i], mla_w_dkv[j], mla_q_norm[j], mla_kv_norm[j],
                           mla_w_uq[j], mla_w_ukv[j], mla_w_o[j], B, S)
        h = _mlp(h, norm_mlp[i], w1, w2, i, out_gain=final_norm if i == depth - 1 else None)
    return h.reshape(B, S, D)
```

```python
import functools

import jax
import jax.numpy as jnp
from jax import lax
from jax.experimental import pallas as pl
from jax.experimental.pallas import tpu as pltpu

F32 = jnp.float32
BF16 = jnp.bfloat16

RMS_EPS = 1e-6
N_MIXERS = 3
HEAD_DIM = 128
MLA_NOPE = 128
MLA_ROPE = 64
MLA_Q_RANK = 768
MLA_KV_RANK = 512
MLA_CHUNK = 64
ROPE_THETA = 10000.0
LOG2E = 1.4426950408889634
VMEM_LIMIT = 56 * 1024 * 1024


def _params(*sem):
    return pltpu.CompilerParams(dimension_semantics=sem, vmem_limit_bytes=VMEM_LIMIT)


def _dot(a, b):
    return jnp.dot(a, b, preferred_element_type=F32)


def _dot_nt(a, b):
    return lax.dot_general(a, b, (((1,), (1,)), ((), ())), preferred_element_type=F32)


def _dot_tn(a, b):
    return lax.dot_general(a, b, (((0,), (0,)), ((), ())), preferred_element_type=F32)


def _split3(x):
    hi = x.astype(BF16)
    r1 = x - hi.astype(F32)
    mid = r1.astype(BF16)
    lo = (r1 - mid.astype(F32)).astype(BF16)
    return hi, mid, lo


def _rotate_half(x, cos, sin):
    half = MLA_ROPE // 2
    swapped = pltpu.roll(x, half, 1) + pltpu.roll(x, HEAD_DIM - half, 1)
    return x * cos + swapped * sin


def _linear_kernel(*refs, has_gain, has_res, has_rope):
    x_ref, w_ref = refs[0], refs[1]
    pos = 2
    g_ref = r_ref = cos_ref = sin_ref = None
    if has_gain:
        g_ref = refs[pos]
        pos += 1
    if has_res:
        r_ref = refs[pos]
        pos += 1
    if has_rope:
        cos_ref, sin_ref = refs[pos], refs[pos + 1]
        pos += 2
    o_ref = refs[pos]

    if has_gain:
        xn_ref = refs[pos + 1]

        @pl.when(pl.program_id(1) == 0)
        def _():
            x = x_ref[...].astype(F32)
            ms = jnp.mean(x * x, axis=-1, keepdims=True)
            xn_ref[...] = (x * lax.rsqrt(ms + RMS_EPS) * g_ref[...]).astype(BF16)

        lhs = xn_ref[...]
    else:
        lhs = x_ref[...]

    acc = _dot(lhs, w_ref[...])
    if has_res:
        acc = acc + r_ref[...]
    if has_rope:
        cos, sin = cos_ref[...], sin_ref[...]
        for c in range(0, acc.shape[1], 2 * HEAD_DIM):
            o_ref[:, c:c + HEAD_DIM] = acc[:, c:c + HEAD_DIM].astype(o_ref.dtype)
            rot = _rotate_half(acc[:, c + HEAD_DIM:c + 2 * HEAD_DIM], cos, sin)
            o_ref[:, c + HEAD_DIM:c + 2 * HEAD_DIM] = rot.astype(o_ref.dtype)
    else:
        o_ref[...] = acc.astype(o_ref.dtype)


LINEAR_VMEM_BUDGET = 44 * 1024 * 1024
RESIDENT_WEIGHT_BYTES = 8 * 1024 * 1024


def _linear_tiles(M, K, N, x_bytes, out_bytes, has_gain, has_res, has_rope):
    widths = [N] if K * N * 2 <= RESIDENT_WEIGHT_BYTES or N % 1024 else [w for w in (4096, 2048, 1024) if N % w == 0]
    for tm in (1024, 512, 256, 128):
        if M % tm:
            continue
        for tn in widths:
            need = 2 * tm * K * x_bytes + 2 * K * tn * 2 + 2 * tm * tn * out_bytes
            need += tm * K * 2 if has_gain else 0
            need += 2 * tm * tn * 4 if has_res else 0
            need += tm * tn * 4 + 4 * tm * HEAD_DIM * 4 if has_rope else 0
            if need <= LINEAR_VMEM_BUDGET:
                return tm, tn
    raise ValueError("no linear tiling fits VMEM")


def _linear(x, w, *, name, gain=None, residual=None, rope=None, out_dtype=F32, x_col=0):
    M = x.shape[0]
    K, N = w.shape
    assert gain is not None or x.dtype == BF16
    tm, tn = _linear_tiles(M, K, N, x.dtype.itemsize, jnp.dtype(out_dtype).itemsize,
                           gain is not None, residual is not None, rope is not None)
    assert rope is None or tn % (2 * HEAD_DIM) == 0
    in_specs = [pl.BlockSpec((tm, K), lambda i, j: (i, x_col)),
                pl.BlockSpec((K, tn), lambda i, j: (0, j))]
    args = [x, w]
    if gain is not None:
        in_specs.append(pl.BlockSpec((1, K), lambda i, j: (0, 0)))
        args.append(gain.reshape(1, K).astype(F32))
    if residual is not None:
        in_specs.append(pl.BlockSpec((tm, tn), lambda i, j: (i, j)))
        args.append(residual)
    if rope is not None:
        in_specs += [pl.BlockSpec((tm, HEAD_DIM), lambda i, j: (i, 0))] * 2
        args += list(rope)
    return pl.pallas_call(
        functools.partial(_linear_kernel, has_gain=gain is not None, has_res=residual is not None,
                          has_rope=rope is not None),
        grid=(M // tm, N // tn),
        in_specs=in_specs,
        out_specs=pl.BlockSpec((tm, tn), lambda i, j: (i, j)),
        out_shape=jax.ShapeDtypeStruct((M, N), out_dtype),
        scratch_shapes=[pltpu.VMEM((tm, K), BF16)] if gain is not None else [],
        compiler_params=_params("parallel", "arbitrary"),
        name=name,
    )(*args)


def _mlp_kernel(*refs, has_out_norm):
    if has_out_norm:
        x_ref, g_ref, w1_ref, w2_ref, go_ref, o_ref, xn_ref = refs
    else:
        x_ref, g_ref, w1_ref, w2_ref, o_ref, xn_ref = refs
    f = pl.program_id(1)

    @pl.when(f == 0)
    def _():
        x = x_ref[...]
        ms = jnp.mean(x * x, axis=-1, keepdims=True)
        xn_ref[...] = (x * lax.rsqrt(ms + RMS_EPS) * g_ref[...]).astype(BF16)
        o_ref[...] = x

    u = _dot(xn_ref[...], w1_ref[...])
    r = jnp.square(jnp.maximum(u, 0.0)).astype(BF16)
    o_ref[...] += _dot(r, w2_ref[...])

    if has_out_norm:
        @pl.when(f == pl.num_programs(1) - 1)
        def _():
            y = o_ref[...]
            ms = jnp.mean(y * y, axis=-1, keepdims=True)
            o_ref[...] = y * lax.rsqrt(ms + RMS_EPS) * go_ref[...]


def _mlp(h, gain, w1, w2, layer, *, out_gain=None, tm=512, tf=1024):
    M, D = h.shape
    F = w1.shape[2]
    assert M % tm == 0 and F % tf == 0
    vec = pl.BlockSpec((1, D), lambda i, f: (0, 0))
    in_specs = [pl.BlockSpec((tm, D), lambda i, f: (i, 0)), vec,
                pl.BlockSpec((None, D, tf), lambda i, f: (layer, 0, f)),
                pl.BlockSpec((None, tf, D), lambda i, f: (layer, f, 0))]
    args = [h, gain.reshape(1, D).astype(F32), w1, w2]
    if out_gain is not None:
        in_specs.append(vec)
        args.append(out_gain.reshape(1, D).astype(F32))
    return pl.pallas_call(
        functools.partial(_mlp_kernel, has_out_norm=out_gain is not None),
        grid=(M // tm, F // tf),
        in_specs=in_specs,
        out_specs=pl.BlockSpec((tm, D), lambda i, f: (i, 0)),
        out_shape=jax.ShapeDtypeStruct((M, D), F32),
        scratch_shapes=[pltpu.VMEM((tm, D), BF16)],
        compiler_params=_params("parallel", "arbitrary"),
        name="mlp",
    )(*args)


SB_SKIP = 90.0


def _sb_kernel(q_ref, k_ref, v_ref, o_ref, *, t, n_sub):
    i = pl.program_id(2)
    row = lax.broadcasted_iota(jnp.int32, (t, t), 0)
    col = lax.broadcasted_iota(jnp.int32, (t, t), 1)
    later = (row > col).astype(BF16)
    causal = col < row

    def logits(q, j):
        return _dot_nt(q, k_ref[0, pl.ds(pl.multiple_of(j * t, t), t), :])

    def log_gates(z, diagonal):
        nz = -z
        lom = jnp.minimum(nz, 0.0) - jnp.log2(1.0 + jnp.exp2(jnp.minimum(z, nz)))
        log_beta = lom + z
        if diagonal:
            lom = jnp.where(causal, lom, 0.0)
        return log_beta, lom

    def later_sums(lom):
        return _dot(lom.astype(BF16), later), jnp.sum(lom, axis=1, keepdims=True)

    def gates(q, j, diagonal):
        log_beta, lom = log_gates(logits(q, j), diagonal)
        return (log_beta,) + later_sums(lom)

    def weigh(j, log_beta, after, diagonal):
        a = jnp.exp2(log_beta + after)
        if diagonal:
            a = jnp.where(causal, a, 0.0)
        return _dot(a.astype(BF16), v_ref[0, pl.ds(pl.multiple_of(j * t, t), t), :])

    skip = -SB_SKIP * LOG2E
    tiles = []
    for a in range(n_sub):
        jd = i * n_sub + a
        tiles.append((slice(a * t, (a + 1) * t), q_ref[0, a * t:(a + 1) * t, :], jd, jnp.maximum(jd - 1, 0)))
    z_d = [logits(q, jd) for _, q, jd, _ in tiles]
    z_p = [logits(q, jp) for _, q, _, jp in tiles]
    g_d = [log_gates(z, True) for z in z_d]
    g_p = [log_gates(z, False) for z in z_p]
    s_d = [later_sums(lom) for _, lom in g_d]
    s_p = [later_sums(lom) for _, lom in g_p]
    near = []
    pending = jnp.full((t, 1), skip, F32)
    for (rows, q, jd, jp), (lb_d, _), (lb_p, _), (within_d, carry), (within_p, total_p) in zip(tiles, g_d, g_p, s_d, s_p):
        has_prev = jd > 0
        acc = weigh(jd, lb_d, within_d, True) + jnp.where(has_prev, weigh(jp, lb_p, within_p + carry, False), 0.0)
        carry = carry + jnp.where(has_prev, total_p, 0.0)
        o_ref[0, rows, :] = acc.astype(o_ref.dtype)
        near.append((rows, q, jd, carry, acc))
        pending = jnp.maximum(pending, jnp.where(jd >= 2, carry, skip))

    def cond(s):
        j, carry, _ = s
        return jnp.logical_and(j >= 0, jnp.max(carry) > skip)

    @pl.when(jnp.max(pending) > skip)
    def _():
        for rows, q, jd, carry, acc in near:
            def body(s, q=q):
                j, carry, acc = s
                log_beta, within, total = gates(q, j, False)
                return j - 1, carry + total, acc + weigh(j, log_beta, within + carry, False)

            _, _, acc = lax.while_loop(cond, body, (jd - 2, carry, acc))
            o_ref[0, rows, :] = acc.astype(o_ref.dtype)


def _sb_attention(qkv, n_heads, *, t=256, n_sub=4):
    B, S, _ = qkv.shape
    d = HEAD_DIM
    tile = t * n_sub
    assert S % tile == 0
    return pl.pallas_call(
        functools.partial(_sb_kernel, t=t, n_sub=n_sub),
        grid=(B, n_heads, S // tile),
        in_specs=[pl.BlockSpec((1, tile, d), lambda b, h, i: (b, i, h)),
                  pl.BlockSpec((1, S, d), lambda b, h, i: (b, 0, n_heads + h)),
                  pl.BlockSpec((1, S, d), lambda b, h, i: (b, 0, 2 * n_heads + h))],
        out_specs=pl.BlockSpec((1, tile, d), lambda b, h, i: (b, i, h)),
        out_shape=jax.ShapeDtypeStruct((B, S, n_heads * d), BF16),
        compiler_params=_params("parallel", "parallel", "arbitrary"),
        name="sb_attn",
    )(qkv, qkv, qkv)


HG_CHUNK = 64
HG_SUB = 16
HG_GROUP = 4
HG_SPAN = 256
HG_SAFE = 70.0


def _hg_kernel(q_ref, fz_ref, v_ref, g_ref, lb_ref, gn_ref, tril_ref, o_ref, state_ref, oc_ref, b_ref, k_ref,
               *, n_chunks):
    C, U, d = HG_CHUNK, HG_SUB, HEAD_DIM

    @pl.when(pl.program_id(2) == 0)
    def _():
        state_ref[...] = jnp.zeros_like(state_ref)

    gn = gn_ref[...]
    row = lax.broadcasted_iota(jnp.int32, (C, C), 0)
    col = lax.broadcasted_iota(jnp.int32, (C, C), 1)
    lower = col <= row
    sub_row = lax.broadcasted_iota(jnp.int32, (U, 1), 0)
    tril = tril_ref[...]

    heads = []
    for hd in range(HG_GROUP):
        lanes = slice(hd * d, (hd + 1) * d)
        lb = lb_ref[:, lanes]
        fz = fz_ref[0, :, lanes]
        e = jnp.exp(-jnp.abs(fz))
        inv = 1.0 / (1.0 + e)
        sig_pos = jnp.where(fz >= 0, inv, e * inv)
        sig_neg = jnp.where(fz >= 0, e * inv, inv)
        log_f = jnp.log(lb + (1.0 - lb) * sig_pos)
        k = (1.0 - lb) * sig_neg
        l_hi = log_f.astype(BF16)
        l_lo = (log_f - l_hi.astype(F32)).astype(BF16)
        span = tril.shape[0]
        b = jnp.concatenate([_dot(tril, l_hi[r:r + span]) + _dot(tril, l_lo[r:r + span])
                             for r in range(0, fz.shape[0], span)], axis=0)
        heads.append((lanes, k, b))
    safe = functools.reduce(jnp.minimum, [jnp.min(b) for _, _, b in heads]) > -HG_SAFE

    def epilogue(o, g):
        o = o * lax.rsqrt(jnp.mean(o * o, axis=-1, keepdims=True) + RMS_EPS) * gn
        return (o * (g / (1.0 + jnp.exp(-g)))).astype(o_ref.dtype)

    @pl.when(safe)
    def _():
        prep = []
        for lanes, k, b in heads:
            q = q_ref[0, :, lanes]
            v16 = v_ref[0, :, lanes].astype(BF16)
            qe = (q * jnp.exp(b)).astype(BF16)
            ke = (k * jnp.exp(-b)).astype(BF16)
            ts = b.shape[0]
            b_last = b.reshape(n_chunks, C, d)[:, C - 1:C, :]
            to_end = (jnp.broadcast_to(b_last, (n_chunks, C, d)).reshape(ts, d) - b)
            kd = (k * jnp.exp(to_end)).astype(BF16)
            prep.append((qe, ke, v16, kd, jnp.exp(b_last)))
        states = [state_ref[hd] for hd in range(HG_GROUP)]
        for c in range(n_chunks):
            rows = slice(c * C, (c + 1) * C)
            for hd, (qe, ke, v16, kd, decay) in enumerate(prep):
                s = jnp.where(lower, _dot_nt(qe[rows], ke[rows]), 0.0)
                oc_ref[hd, rows, :] = (_dot(s.astype(BF16), v16[rows])
                                       + _dot_nt(qe[rows], states[hd].astype(BF16)))
                states[hd] = states[hd] * decay[c] + _dot_tn(v16[rows], kd[rows])
        for hd, (lanes, _, _) in enumerate(heads):
            state_ref[hd] = states[hd]
            o_ref[0, :, lanes] = epilogue(oc_ref[hd], g_ref[0, :, lanes])

    def head_chunk(hd, r0):
        lanes = slice(hd * d, (hd + 1) * d)
        q = q_ref[0, pl.ds(r0, C), lanes]
        v = v_ref[0, pl.ds(r0, C), lanes]
        b = b_ref[hd, pl.ds(r0, C), :]
        k = k_ref[hd, pl.ds(r0, C), :]
        log2_k = jnp.log2(k)
        state_t = state_ref[hd]
        rows = pl.ds(r0, C)
        oc_ref[hd, rows, :] = _dot_nt((q * jnp.exp(b)).astype(BF16), state_t.astype(BF16))
        v16 = v.astype(BF16)
        for J in range(C // U - 1):
            lo_r, hi_r = J * U, (J + 1) * U
            ref_b = b[hi_r - 1:hi_r, :]
            kt = (k[lo_r:hi_r] * jnp.exp(ref_b - b[lo_r:hi_r])).astype(BF16)
            qs = (q[hi_r:] * jnp.exp(b[hi_r:] - ref_b)).astype(BF16)
            s = _dot_nt(qs, kt)
            oc_ref[hd, pl.ds(r0 + hi_r, C - hi_r), :] += _dot(s.astype(BF16), v16[lo_r:hi_r])
        b2 = b * LOG2E
        key2 = log2_k - b2
        for J in range(C // U):
            lo_r, hi_r = J * U, (J + 1) * U
            qj, bj, kj, vj = q[lo_r:hi_r], b2[lo_r:hi_r], key2[lo_r:hi_r], v[lo_r:hi_r]
            acc = jnp.zeros((U, d), F32)
            for s_i in range(U):
                w = qj * jnp.exp2(jnp.minimum(bj + kj[s_i:s_i + 1], 0.0))
                sc = jnp.sum(w, axis=1, keepdims=True)
                sc = jnp.where(sub_row >= s_i, sc, 0.0)
                acc = acc + sc * vj[s_i:s_i + 1]
            oc_ref[hd, pl.ds(r0 + lo_r, U), :] += acc
        b_last = b[C - 1:C, :]
        kd = (k * jnp.exp(b_last - b)).astype(BF16)
        state_ref[hd] = state_t * jnp.exp(b_last) + _dot_tn(v16, kd)
        o_ref[0, rows, lanes] = epilogue(oc_ref[hd, rows, :], g_ref[0, rows, lanes])

    @pl.when(jnp.logical_not(safe))
    def _():
        for hd, (_, k, b) in enumerate(heads):
            b_ref[hd] = b
            k_ref[hd] = k

        def chunk(c, _):
            r0 = pl.multiple_of(c * C, C)
            for hd in range(HG_GROUP):
                head_chunk(hd, r0)
            return 0

        lax.fori_loop(0, n_chunks, chunk, 0)


def _hgrn2(proj, lb, g_norm, n_heads, *, ts=512):
    B, S, _ = proj.shape
    d, G = HEAD_DIM, HG_GROUP
    assert S % ts == 0 and ts % HG_CHUNK == 0 and n_heads % G == 0
    n_groups = n_heads // G

    def col(part):
        return pl.BlockSpec((1, ts, G * d), lambda b, h, s: (b, s, part * n_groups + h))

    span = min(ts, HG_SPAN)
    r = jnp.arange(span)
    tril = ((r[None, :] <= r[:, None]) & (r[None, :] // HG_CHUNK == r[:, None] // HG_CHUNK)).astype(BF16)
    return pl.pallas_call(
        functools.partial(_hg_kernel, n_chunks=ts // HG_CHUNK),
        grid=(B, n_groups, S // ts),
        in_specs=[col(0), col(1), col(2), col(3),
                  pl.BlockSpec((1, G * d), lambda b, h, s: (0, h)),
                  pl.BlockSpec((1, d), lambda b, h, s: (0, 0)),
                  pl.BlockSpec((span, span), lambda b, h, s: (0, 0))],
        out_specs=pl.BlockSpec((1, ts, G * d), lambda b, h, s: (b, s, h)),
        out_shape=jax.ShapeDtypeStruct((B, S, n_heads * d), BF16),
        scratch_shapes=[pltpu.VMEM((G, d, d), F32), pltpu.VMEM((G, ts, d), F32),
                        pltpu.VMEM((G, ts, d), F32), pltpu.VMEM((G, ts, d), F32)],
        compiler_params=_params("parallel", "parallel", "arbitrary"),
        name="hgrn2",
    )(proj, proj, proj, proj, lb.reshape(1, n_heads * d), g_norm.reshape(1, d), tril)


def _rope_kernel(pos_ref, invf_ref, x_ref, kpe_ref, cos_ref, sin_ref):
    half = MLA_ROPE // 2
    ang = pos_ref[...].astype(F32) * invf_ref[...]
    lane = lax.broadcasted_iota(jnp.int32, ang.shape, 1)
    c = jnp.where(lane < MLA_ROPE, jnp.cos(ang), 0.0)
    s = jnp.sin(ang)
    s = jnp.where(lane < half, -s, jnp.where(lane < MLA_ROPE, s, 0.0))
    cos_ref[...] = c
    sin_ref[...] = s
    kpe_ref[...] = _rotate_half(x_ref[...], c, s).astype(kpe_ref.dtype)


def _rope_tables(positions, dkv, kpe_col, *, tm=512):
    M = dkv.shape[0]
    half = MLA_ROPE // 2
    freq = ROPE_THETA ** (-jnp.arange(0, MLA_ROPE, 2, dtype=F32) / MLA_ROPE)
    invf = jnp.concatenate([freq, freq, jnp.zeros((HEAD_DIM - 2 * half,), F32)]).reshape(1, HEAD_DIM)
    row = pl.BlockSpec((tm, HEAD_DIM), lambda i: (i, 0))
    return pl.pallas_call(
        _rope_kernel,
        grid=(M // tm,),
        in_specs=[pl.BlockSpec((tm, 1), lambda i: (i, 0)),
                  pl.BlockSpec((1, HEAD_DIM), lambda i: (0, 0)),
                  pl.BlockSpec((tm, HEAD_DIM), lambda i: (i, kpe_col))],
        out_specs=[row, row, row],
        out_shape=[jax.ShapeDtypeStruct((M, HEAD_DIM), BF16),
                   jax.ShapeDtypeStruct((M, HEAD_DIM), F32),
                   jax.ShapeDtypeStruct((M, HEAD_DIM), F32)],
        compiler_params=_params("parallel"),
        name="mla_rope",
    )(positions.reshape(M, 1), invf, dkv)


MLA_ROW_CHUNK = 256


def _mla_kernel(q_ref, kn_ref, v_ref, kpe_ref, o_ref, kcat_ref, v1_ref, sa_ref, sb_ref, m_ref, acc_ref, *, t):
    n_q = q_ref.shape[1] // t
    n_pairs = n_q * (n_q + 1) // 2
    for n in range(n_q):
        rows = slice(n * t, (n + 1) * t)
        kcat_ref[:MLA_NOPE, rows] = kn_ref[0, rows, :].T
        kcat_ref[MLA_NOPE:, rows] = kpe_ref[0, rows, :].T
    rc = MLA_ROW_CHUNK
    row = lax.broadcasted_iota(jnp.int32, (rc, t), 0) // MLA_CHUNK
    col = lax.broadcasted_iota(jnp.int32, (rc, t), 1) // MLA_CHUNK

    def tile(n):
        return pl.ds(pl.multiple_of(n * t, t), t)

    def scores(i, j, dst_ref):
        dst_ref[...] = _dot(q_ref[0, tile(i), :], kcat_ref[:, tile(j)])

    v1_ref[:, :HEAD_DIM] = v_ref[0]
    v1_ref[:, HEAD_DIM:] = jnp.ones((v1_ref.shape[0], HEAD_DIM), BF16)

    def reset():
        m_ref[...] = jnp.full_like(m_ref, -jnp.inf)
        acc_ref[...] = jnp.zeros_like(acc_ref)

    def absorb(j, src_ref, diagonal):
        vb = v1_ref[tile(j), :]
        for r in range(t // rc):
            rows = slice(r * rc, (r + 1) * rc)
            s = src_ref[rows, :]
            if diagonal:
                s = jnp.where(col <= row + (r * rc) // MLA_CHUNK, s, -jnp.inf)
            m_old = m_ref[rows, :]
            lane_groups = [s[:, c:c + HEAD_DIM] for c in range(0, t, HEAD_DIM)]
            m_col = jnp.max(functools.reduce(jnp.maximum, lane_groups + [m_old]), axis=1, keepdims=True)
            m_new = jnp.broadcast_to(m_col, m_old.shape)
            alpha = jnp.exp2(m_old - m_new)
            p = jnp.exp2((s - m_col).astype(BF16))
            acc_ref[rows, :] = jnp.concatenate([alpha, alpha], axis=1) * acc_ref[rows, :] + _dot(p, vb)
            m_ref[rows, :] = m_new

    def successor(i, j):
        last = j == i
        return jnp.where(last, i + 1, i), jnp.where(last, 0, j + 1)

    def step(cur_ref, nxt_ref, i, j):
        ni, nj = successor(i, j)
        ni = jnp.minimum(ni, n_q - 1)

        @pl.when(j < i)
        def _():
            scores(ni, nj, nxt_ref)
            absorb(j, cur_ref, False)

        @pl.when(j == i)
        def _():
            scores(ni, nj, nxt_ref)
            absorb(j, cur_ref, True)
            o_ref[0, tile(i), :] = (acc_ref[:, :HEAD_DIM] / acc_ref[:, HEAD_DIM:]).astype(o_ref.dtype)
            reset()

    def two_steps(_, ij):
        i, j = ij
        i1, j1 = successor(i, j)

        @pl.when(j + 1 < i)
        def _():
            scores(i, j + 1, sb_ref)
            absorb(j, sa_ref, False)
            scores(i, j + 2, sa_ref)
            absorb(j + 1, sb_ref, False)

        @pl.when(j + 1 >= i)
        def _():
            step(sa_ref, sb_ref, i, j)
            step(sb_ref, sa_ref, i1, j1)

        return successor(i1, j1)

    reset()
    scores(0, 0, sa_ref)
    ij = lax.fori_loop(0, n_pairs // 2, two_steps, (jnp.int32(0), jnp.int32(0)))
    if n_pairs % 2:
        step(sa_ref, sb_ref, *ij)


def _mla_attention(q, kv, kpe, n_heads, *, t=512):
    B, S, _ = q.shape
    d = HEAD_DIM
    assert S % t == 0 and t % MLA_CHUNK == 0
    return pl.pallas_call(
        functools.partial(_mla_kernel, t=t),
        grid=(B, n_heads),
        in_specs=[pl.BlockSpec((1, S, 2 * d), lambda b, h: (b, 0, h)),
                  pl.BlockSpec((1, S, d), lambda b, h: (b, 0, 2 * h)),
                  pl.BlockSpec((1, S, d), lambda b, h: (b, 0, 2 * h + 1)),
                  pl.BlockSpec((1, S, d), lambda b, h: (b, 0, 0))],
        out_specs=pl.BlockSpec((1, S, d), lambda b, h: (b, 0, h)),
        out_shape=jax.ShapeDtypeStruct((B, S, n_heads * d), BF16),
        scratch_shapes=[pltpu.VMEM((2 * d, S), BF16), pltpu.VMEM((S, 2 * d), BF16),
                        pltpu.VMEM((t, t), F32), pltpu.VMEM((t, t), F32),
                        pltpu.VMEM((t, d), F32), pltpu.VMEM((t, 2 * d), F32)],
        compiler_params=_params("parallel", "parallel"),
        name="mla_attn",
    )(q, kv, kv, kpe)


def _sb_layer(h, gain, w_qkv, w_o, B, S):
    M, D = h.shape
    H = w_o.shape[0] // HEAD_DIM
    col_scale = jnp.where(jnp.arange(w_qkv.shape[1]) < H * HEAD_DIM, HEAD_DIM ** -0.5 * LOG2E, 1.0)
    qkv = _linear(h, (w_qkv * col_scale).astype(BF16), name="sb_qkv", gain=gain, out_dtype=BF16)
    o = _sb_attention(qkv.reshape(B, S, -1), H)
    return _linear(o.reshape(M, -1), w_o.astype(BF16), name="sb_out", residual=h)


def _hg_layer(h, gain, w_in, g_norm, w_o, lb, B, S):
    M, D = h.shape
    H = w_o.shape[0] // HEAD_DIM
    proj = _linear(h, w_in.astype(BF16), name="hg_in", gain=gain, out_dtype=F32)
    o = _hgrn2(proj.reshape(B, S, -1), lb, g_norm, H)
    return _linear(o.reshape(M, -1), w_o.astype(BF16), name="hg_out", residual=h)


def _mla_layer(h, positions, gain, w_dkv, q_norm, kv_norm, w_uq, w_ukv, w_o, B, S):
    M, D = h.shape
    H = w_o.shape[0] // MLA_NOPE
    qr, kvr = MLA_Q_RANK, MLA_KV_RANK
    pad_a = jnp.zeros((D, qr - kvr), F32)
    pad_b = jnp.zeros((D, HEAD_DIM - MLA_ROPE), F32)
    w_d = jnp.concatenate([w_dkv[:, qr:qr + kvr], pad_a, w_dkv[:, :qr], w_dkv[:, qr + kvr:], pad_b], axis=1)
    dkv = _linear(h, w_d.astype(BF16), name="mla_down", gain=gain, out_dtype=F32)
    kpe, cos, sin = _rope_tables(positions, dkv, (2 * qr) // HEAD_DIM)
    w_q = w_uq.reshape(qr, H, MLA_NOPE + MLA_ROPE) * ((MLA_NOPE + MLA_ROPE) ** -0.5 * LOG2E)
    w_q = jnp.concatenate([w_q, jnp.zeros((qr, H, HEAD_DIM - MLA_ROPE), F32)], axis=2).reshape(qr, H * 2 * HEAD_DIM)
    q = _linear(dkv, w_q.astype(BF16), name="mla_q", gain=q_norm, rope=(cos, sin), out_dtype=BF16, x_col=1)
    kv = _linear(dkv, w_ukv.astype(BF16), name="mla_kv", gain=kv_norm, out_dtype=BF16, x_col=0)
    o = _mla_attention(q.reshape(B, S, -1), kv.reshape(B, S, -1), kpe.reshape(B, S, -1), H)
    return _linear(o.reshape(M, -1), w_o.astype(BF16), name="mla_out", residual=h)


def kernel(x, positions, norm_mix, norm_mlp, final_norm, sb_w_qkv, sb_w_o, hg_w_in, hg_lb_logits, hg_g_norm, hg_w_o, mla_w_dkv, mla_q_norm, mla_kv_norm, mla_w_uq, mla_w_ukv, mla_w_o, mlp_w1, mlp_w2):
    B, S, D = x.shape
    depth = norm_mix.shape[0]
    assert depth >= 1
    p_lb = jax.nn.softmax(hg_lb_logits.astype(F32), axis=0)
    lb_all = jnp.cumsum(p_lb, axis=0) - p_lb[0]
    h = x.reshape(B * S, D)
    w1, w2 = mlp_w1.astype(BF16), mlp_w2.astype(BF16)
    for i in range(depth):
        m, j = i % N_MIXERS, i // N_MIXERS
        if m == 0:
            h = _sb_layer(h, norm_mix[i], sb_w_qkv[j], sb_w_o[j], B, S)
        elif m == 1:
            h = _hg_layer(h, norm_mix[i], hg_w_in[j], hg_g_norm[j], hg_w_o[j], lb_all[i], B, S)
        else:
            h = _mla_layer(h, positions, norm_mix[i], mla_w_dkv[j], mla_q_norm[j], mla_kv_norm[j],
                           mla_w_uq[j], mla_w_ukv[j], mla_w_o[j], B, S)
        h = _mlp(h, norm_mlp[i], w1, w2, i, out_gain=final_norm if i == depth - 1 else None)
    return h.reshape(B, S, D)
```

```python
import functools

import jax
import jax.numpy as jnp
from jax import lax
from jax.experimental import pallas as pl
from jax.experimental.pallas import tpu as pltpu

F32 = jnp.float32
BF16 = jnp.bfloat16

RMS_EPS = 1e-6
N_MIXERS = 3
HEAD_DIM = 128
MLA_NOPE = 128
MLA_ROPE = 64
MLA_Q_RANK = 768
MLA_KV_RANK = 512
MLA_CHUNK = 64
ROPE_THETA = 10000.0
LOG2E = 1.4426950408889634
VMEM_LIMIT = 56 * 1024 * 1024


def _params(*sem):
    return pltpu.CompilerParams(dimension_semantics=sem, vmem_limit_bytes=VMEM_LIMIT)


def _dot(a, b):
    return jnp.dot(a, b, preferred_element_type=F32)


def _dot_nt(a, b):
    return lax.dot_general(a, b, (((1,), (1,)), ((), ())), preferred_element_type=F32)


def _dot_tn(a, b):
    return lax.dot_general(a, b, (((0,), (0,)), ((), ())), preferred_element_type=F32)


def _split3(x):
    hi = x.astype(BF16)
    r1 = x - hi.astype(F32)
    mid = r1.astype(BF16)
    lo = (r1 - mid.astype(F32)).astype(BF16)
    return hi, mid, lo


def _rotate_half(x, cos, sin):
    half = MLA_ROPE // 2
    swapped = pltpu.roll(x, half, 1) + pltpu.roll(x, HEAD_DIM - half, 1)
    return x * cos + swapped * sin


def _linear_kernel(*refs, has_gain, has_res, has_rope):
    x_ref, w_ref = refs[0], refs[1]
    pos = 2
    g_ref = r_ref = cos_ref = sin_ref = None
    if has_gain:
        g_ref = refs[pos]
        pos += 1
    if has_res:
        r_ref = refs[pos]
        pos += 1
    if has_rope:
        cos_ref, sin_ref = refs[pos], refs[pos + 1]
        pos += 2
    o_ref = refs[pos]

    if has_gain:
        xn_ref = refs[pos + 1]

        @pl.when(pl.program_id(1) == 0)
        def _():
            x = x_ref[...].astype(F32)
            ms = jnp.mean(x * x, axis=-1, keepdims=True)
            xn_ref[...] = (x * lax.rsqrt(ms + RMS_EPS) * g_ref[...]).astype(BF16)

        lhs = xn_ref[...]
    else:
        lhs = x_ref[...]

    acc = _dot(lhs, w_ref[...])
    if has_res:
        acc = acc + r_ref[...]
    if has_rope:
        cos, sin = cos_ref[...], sin_ref[...]
        for c in range(0, acc.shape[1], 2 * HEAD_DIM):
            o_ref[:, c:c + HEAD_DIM] = acc[:, c:c + HEAD_DIM].astype(o_ref.dtype)
            rot = _rotate_half(acc[:, c + HEAD_DIM:c + 2 * HEAD_DIM], cos, sin)
            o_ref[:, c + HEAD_DIM:c + 2 * HEAD_DIM] = rot.astype(o_ref.dtype)
    else:
        o_ref[...] = acc.astype(o_ref.dtype)


LINEAR_VMEM_BUDGET = 44 * 1024 * 1024
RESIDENT_WEIGHT_BYTES = 8 * 1024 * 1024


def _linear_tiles(M, K, N, x_bytes, out_bytes, has_gain, has_res, has_rope):
    widths = [N] if K * N * 2 <= RESIDENT_WEIGHT_BYTES or N % 1024 else [w for w in (4096, 2048, 1024) if N % w == 0]
    for tm in (1024, 512, 256, 128):
        if M % tm:
            continue
        for tn in widths:
            need = 2 * tm * K * x_bytes + 2 * K * tn * 2 + 2 * tm * tn * out_bytes
            need += tm * K * 2 if has_gain else 0
            need += 2 * tm * tn * 4 if has_res else 0
            need += tm * tn * 4 + 4 * tm * HEAD_DIM * 4 if has_rope else 0
            if need <= LINEAR_VMEM_BUDGET:
                return tm, tn
    raise ValueError("no linear tiling fits VMEM")


def _linear(x, w, *, name, gain=None, residual=None, rope=None, out_dtype=F32, x_col=0):
    M = x.shape[0]
    K, N = w.shape
    assert gain is not None or x.dtype == BF16
    tm, tn = _linear_tiles(M, K, N, x.dtype.itemsize, jnp.dtype(out_dtype).itemsize,
                           gain is not None, residual is not None, rope is not None)
    assert rope is None or tn % (2 * HEAD_DIM) == 0
    in_specs = [pl.BlockSpec((tm, K), lambda i, j: (i, x_col)),
                pl.BlockSpec((K, tn), lambda i, j: (0, j))]
    args = [x, w]
    if gain is not None:
        in_specs.append(pl.BlockSpec((1, K), lambda i, j: (0, 0)))
        args.append(gain.reshape(1, K).astype(F32))
    if residual is not None:
        in_specs.append(pl.BlockSpec((tm, tn), lambda i, j: (i, j)))
        args.append(residual)
    if rope is not None:
        in_specs += [pl.BlockSpec((tm, HEAD_DIM), lambda i, j: (i, 0))] * 2
        args += list(rope)
    return pl.pallas_call(
        functools.partial(_linear_kernel, has_gain=gain is not None, has_res=residual is not None,
                          has_rope=rope is not None),
        grid=(M // tm, N // tn),
        in_specs=in_specs,
        out_specs=pl.BlockSpec((tm, tn), lambda i, j: (i, j)),
        out_shape=jax.ShapeDtypeStruct((M, N), out_dtype),
        scratch_shapes=[pltpu.VMEM((tm, K), BF16)] if gain is not None else [],
        compiler_params=_params("parallel", "arbitrary"),
        name=name,
    )(*args)


def _mlp_kernel(*refs, has_out_norm):
    if has_out_norm:
        x_ref, g_ref, w1_ref, w2_ref, go_ref, o_ref, xn_ref = refs
    else:
        x_ref, g_ref, w1_ref, w2_ref, o_ref, xn_ref = refs
    f = pl.program_id(1)

    @pl.when(f == 0)
    def _():
        x = x_ref[...]
        ms = jnp.mean(x * x, axis=-1, keepdims=True)
        xn_ref[...] = (x * lax.rsqrt(ms + RMS_EPS) * g_ref[...]).astype(BF16)
        o_ref[...] = x

    u = _dot(xn_ref[...], w1_ref[...])
    r = jnp.square(jnp.maximum(u, 0.0)).astype(BF16)
    o_ref[...] += _dot(r, w2_ref[...])

    if has_out_norm:
        @pl.when(f == pl.num_programs(1) - 1)
        def _():
            y = o_ref[...]
            ms = jnp.mean(y * y, axis=-1, keepdims=True)
            o_ref[...] = y * lax.rsqrt(ms + RMS_EPS) * go_ref[...]


def _mlp(h, gain, w1, w2, layer, *, out_gain=None, tm=512, tf=1024):
    M, D = h.shape
    F = w1.shape[2]
    assert M % tm == 0 and F % tf == 0
    vec = pl.BlockSpec((1, D), lambda i, f: (0, 0))
    in_specs = [pl.BlockSpec((tm, D), lambda i, f: (i, 0)), vec,
                pl.BlockSpec((None, D, tf), lambda i, f: (layer, 0, f)),
                pl.BlockSpec((None, tf, D), lambda i, f: (layer, f, 0))]
    args = [h, gain.reshape(1, D).astype(F32), w1, w2]
    if out_gain is not None:
        in_specs.append(vec)
        args.append(out_gain.reshape(1, D).astype(F32))
    return pl.pallas_call(
        functools.partial(_mlp_kernel, has_out_norm=out_gain is not None),
        grid=(M // tm, F // tf),
        in_specs=in_specs,
        out_specs=pl.BlockSpec((tm, D), lambda i, f: (i, 0)),
        out_shape=jax.ShapeDtypeStruct((M, D), F32),
        scratch_shapes=[pltpu.VMEM((tm, D), BF16)],
        compiler_params=_params("parallel", "arbitrary"),
        name="mlp",
    )(*args)


SB_SKIP = 90.0


def _sb_kernel(q_ref, k_ref, v_ref, o_ref, *, t, n_sub):
    i = pl.program_id(2)
    row = lax.broadcasted_iota(jnp.int32, (t, t), 0)
    col = lax.broadcasted_iota(jnp.int32, (t, t), 1)
    later = (row > col).astype(BF16)
    causal = col < row

    def logits(q, j):
        return _dot_nt(q, k_ref[0, pl.ds(pl.multiple_of(j * t, t), t), :])

    def log_gates(z, diagonal):
        nz = -z
        lom = jnp.minimum(nz, 0.0) - jnp.log2(1.0 + jnp.exp2(jnp.minimum(z, nz)))
        log_beta = lom + z
        if diagonal:
            lom = jnp.where(causal, lom, 0.0)
        return log_beta, lom

    def later_sums(lom):
        return _dot(lom.astype(BF16), later), jnp.sum(lom, axis=1, keepdims=True)

    def gates(q, j, diagonal):
        log_beta, lom = log_gates(logits(q, j), diagonal)
        return (log_beta,) + later_sums(lom)

    def weigh(j, log_beta, after, diagonal):
        a = jnp.exp2(log_beta + after)
        if diagonal:
            a = jnp.where(causal, a, 0.0)
        return _dot(a.astype(BF16), v_ref[0, pl.ds(pl.multiple_of(j * t, t), t), :])

    skip = -SB_SKIP * LOG2E
    tiles = []
    for a in range(n_sub):
        jd = i * n_sub + a
        tiles.append((slice(a * t, (a + 1) * t), q_ref[0, a * t:(a + 1) * t, :], jd, jnp.maximum(jd - 1, 0)))
    z_d = [logits(q, jd) for _, q, jd, _ in tiles]
    z_p = [logits(q, jp) for _, q, _, jp in tiles]
    g_d = [log_gates(z, True) for z in z_d]
    g_p = [log_gates(z, False) for z in z_p]
    s_d = [later_sums(lom) for _, lom in g_d]
    s_p = [later_sums(lom) for _, lom in g_p]
    near = []
    pending = jnp.full((t, 1), skip, F32)
    for (rows, q, jd, jp), (lb_d, _), (lb_p, _), (within_d, carry), (within_p, total_p) in zip(tiles, g_d, g_p, s_d, s_p):
        has_prev = jd > 0
        acc = weigh(jd, lb_d, within_d, True) + jnp.where(has_prev, weigh(jp, lb_p, within_p + carry, False), 0.0)
        carry = carry + jnp.where(has_prev, total_p, 0.0)
        o_ref[0, rows, :] = acc.astype(o_ref.dtype)
        near.append((rows, q, jd, carry, acc))
        pending = jnp.maximum(pending, jnp.where(jd >= 2, carry, skip))

    def cond(s):
        j, carry, _ = s
        return jnp.logical_and(j >= 0, jnp.max(carry) > skip)

    @pl.when(jnp.max(pending) > skip)
    def _():
        for rows, q, jd, carry, acc in near:
            def body(s, q=q):
                j, carry, acc = s
                log_beta, within, total = gates(q, j, False)
                return j - 1, carry + total, acc + weigh(j, log_beta, within + carry, False)

            _, _, acc = lax.while_loop(cond, body, (jd - 2, carry, acc))
            o_ref[0, rows, :] = acc.astype(o_ref.dtype)


def _sb_attention(qkv, n_heads, *, t=256, n_sub=4):
    B, S, _ = qkv.shape
    d = HEAD_DIM
    tile = t * n_sub
    assert S % tile == 0
    return pl.pallas_call(
        functools.partial(_sb_kernel, t=t, n_sub=n_sub),
        grid=(B, n_heads, S // tile),
        in_specs=[pl.BlockSpec((1, tile, d), lambda b, h, i: (b, i, h)),
                  pl.BlockSpec((1, S, d), lambda b, h, i: (b, 0, n_heads + h)),
                  pl.BlockSpec((1, S, d), lambda b, h, i: (b, 0, 2 * n_heads + h))],
        out_specs=pl.BlockSpec((1, tile, d), lambda b, h, i: (b, i, h)),
        out_shape=jax.ShapeDtypeStruct((B, S, n_heads * d), BF16),
        compiler_params=_params("parallel", "parallel", "arbitrary"),
        name="sb_attn",
    )(qkv, qkv, qkv)


HG_CHUNK = 64
HG_SUB = 16
HG_GROUP = 4
HG_SPAN = 256
HG_SAFE = 70.0


def _hg_kernel(q_ref, fz_ref, v_ref, g_ref, lb_ref, gn_ref, tril_ref, o_ref, state_ref, oc_ref, b_ref, k_ref,
               *, n_chunks):
    C, U, d = HG_CHUNK, HG_SUB, HEAD_DIM

    @pl.when(pl.program_id(2) == 0)
    def _():
        state_ref[...] = jnp.zeros_like(state_ref)

    gn = gn_ref[...]
    row = lax.broadcasted_iota(jnp.int32, (C, C), 0)
    col = lax.broadcasted_iota(jnp.int32, (C, C), 1)
    lower = col <= row
    sub_row = lax.broadcasted_iota(jnp.int32, (U, 1), 0)
    tril = tril_ref[...]

    heads = []
    for hd in range(HG_GROUP):
        lanes = slice(hd * d, (hd + 1) * d)
        lb = lb_ref[:, lanes]
        fz = fz_ref[0, :, lanes]
        e = jnp.exp(-jnp.abs(fz))
        inv = 1.0 / (1.0 + e)
        sig_pos = jnp.where(fz >= 0, inv, e * inv)
        sig_neg = jnp.where(fz >= 0, e * inv, inv)
        log_f = jnp.log(lb + (1.0 - lb) * sig_pos)
        k = (1.0 - lb) * sig_neg
        l_hi = log_f.astype(BF16)
        l_lo = (log_f - l_hi.astype(F32)).astype(BF16)
        span = tril.shape[0]
        b = jnp.concatenate([_dot(tril, l_hi[r:r + span]) + _dot(tril, l_lo[r:r + span])
                             for r in range(0, fz.shape[0], span)], axis=0)
        heads.append((lanes, k, b))

    def chunk_row(b, r):
        picked = b.reshape(n_chunks, C, d)[:, r:r + 1, :]
        return jnp.broadcast_to(picked, (n_chunks, C, d)).reshape(b.shape)

    mids = [b - chunk_row(b, C // 2 - 1) for _, _, b in heads]
    safe = functools.reduce(jnp.maximum, [jnp.max(jnp.abs(rel)) for rel in mids]) < HG_SAFE

    def epilogue(o, g):
        o = o * lax.rsqrt(jnp.mean(o * o, axis=-1, keepdims=True) + RMS_EPS) * gn
        return (o * (g / (1.0 + jnp.exp(-g)))).astype(o_ref.dtype)

    @pl.when(safe)
    def _():
        prep = []
        for (lanes, k, b), rel in zip(heads, mids):
            q = q_ref[0, :, lanes]
            v16 = v_ref[0, :, lanes].astype(BF16)
            q_in = (q * jnp.exp(rel)).astype(BF16)
            k_in = (k * jnp.exp(-rel)).astype(BF16)
            q_st = (q * jnp.exp(b)).astype(BF16)
            b_last = b.reshape(n_chunks, C, d)[:, C - 1:C, :]
            kd = (k * jnp.exp(chunk_row(b, C - 1) - b)).astype(BF16)
            prep.append((q_in, k_in, q_st, v16, kd, jnp.exp(b_last)))
        states = [state_ref[hd] for hd in range(HG_GROUP)]
        for c in range(n_chunks):
            rows = slice(c * C, (c + 1) * C)
            for hd, (q_in, k_in, q_st, v16, kd, decay) in enumerate(prep):
                s = jnp.where(lower, _dot_nt(q_in[rows], k_in[rows]), 0.0)
                oc_ref[hd, rows, :] = (_dot(s.astype(BF16), v16[rows])
                                       + _dot_nt(q_st[rows], states[hd].astype(BF16)))
                states[hd] = states[hd] * decay[c] + _dot_tn(v16[rows], kd[rows])
        for hd, (lanes, _, _) in enumerate(heads):
            state_ref[hd] = states[hd]
            o_ref[0, :, lanes] = epilogue(oc_ref[hd], g_ref[0, :, lanes])

    def head_chunk(hd, r0):
        lanes = slice(hd * d, (hd + 1) * d)
        q = q_ref[0, pl.ds(r0, C), lanes]
        v = v_ref[0, pl.ds(r0, C), lanes]
        b = b_ref[hd, pl.ds(r0, C), :]
        k = k_ref[hd, pl.ds(r0, C), :]
        log2_k = jnp.log2(k)
        state_t = state_ref[hd]
        rows = pl.ds(r0, C)
        oc_ref[hd, rows, :] = _dot_nt((q * jnp.exp(b)).astype(BF16), state_t.astype(BF16))
        v16 = v.astype(BF16)
        for J in range(C // U - 1):
            lo_r, hi_r = J * U, (J + 1) * U
            ref_b = b[hi_r - 1:hi_r, :]
            kt = (k[lo_r:hi_r] * jnp.exp(ref_b - b[lo_r:hi_r])).astype(BF16)
            qs = (q[hi_r:] * jnp.exp(b[hi_r:] - ref_b)).astype(BF16)
            s = _dot_nt(qs, kt)
            oc_ref[hd, pl.ds(r0 + hi_r, C - hi_r), :] += _dot(s.astype(BF16), v16[lo_r:hi_r])
        b2 = b * LOG2E
        key2 = log2_k - b2
        for J in range(C // U):
            lo_r, hi_r = J * U, (J + 1) * U
            qj, bj, kj, vj = q[lo_r:hi_r], b2[lo_r:hi_r], key2[lo_r:hi_r], v[lo_r:hi_r]
            acc = jnp.zeros((U, d), F32)
            for s_i in range(U):
                w = qj * jnp.exp2(jnp.minimum(bj + kj[s_i:s_i + 1], 0.0))
                sc = jnp.sum(w, axis=1, keepdims=True)
                sc = jnp.where(sub_row >= s_i, sc, 0.0)
                acc = acc + sc * vj[s_i:s_i + 1]
            oc_ref[hd, pl.ds(r0 + lo_r, U), :] += acc
        b_last = b[C - 1:C, :]
        kd = (k * jnp.exp(b_last - b)).astype(BF16)
        state_ref[hd] = state_t * jnp.exp(b_last) + _dot_tn(v16, kd)
        o_ref[0, rows, lanes] = epilogue(oc_ref[hd, rows, :], g_ref[0, rows, lanes])

    @pl.when(jnp.logical_not(safe))
    def _():
        for hd, (_, k, b) in enumerate(heads):
            b_ref[hd] = b
            k_ref[hd] = k

        def chunk(c, _):
            r0 = pl.multiple_of(c * C, C)
            for hd in range(HG_GROUP):
                head_chunk(hd, r0)
            return 0

        lax.fori_loop(0, n_chunks, chunk, 0)


def _hgrn2(proj, lb, g_norm, n_heads, *, ts=512):
    B, S, _ = proj.shape
    d, G = HEAD_DIM, HG_GROUP
    assert S % ts == 0 and ts % HG_CHUNK == 0 and n_heads % G == 0
    n_groups = n_heads // G

    def col(part):
        return pl.BlockSpec((1, ts, G * d), lambda b, h, s: (b, s, part * n_groups + h))

    span = min(ts, HG_SPAN)
    r = jnp.arange(span)
    tril = ((r[None, :] <= r[:, None]) & (r[None, :] // HG_CHUNK == r[:, None] // HG_CHUNK)).astype(BF16)
    return pl.pallas_call(
        functools.partial(_hg_kernel, n_chunks=ts // HG_CHUNK),
        grid=(B, n_groups, S // ts),
        in_specs=[col(0), col(1), col(2), col(3),
                  pl.BlockSpec((1, G * d), lambda b, h, s: (0, h)),
                  pl.BlockSpec((1, d), lambda b, h, s: (0, 0)),
                  pl.BlockSpec((span, span), lambda b, h, s: (0, 0))],
        out_specs=pl.BlockSpec((1, ts, G * d), lambda b, h, s: (b, s, h)),
        out_shape=jax.ShapeDtypeStruct((B, S, n_heads * d), BF16),
        scratch_shapes=[pltpu.VMEM((G, d, d), F32), pltpu.VMEM((G, ts, d), F32),
                        pltpu.VMEM((G, ts, d), F32), pltpu.VMEM((G, ts, d), F32)],
        compiler_params=_params("parallel", "parallel", "arbitrary"),
        name="hgrn2",
    )(proj, proj, proj, proj, lb.reshape(1, n_heads * d), g_norm.reshape(1, d), tril)


def _rope_kernel(pos_ref, invf_ref, x_ref, kpe_ref, cos_ref, sin_ref):
    half = MLA_ROPE // 2
    ang = pos_ref[...].astype(F32) * invf_ref[...]
    lane = lax.broadcasted_iota(jnp.int32, ang.shape, 1)
    c = jnp.where(lane < MLA_ROPE, jnp.cos(ang), 0.0)
    s = jnp.sin(ang)
    s = jnp.where(lane < half, -s, jnp.where(lane < MLA_ROPE, s, 0.0))
    cos_ref[...] = c
    sin_ref[...] = s
    kpe_ref[...] = _rotate_half(x_ref[...], c, s).astype(kpe_ref.dtype)


def _rope_tables(positions, dkv, kpe_col, *, tm=512):
    M = dkv.shape[0]
    half = MLA_ROPE // 2
    freq = ROPE_THETA ** (-jnp.arange(0, MLA_ROPE, 2, dtype=F32) / MLA_ROPE)
    invf = jnp.concatenate([freq, freq, jnp.zeros((HEAD_DIM - 2 * half,), F32)]).reshape(1, HEAD_DIM)
    row = pl.BlockSpec((tm, HEAD_DIM), lambda i: (i, 0))
    return pl.pallas_call(
        _rope_kernel,
        grid=(M // tm,),
        in_specs=[pl.BlockSpec((tm, 1), lambda i: (i, 0)),
                  pl.BlockSpec((1, HEAD_DIM), lambda i: (0, 0)),
                  pl.BlockSpec((tm, HEAD_DIM), lambda i: (i, kpe_col))],
        out_specs=[row, row, row],
        out_shape=[jax.ShapeDtypeStruct((M, HEAD_DIM), BF16),
                   jax.ShapeDtypeStruct((M, HEAD_DIM), F32),
                   jax.ShapeDtypeStruct((M, HEAD_DIM), F32)],
        compiler_params=_params("parallel"),
        name="mla_rope",
    )(positions.reshape(M, 1), invf, dkv)


MLA_ROW_CHUNK = 256


def _mla_kernel(q_ref, kn_ref, v_ref, kpe_ref, o_ref, kcat_ref, v1_ref, sa_ref, sb_ref, m_ref, acc_ref, *, t):
    n_q = q_ref.shape[1] // t
    n_pairs = n_q * (n_q + 1) // 2
    for n in range(n_q):
        rows = slice(n * t, (n + 1) * t)
        kcat_ref[:MLA_NOPE, rows] = kn_ref[0, rows, :].T
        kcat_ref[MLA_NOPE:, rows] = kpe_ref[0, rows, :].T
    rc = MLA_ROW_CHUNK
    row = lax.broadcasted_iota(jnp.int32, (rc, t), 0) // MLA_CHUNK
    col = lax.broadcasted_iota(jnp.int32, (rc, t), 1) // MLA_CHUNK

    def tile(n):
        return pl.ds(pl.multiple_of(n * t, t), t)

    def scores(i, j, dst_ref):
        dst_ref[...] = _dot(q_ref[0, tile(i), :], kcat_ref[:, tile(j)])

    v1_ref[:, :HEAD_DIM] = v_ref[0]
    v1_ref[:, HEAD_DIM:] = jnp.ones((v1_ref.shape[0], HEAD_DIM), BF16)

    def reset():
        m_ref[...] = jnp.full_like(m_ref, -jnp.inf)
        acc_ref[...] = jnp.zeros_like(acc_ref)

    def absorb(j, src_ref, diagonal):
        vb = v1_ref[tile(j), :]
        for r in range(t // rc):
            rows = slice(r * rc, (r + 1) * rc)
            s = src_ref[rows, :]
            if diagonal:
                s = jnp.where(col <= row + (r * rc) // MLA_CHUNK, s, -jnp.inf)
            m_old = m_ref[rows, :]
            lane_groups = [s[:, c:c + HEAD_DIM] for c in range(0, t, HEAD_DIM)]
            m_col = jnp.max(functools.reduce(jnp.maximum, lane_groups + [m_old]), axis=1, keepdims=True)
            m_new = jnp.broadcast_to(m_col, m_old.shape)
            alpha = jnp.exp2(m_old - m_new)
            p = jnp.exp2((s - m_col).astype(BF16))
            acc_ref[rows, :] = jnp.concatenate([alpha, alpha], axis=1) * acc_ref[rows, :] + _dot(p, vb)
            m_ref[rows, :] = m_new

    def successor(i, j):
        last = j == i
        return jnp.where(last, i + 1, i), jnp.where(last, 0, j + 1)

    def step(cur_ref, nxt_ref, i, j):
        ni, nj = successor(i, j)
        ni = jnp.minimum(ni, n_q - 1)

        @pl.when(j < i)
        def _():
            scores(ni, nj, nxt_ref)
            absorb(j, cur_ref, False)

        @pl.when(j == i)
        def _():
            scores(ni, nj, nxt_ref)
            absorb(j, cur_ref, True)
            o_ref[0, tile(i), :] = (acc_ref[:, :HEAD_DIM] / acc_ref[:, HEAD_DIM:]).astype(o_ref.dtype)
            reset()

    def two_steps(_, ij):
        i, j = ij
        i1, j1 = successor(i, j)

        @pl.when(j + 1 < i)
        def _():
            scores(i, j + 1, sb_ref)
            absorb(j, sa_ref, False)
            scores(i, j + 2, sa_ref)
            absorb(j + 1, sb_ref, False)

        @pl.when(j + 1 >= i)
        def _():
            step(sa_ref, sb_ref, i, j)
            step(sb_ref, sa_ref, i1, j1)

        return successor(i1, j1)

    reset()
    scores(0, 0, sa_ref)
    ij = lax.fori_loop(0, n_pairs // 2, two_steps, (jnp.int32(0), jnp.int32(0)))
    if n_pairs % 2:
        step(sa_ref, sb_ref, *ij)


def _mla_attention(q, kv, kpe, n_heads, *, t=512):
    B, S, _ = q.shape
    d = HEAD_DIM
    assert S % t == 0 and t % MLA_CHUNK == 0
    return pl.pallas_call(
        functools.partial(_mla_kernel, t=t),
        grid=(B, n_heads),
        in_specs=[pl.BlockSpec((1, S, 2 * d), lambda b, h: (b, 0, h)),
                  pl.BlockSpec((1, S, d), lambda b, h: (b, 0, 2 * h)),
                  pl.BlockSpec((1, S, d), lambda b, h: (b, 0, 2 * h + 1)),
                  pl.BlockSpec((1, S, d), lambda b, h: (b, 0, 0))],
        out_specs=pl.BlockSpec((1, S, d), lambda b, h: (b, 0, h)),
        out_shape=jax.ShapeDtypeStruct((B, S, n_heads * d), BF16),
        scratch_shapes=[pltpu.VMEM((2 * d, S), BF16), pltpu.VMEM((S, 2 * d), BF16),
                        pltpu.VMEM((t, t), F32), pltpu.VMEM((t, t), F32),
                        pltpu.VMEM((t, d), F32), pltpu.VMEM((t, 2 * d), F32)],
        compiler_params=_params("parallel", "parallel"),
        name="mla_attn",
    )(q, kv, kv, kpe)


def _sb_layer(h, gain, w_qkv, w_o, B, S):
    M, D = h.shape
    H = w_o.shape[0] // HEAD_DIM
    col_scale = jnp.where(jnp.arange(w_qkv.shape[1]) < H * HEAD_DIM, HEAD_DIM ** -0.5 * LOG2E, 1.0)
    qkv = _linear(h, (w_qkv * col_scale).astype(BF16), name="sb_qkv", gain=gain, out_dtype=BF16)
    o = _sb_attention(qkv.reshape(B, S, -1), H)
    return _linear(o.reshape(M, -1), w_o.astype(BF16), name="sb_out", residual=h)


def _hg_layer(h, gain, w_in, g_norm, w_o, lb, B, S):
    M, D = h.shape
    H = w_o.shape[0] // HEAD_DIM
    proj = _linear(h, w_in.astype(BF16), name="hg_in", gain=gain, out_dtype=F32)
    o = _hgrn2(proj.reshape(B, S, -1), lb, g_norm, H)
    return _linear(o.reshape(M, -1), w_o.astype(BF16), name="hg_out", residual=h)


def _mla_layer(h, positions, gain, w_dkv, q_norm, kv_norm, w_uq, w_ukv, w_o, B, S):
    M, D = h.shape
    H = w_o.shape[0] // MLA_NOPE
    qr, kvr = MLA_Q_RANK, MLA_KV_RANK
    pad_a = jnp.zeros((D, qr - kvr), F32)
    pad_b = jnp.zeros((D, HEAD_DIM - MLA_ROPE), F32)
    w_d = jnp.concatenate([w_dkv[:, qr:qr + kvr], pad_a, w_dkv[:, :qr], w_dkv[:, qr + kvr:], pad_b], axis=1)
    dkv = _linear(h, w_d.astype(BF16), name="mla_down", gain=gain, out_dtype=F32)
    kpe, cos, sin = _rope_tables(positions, dkv, (2 * qr) // HEAD_DIM)
    w_q = w_uq.reshape(qr, H, MLA_NOPE + MLA_ROPE) * ((MLA_NOPE + MLA_ROPE) ** -0.5 * LOG2E)
    w_q = jnp.concatenate([w_q, jnp.zeros((qr, H, HEAD_DIM - MLA_ROPE), F32)], axis=2).reshape(qr, H * 2 * HEAD_DIM)
    q = _linear(dkv, w_q.astype(BF16), name="mla_q", gain=q_norm, rope=(cos, sin), out_dtype=BF16, x_col=1)
    kv = _linear(dkv, w_ukv.astype(BF16), name="mla_kv", gain=kv_norm, out_dtype=BF16, x_col=0)
    o = _mla_attention(q.reshape(B, S, -1), kv.reshape(B, S, -1), kpe.reshape(B, S, -1), H)
    return _linear(o.reshape(M, -1), w_o.astype(BF16), name="mla_out", residual=h)


def kernel(x, positions, norm_mix, norm_mlp, final_norm, sb_w_qkv, sb_w_o, hg_w_in, hg_lb_logits, hg_g_norm, hg_w_o, mla_w_dkv, mla_q_norm, mla_kv_norm, mla_w_uq, mla_w_ukv, mla_w_o, mlp_w1, mlp_w2):
    B, S, D = x.shape
    depth = norm_mix.shape[0]
    assert depth >= 1
    p_lb = jax.nn.softmax(hg_lb_logits.astype(F32), axis=0)
    lb_all = jnp.cumsum(p_lb, axis=0) - p_lb[0]
    h = x.reshape(B * S, D)
    w1, w2 = mlp_w1.astype(BF16), mlp_w2.astype(BF16)
    for i in range(depth):
        m, j = i % N_MIXERS, i // N_MIXERS
        if m == 0:
            h = _sb_layer(h, norm_mix[i], sb_w_qkv[j], sb_w_o[j], B, S)
        elif m == 1:
            h = _hg_layer(h, norm_mix[i], hg_w_in[j], hg_g_norm[j], hg_w_o[j], lb_all[i], B, S)
        else:
            h = _mla_layer(h, positions, norm_mix[i], mla_w_dkv[j], mla_q_norm[j], mla_kv_norm[j],
                           mla_w_uq[j], mla_w_ukv[j], mla_w_o[j], B, S)
        h = _mlp(h, norm_mlp[i], w1, w2, i, out_gain=final_norm if i == depth - 1 else None)
    return h.reshape(B, S, D)
```

```python
import functools

import jax
import jax.numpy as jnp
from jax import lax
from jax.experimental import pallas as pl
from jax.experimental.pallas import tpu as pltpu

F32 = jnp.float32
BF16 = jnp.bfloat16

RMS_EPS = 1e-6
N_MIXERS = 3
HEAD_DIM = 128
MLA_NOPE = 128
MLA_ROPE = 64
MLA_Q_RANK = 768
MLA_KV_RANK = 512
MLA_CHUNK = 64
ROPE_THETA = 10000.0
LOG2E = 1.4426950408889634
VMEM_LIMIT = 56 * 1024 * 1024


def _params(*sem):
    return pltpu.CompilerParams(dimension_semantics=sem, vmem_limit_bytes=VMEM_LIMIT)


def _dot(a, b):
    return jnp.dot(a, b, preferred_element_type=F32)


def _dot_nt(a, b):
    return lax.dot_general(a, b, (((1,), (1,)), ((), ())), preferred_element_type=F32)


def _dot_tn(a, b):
    return lax.dot_general(a, b, (((0,), (0,)), ((), ())), preferred_element_type=F32)


def _split3(x):
    hi = x.astype(BF16)
    r1 = x - hi.astype(F32)
    mid = r1.astype(BF16)
    lo = (r1 - mid.astype(F32)).astype(BF16)
    return hi, mid, lo


def _rotate_half(x, cos, sin):
    half = MLA_ROPE // 2
    swapped = pltpu.roll(x, half, 1) + pltpu.roll(x, HEAD_DIM - half, 1)
    return x * cos + swapped * sin


def _linear_kernel(*refs, has_gain, has_res, has_rope):
    x_ref, w_ref = refs[0], refs[1]
    pos = 2
    g_ref = r_ref = cos_ref = sin_ref = None
    if has_gain:
        g_ref = refs[pos]
        pos += 1
    if has_res:
        r_ref = refs[pos]
        pos += 1
    if has_rope:
        cos_ref, sin_ref = refs[pos], refs[pos + 1]
        pos += 2
    o_ref = refs[pos]

    if has_gain:
        xn_ref = refs[pos + 1]

        @pl.when(pl.program_id(1) == 0)
        def _():
            x = x_ref[...].astype(F32)
            ms = jnp.mean(x * x, axis=-1, keepdims=True)
            xn_ref[...] = (x * lax.rsqrt(ms + RMS_EPS) * g_ref[...]).astype(BF16)

        lhs = xn_ref[...]
    else:
        lhs = x_ref[...]

    acc = _dot(lhs, w_ref[...])
    if has_res:
        acc = acc + r_ref[...]
    if has_rope:
        cos, sin = cos_ref[...], sin_ref[...]
        for c in range(0, acc.shape[1], 2 * HEAD_DIM):
            o_ref[:, c:c + HEAD_DIM] = acc[:, c:c + HEAD_DIM].astype(o_ref.dtype)
            rot = _rotate_half(acc[:, c + HEAD_DIM:c + 2 * HEAD_DIM], cos, sin)
            o_ref[:, c + HEAD_DIM:c + 2 * HEAD_DIM] = rot.astype(o_ref.dtype)
    else:
        o_ref[...] = acc.astype(o_ref.dtype)


LINEAR_VMEM_BUDGET = 44 * 1024 * 1024
RESIDENT_WEIGHT_BYTES = 8 * 1024 * 1024


def _linear_tiles(M, K, N, x_bytes, out_bytes, has_gain, has_res, has_rope):
    widths = [N] if K * N * 2 <= RESIDENT_WEIGHT_BYTES or N % 1024 else [w for w in (4096, 2048, 1024) if N % w == 0]
    for tm in (1024, 512, 256, 128):
        if M % tm:
            continue
        for tn in widths:
            need = 2 * tm * K * x_bytes + 2 * K * tn * 2 + 2 * tm * tn * out_bytes
            need += tm * K * 2 if has_gain else 0
            need += 2 * tm * tn * 4 if has_res else 0
            need += tm * tn * 4 + 4 * tm * HEAD_DIM * 4 if has_rope else 0
            if need <= LINEAR_VMEM_BUDGET:
                return tm, tn
    raise ValueError("no linear tiling fits VMEM")


def _linear(x, w, *, name, gain=None, residual=None, rope=None, out_dtype=F32, x_col=0):
    M = x.shape[0]
    K, N = w.shape
    assert gain is not None or x.dtype == BF16
    tm, tn = _linear_tiles(M, K, N, x.dtype.itemsize, jnp.dtype(out_dtype).itemsize,
                           gain is not None, residual is not None, rope is not None)
    assert rope is None or tn % (2 * HEAD_DIM) == 0
    in_specs = [pl.BlockSpec((tm, K), lambda i, j: (i, x_col)),
                pl.BlockSpec((K, tn), lambda i, j: (0, j))]
    args = [x, w]
    if gain is not None:
        in_specs.append(pl.BlockSpec((1, K), lambda i, j: (0, 0)))
        args.append(gain.reshape(1, K).astype(F32))
    if residual is not None:
        in_specs.append(pl.BlockSpec((tm, tn), lambda i, j: (i, j)))
        args.append(residual)
    if rope is not None:
        in_specs += [pl.BlockSpec((tm, HEAD_DIM), lambda i, j: (i, 0))] * 2
        args += list(rope)
    return pl.pallas_call(
        functools.partial(_linear_kernel, has_gain=gain is not None, has_res=residual is not None,
                          has_rope=rope is not None),
        grid=(M // tm, N // tn),
        in_specs=in_specs,
        out_specs=pl.BlockSpec((tm, tn), lambda i, j: (i, j)),
        out_shape=jax.ShapeDtypeStruct((M, N), out_dtype),
        scratch_shapes=[pltpu.VMEM((tm, K), BF16)] if gain is not None else [],
        compiler_params=_params("parallel", "arbitrary"),
        name=name,
    )(*args)


def _mlp_kernel(*refs, has_out_norm):
    if has_out_norm:
        x_ref, g_ref, w1_ref, w2_ref, go_ref, o_ref, xn_ref = refs
    else:
        x_ref, g_ref, w1_ref, w2_ref, o_ref, xn_ref = refs
    f = pl.program_id(1)

    @pl.when(f == 0)
    def _():
        x = x_ref[...]
        ms = jnp.mean(x * x, axis=-1, keepdims=True)
        xn_ref[...] = (x * lax.rsqrt(ms + RMS_EPS) * g_ref[...]).astype(BF16)
        o_ref[...] = x

    u = _dot(xn_ref[...], w1_ref[...])
    r = jnp.square(jnp.maximum(u, 0.0)).astype(BF16)
    o_ref[...] += _dot(r, w2_ref[...])

    if has_out_norm:
        @pl.when(f == pl.num_programs(1) - 1)
        def _():
            y = o_ref[...]
            ms = jnp.mean(y * y, axis=-1, keepdims=True)
            o_ref[...] = y * lax.rsqrt(ms + RMS_EPS) * go_ref[...]


def _mlp(h, gain, w1, w2, layer, *, out_gain=None, tm=512, tf=1024):
    M, D = h.shape
    F = w1.shape[2]
    assert M % tm == 0 and F % tf == 0
    vec = pl.BlockSpec((1, D), lambda i, f: (0, 0))
    in_specs = [pl.BlockSpec((tm, D), lambda i, f: (i, 0)), vec,
                pl.BlockSpec((None, D, tf), lambda i, f: (layer, 0, f)),
                pl.BlockSpec((None, tf, D), lambda i, f: (layer, f, 0))]
    args = [h, gain.reshape(1, D).astype(F32), w1, w2]
    if out_gain is not None:
        in_specs.append(vec)
        args.append(out_gain.reshape(1, D).astype(F32))
    return pl.pallas_call(
        functools.partial(_mlp_kernel, has_out_norm=out_gain is not None),
        grid=(M // tm, F // tf),
        in_specs=in_specs,
        out_specs=pl.BlockSpec((tm, D), lambda i, f: (i, 0)),
        out_shape=jax.ShapeDtypeStruct((M, D), F32),
        scratch_shapes=[pltpu.VMEM((tm, D), BF16)],
        compiler_params=_params("parallel", "arbitrary"),
        name="mlp",
    )(*args)


SB_SKIP = 90.0


def _sb_kernel(q_ref, k_ref, v_ref, o_ref, *, t, n_sub):
    i = pl.program_id(2)
    row = lax.broadcasted_iota(jnp.int32, (t, t), 0)
    col = lax.broadcasted_iota(jnp.int32, (t, t), 1)
    later = (row > col).astype(BF16)
    causal = col < row

    def logits(q, j):
        return _dot_nt(q, k_ref[0, pl.ds(pl.multiple_of(j * t, t), t), :])

    def log_gates(z, diagonal):
        nz = -z
        lom = jnp.minimum(nz, 0.0) - jnp.log2(1.0 + jnp.exp2(jnp.minimum(z, nz)))
        log_beta = lom + z
        if diagonal:
            lom = jnp.where(causal, lom, 0.0)
        return log_beta, lom

    def later_sums(lom):
        return _dot(lom.astype(BF16), later), jnp.sum(lom, axis=1, keepdims=True)

    def gates(q, j, diagonal):
        log_beta, lom = log_gates(logits(q, j), diagonal)
        return (log_beta,) + later_sums(lom)

    def weigh(j, log_beta, after, diagonal):
        a = jnp.exp2(log_beta + after)
        if diagonal:
            a = jnp.where(causal, a, 0.0)
        return _dot(a.astype(BF16), v_ref[0, pl.ds(pl.multiple_of(j * t, t), t), :])

    skip = -SB_SKIP * LOG2E
    tiles = []
    for a in range(n_sub):
        jd = i * n_sub + a
        tiles.append((slice(a * t, (a + 1) * t), q_ref[0, a * t:(a + 1) * t, :], jd, jnp.maximum(jd - 1, 0)))
    z_d = [logits(q, jd) for _, q, jd, _ in tiles]
    z_p = [logits(q, jp) for _, q, _, jp in tiles]
    g_d = [log_gates(z, True) for z in z_d]
    g_p = [log_gates(z, False) for z in z_p]
    s_d = [later_sums(lom) for _, lom in g_d]
    s_p = [later_sums(lom) for _, lom in g_p]
    near = []
    pending = jnp.full((t, 1), skip, F32)
    for (rows, q, jd, jp), (lb_d, _), (lb_p, _), (within_d, carry), (within_p, total_p) in zip(tiles, g_d, g_p, s_d, s_p):
        has_prev = jd > 0
        acc = weigh(jd, lb_d, within_d, True) + jnp.where(has_prev, weigh(jp, lb_p, within_p + carry, False), 0.0)
        carry = carry + jnp.where(has_prev, total_p, 0.0)
        o_ref[0, rows, :] = acc.astype(o_ref.dtype)
        near.append((rows, q, jd, carry, acc))
        pending = jnp.maximum(pending, jnp.where(jd >= 2, carry, skip))

    def cond(s):
        j, carry, _ = s
        return jnp.logical_and(j >= 0, jnp.max(carry) > skip)

    @pl.when(jnp.max(pending) > skip)
    def _():
        for rows, q, jd, carry, acc in near:
            def body(s, q=q):
                j, carry, acc = s
                log_beta, within, total = gates(q, j, False)
                return j - 1, carry + total, acc + weigh(j, log_beta, within + carry, False)

            _, _, acc = lax.while_loop(cond, body, (jd - 2, carry, acc))
            o_ref[0, rows, :] = acc.astype(o_ref.dtype)


def _sb_attention(qkv, n_heads, *, t=256, n_sub=8):
    B, S, _ = qkv.shape
    d = HEAD_DIM
    tile = t * n_sub
    assert S % tile == 0
    return pl.pallas_call(
        functools.partial(_sb_kernel, t=t, n_sub=n_sub),
        grid=(B, n_heads, S // tile),
        in_specs=[pl.BlockSpec((1, tile, d), lambda b, h, i: (b, i, h)),
                  pl.BlockSpec((1, S, d), lambda b, h, i: (b, 0, n_heads + h)),
                  pl.BlockSpec((1, S, d), lambda b, h, i: (b, 0, 2 * n_heads + h))],
        out_specs=pl.BlockSpec((1, tile, d), lambda b, h, i: (b, i, h)),
        out_shape=jax.ShapeDtypeStruct((B, S, n_heads * d), BF16),
        compiler_params=_params("parallel", "parallel", "arbitrary"),
        name="sb_attn",
    )(qkv, qkv, qkv)


HG_CHUNK = 64
HG_SUB = 16
HG_GROUP = 4
HG_SPAN = 256
HG_SAFE = 70.0


def _hg_kernel(q_ref, fz_ref, v_ref, g_ref, lb_ref, gn_ref, tril_ref, o_ref, state_ref, oc_ref, b_ref, k_ref,
               *, n_chunks):
    C, U, d = HG_CHUNK, HG_SUB, HEAD_DIM

    @pl.when(pl.program_id(2) == 0)
    def _():
        state_ref[...] = jnp.zeros_like(state_ref)

    gn = gn_ref[...]
    row = lax.broadcasted_iota(jnp.int32, (C, C), 0)
    col = lax.broadcasted_iota(jnp.int32, (C, C), 1)
    lower = col <= row
    sub_row = lax.broadcasted_iota(jnp.int32, (U, 1), 0)
    tril = tril_ref[...]

    heads = []
    for hd in range(HG_GROUP):
        lanes = slice(hd * d, (hd + 1) * d)
        lb = lb_ref[:, lanes]
        fz = fz_ref[0, :, lanes]
        e = jnp.exp(-jnp.abs(fz))
        inv = 1.0 / (1.0 + e)
        sig_pos = jnp.where(fz >= 0, inv, e * inv)
        sig_neg = jnp.where(fz >= 0, e * inv, inv)
        log_f = jnp.log(lb + (1.0 - lb) * sig_pos)
        k = (1.0 - lb) * sig_neg
        l_hi = log_f.astype(BF16)
        l_lo = (log_f - l_hi.astype(F32)).astype(BF16)
        span = tril.shape[0]
        b = jnp.concatenate([_dot(tril, l_hi[r:r + span]) + _dot(tril, l_lo[r:r + span])
                             for r in range(0, fz.shape[0], span)], axis=0)
        heads.append((lanes, k, b))

    def chunk_row(b, r):
        picked = b.reshape(n_chunks, C, d)[:, r:r + 1, :]
        return jnp.broadcast_to(picked, (n_chunks, C, d)).reshape(b.shape)

    mids = [b - chunk_row(b, C // 2 - 1) for _, _, b in heads]
    safe = functools.reduce(jnp.maximum, [jnp.max(jnp.abs(rel)) for rel in mids]) < HG_SAFE

    def epilogue(o, g):
        o = o * lax.rsqrt(jnp.mean(o * o, axis=-1, keepdims=True) + RMS_EPS) * gn
        return (o * (g / (1.0 + jnp.exp(-g)))).astype(o_ref.dtype)

    @pl.when(safe)
    def _():
        prep = []
        for (lanes, k, b), rel in zip(heads, mids):
            q = q_ref[0, :, lanes]
            v16 = v_ref[0, :, lanes].astype(BF16)
            q_in = (q * jnp.exp(rel)).astype(BF16)
            k_in = (k * jnp.exp(-rel)).astype(BF16)
            q_st = (q * jnp.exp(b)).astype(BF16)
            b_last = b.reshape(n_chunks, C, d)[:, C - 1:C, :]
            kd = (k * jnp.exp(chunk_row(b, C - 1) - b)).astype(BF16)
            prep.append((q_in, k_in, q_st, v16, kd, jnp.exp(b_last)))
        states = [state_ref[hd] for hd in range(HG_GROUP)]
        for c in range(n_chunks):
            rows = slice(c * C, (c + 1) * C)
            for hd, (q_in, k_in, q_st, v16, kd, decay) in enumerate(prep):
                s = jnp.where(lower, _dot_nt(q_in[rows], k_in[rows]), 0.0)
                oc_ref[hd, rows, :] = (_dot(s.astype(BF16), v16[rows])
                                       + _dot_nt(q_st[rows], states[hd].astype(BF16)))
                states[hd] = states[hd] * decay[c] + _dot_tn(v16[rows], kd[rows])
        for hd, (lanes, _, _) in enumerate(heads):
            state_ref[hd] = states[hd]
            o_ref[0, :, lanes] = epilogue(oc_ref[hd], g_ref[0, :, lanes])

    def head_chunk(hd, r0):
        lanes = slice(hd * d, (hd + 1) * d)
        q = q_ref[0, pl.ds(r0, C), lanes]
        v = v_ref[0, pl.ds(r0, C), lanes]
        b = b_ref[hd, pl.ds(r0, C), :]
        k = k_ref[hd, pl.ds(r0, C), :]
        log2_k = jnp.log2(k)
        state_t = state_ref[hd]
        rows = pl.ds(r0, C)
        oc_ref[hd, rows, :] = _dot_nt((q * jnp.exp(b)).astype(BF16), state_t.astype(BF16))
        v16 = v.astype(BF16)
        for J in range(C // U - 1):
            lo_r, hi_r = J * U, (J + 1) * U
            ref_b = b[hi_r - 1:hi_r, :]
            kt = (k[lo_r:hi_r] * jnp.exp(ref_b - b[lo_r:hi_r])).astype(BF16)
            qs = (q[hi_r:] * jnp.exp(b[hi_r:] - ref_b)).astype(BF16)
            s = _dot_nt(qs, kt)
            oc_ref[hd, pl.ds(r0 + hi_r, C - hi_r), :] += _dot(s.astype(BF16), v16[lo_r:hi_r])
        b2 = b * LOG2E
        key2 = log2_k - b2
        for J in range(C // U):
            lo_r, hi_r = J * U, (J + 1) * U
            qj, bj, kj, vj = q[lo_r:hi_r], b2[lo_r:hi_r], key2[lo_r:hi_r], v[lo_r:hi_r]
            acc = jnp.zeros((U, d), F32)
            for s_i in range(U):
                w = qj * jnp.exp2(jnp.minimum(bj + kj[s_i:s_i + 1], 0.0))
                sc = jnp.sum(w, axis=1, keepdims=True)
                sc = jnp.where(sub_row >= s_i, sc, 0.0)
                acc = acc + sc * vj[s_i:s_i + 1]
            oc_ref[hd, pl.ds(r0 + lo_r, U), :] += acc
        b_last = b[C - 1:C, :]
        kd = (k * jnp.exp(b_last - b)).astype(BF16)
        state_ref[hd] = state_t * jnp.exp(b_last) + _dot_tn(v16, kd)
        o_ref[0, rows, lanes] = epilogue(oc_ref[hd, rows, :], g_ref[0, rows, lanes])

    @pl.when(jnp.logical_not(safe))
    def _():
        for hd, (_, k, b) in enumerate(heads):
            b_ref[hd] = b
            k_ref[hd] = k

        def chunk(c, _):
            r0 = pl.multiple_of(c * C, C)
            for hd in range(HG_GROUP):
                head_chunk(hd, r0)
            return 0

        lax.fori_loop(0, n_chunks, chunk, 0)


def _hgrn2(proj, lb, g_norm, n_heads, *, ts=512):
    B, S, _ = proj.shape
    d, G = HEAD_DIM, HG_GROUP
    assert S % ts == 0 and ts % HG_CHUNK == 0 and n_heads % G == 0
    n_groups = n_heads // G

    def col(part):
        return pl.BlockSpec((1, ts, G * d), lambda b, h, s: (b, s, part * n_groups + h))

    span = min(ts, HG_SPAN)
    r = jnp.arange(span)
    tril = ((r[None, :] <= r[:, None]) & (r[None, :] // HG_CHUNK == r[:, None] // HG_CHUNK)).astype(BF16)
    return pl.pallas_call(
        functools.partial(_hg_kernel, n_chunks=ts // HG_CHUNK),
        grid=(B, n_groups, S // ts),
        in_specs=[col(0), col(1), col(2), col(3),
                  pl.BlockSpec((1, G * d), lambda b, h, s: (0, h)),
                  pl.BlockSpec((1, d), lambda b, h, s: (0, 0)),
                  pl.BlockSpec((span, span), lambda b, h, s: (0, 0))],
        out_specs=pl.BlockSpec((1, ts, G * d), lambda b, h, s: (b, s, h)),
        out_shape=jax.ShapeDtypeStruct((B, S, n_heads * d), BF16),
        scratch_shapes=[pltpu.VMEM((G, d, d), F32), pltpu.VMEM((G, ts, d), F32),
                        pltpu.VMEM((G, ts, d), F32), pltpu.VMEM((G, ts, d), F32)],
        compiler_params=_params("parallel", "parallel", "arbitrary"),
        name="hgrn2",
    )(proj, proj, proj, proj, lb.reshape(1, n_heads * d), g_norm.reshape(1, d), tril)


def _rope_kernel(pos_ref, invf_ref, x_ref, kpe_ref, cos_ref, sin_ref):
    half = MLA_ROPE // 2
    ang = pos_ref[...].astype(F32) * invf_ref[...]
    lane = lax.broadcasted_iota(jnp.int32, ang.shape, 1)
    c = jnp.where(lane < MLA_ROPE, jnp.cos(ang), 0.0)
    s = jnp.sin(ang)
    s = jnp.where(lane < half, -s, jnp.where(lane < MLA_ROPE, s, 0.0))
    cos_ref[...] = c
    sin_ref[...] = s
    kpe_ref[...] = _rotate_half(x_ref[...], c, s).astype(kpe_ref.dtype)


def _rope_tables(positions, dkv, kpe_col, *, tm=512):
    M = dkv.shape[0]
    half = MLA_ROPE // 2
    freq = ROPE_THETA ** (-jnp.arange(0, MLA_ROPE, 2, dtype=F32) / MLA_ROPE)
    invf = jnp.concatenate([freq, freq, jnp.zeros((HEAD_DIM - 2 * half,), F32)]).reshape(1, HEAD_DIM)
    row = pl.BlockSpec((tm, HEAD_DIM), lambda i: (i, 0))
    return pl.pallas_call(
        _rope_kernel,
        grid=(M // tm,),
        in_specs=[pl.BlockSpec((tm, 1), lambda i: (i, 0)),
                  pl.BlockSpec((1, HEAD_DIM), lambda i: (0, 0)),
                  pl.BlockSpec((tm, HEAD_DIM), lambda i: (i, kpe_col))],
        out_specs=[row, row, row],
        out_shape=[jax.ShapeDtypeStruct((M, HEAD_DIM), BF16),
                   jax.ShapeDtypeStruct((M, HEAD_DIM), F32),
                   jax.ShapeDtypeStruct((M, HEAD_DIM), F32)],
        compiler_params=_params("parallel"),
        name="mla_rope",
    )(positions.reshape(M, 1), invf, dkv)


MLA_ROW_CHUNK = 256


def _mla_kernel(q_ref, kn_ref, v_ref, kpe_ref, o_ref, kcat_ref, v1_ref, sa_ref, sb_ref, m_ref, acc_ref, *, t):
    n_q = q_ref.shape[1] // t
    n_pairs = n_q * (n_q + 1) // 2
    for n in range(n_q):
        rows = slice(n * t, (n + 1) * t)
        kcat_ref[:MLA_NOPE, rows] = kn_ref[0, rows, :].T
        kcat_ref[MLA_NOPE:, rows] = kpe_ref[0, rows, :].T
    rc = MLA_ROW_CHUNK
    row = lax.broadcasted_iota(jnp.int32, (rc, t), 0) // MLA_CHUNK
    col = lax.broadcasted_iota(jnp.int32, (rc, t), 1) // MLA_CHUNK

    def tile(n):
        return pl.ds(pl.multiple_of(n * t, t), t)

    def scores(i, j, dst_ref):
        dst_ref[...] = _dot(q_ref[0, tile(i), :], kcat_ref[:, tile(j)])

    v1_ref[:, :HEAD_DIM] = v_ref[0]
    v1_ref[:, HEAD_DIM:] = jnp.ones((v1_ref.shape[0], HEAD_DIM), BF16)

    def reset():
        m_ref[...] = jnp.full_like(m_ref, -jnp.inf)
        acc_ref[...] = jnp.zeros_like(acc_ref)

    def absorb(j, src_ref, diagonal):
        vb = v1_ref[tile(j), :]
        for r in range(t // rc):
            rows = slice(r * rc, (r + 1) * rc)
            s = src_ref[rows, :]
            if diagonal:
                s = jnp.where(col <= row + (r * rc) // MLA_CHUNK, s, -jnp.inf)
            m_old = m_ref[rows, :]
            lane_groups = [s[:, c:c + HEAD_DIM] for c in range(0, t, HEAD_DIM)]
            m_col = jnp.max(functools.reduce(jnp.maximum, lane_groups + [m_old]), axis=1, keepdims=True)
            m_new = jnp.broadcast_to(m_col, m_old.shape)
            alpha = jnp.exp2(m_old - m_new)
            p = jnp.exp2((s - m_col).astype(BF16))
            acc_ref[rows, :] = jnp.concatenate([alpha, alpha], axis=1) * acc_ref[rows, :] + _dot(p, vb)
            m_ref[rows, :] = m_new

    def successor(i, j):
        last = j == i
        return jnp.where(last, i + 1, i), jnp.where(last, 0, j + 1)

    def plain(cur_ref, nxt_ref, i, j, nxt):
        scores(*nxt, nxt_ref)
        absorb(j, cur_ref, False)

    def closing(cur_ref, nxt_ref, i, nxt):
        scores(*nxt, nxt_ref)
        absorb(i, cur_ref, True)
        o_ref[0, tile(i), :] = (acc_ref[:, :HEAD_DIM] / acc_ref[:, HEAD_DIM:]).astype(o_ref.dtype)
        reset()

    def step(cur_ref, nxt_ref, i, j):
        ni, nj = successor(i, j)
        nxt = (jnp.minimum(ni, n_q - 1), nj)

        @pl.when(j < i)
        def _():
            plain(cur_ref, nxt_ref, i, j, nxt)

        @pl.when(j == i)
        def _():
            closing(cur_ref, nxt_ref, i, nxt)

    def two_steps(_, ij):
        i, j = ij
        i1, j1 = successor(i, j)
        i2, j2 = successor(i1, j1)
        after = (jnp.minimum(i2, n_q - 1), j2)

        @pl.when(j + 1 < i)
        def _():
            plain(sa_ref, sb_ref, i, j, (i, j + 1))
            plain(sb_ref, sa_ref, i, j + 1, after)

        @pl.when(j + 1 == i)
        def _():
            plain(sa_ref, sb_ref, i, j, (i, i))
            closing(sb_ref, sa_ref, i, after)

        @pl.when(j == i)
        def _():
            closing(sa_ref, sb_ref, i, (i + 1, 0))
            plain(sb_ref, sa_ref, i + 1, 0, after)

        return i2, j2

    reset()
    scores(0, 0, sa_ref)
    ij = lax.fori_loop(0, n_pairs // 2, two_steps, (jnp.int32(0), jnp.int32(0)))
    if n_pairs % 2:
        step(sa_ref, sb_ref, *ij)


def _mla_attention(q, kv, kpe, n_heads, *, t=512):
    B, S, _ = q.shape
    d = HEAD_DIM
    assert S % t == 0 and t % MLA_CHUNK == 0
    return pl.pallas_call(
        functools.partial(_mla_kernel, t=t),
        grid=(B, n_heads),
        in_specs=[pl.BlockSpec((1, S, 2 * d), lambda b, h: (b, 0, h)),
                  pl.BlockSpec((1, S, d), lambda b, h: (b, 0, 2 * h)),
                  pl.BlockSpec((1, S, d), lambda b, h: (b, 0, 2 * h + 1)),
                  pl.BlockSpec((1, S, d), lambda b, h: (b, 0, 0))],
        out_specs=pl.BlockSpec((1, S, d), lambda b, h: (b, 0, h)),
        out_shape=jax.ShapeDtypeStruct((B, S, n_heads * d), BF16),
        scratch_shapes=[pltpu.VMEM((2 * d, S), BF16), pltpu.VMEM((S, 2 * d), BF16),
                        pltpu.VMEM((t, t), F32), pltpu.VMEM((t, t), F32),
                        pltpu.VMEM((t, d), F32), pltpu.VMEM((t, 2 * d), F32)],
        compiler_params=_params("parallel", "parallel"),
        name="mla_attn",
    )(q, kv, kv, kpe)


def _sb_layer(h, gain, w_qkv, w_o, B, S):
    M, D = h.shape
    H = w_o.shape[0] // HEAD_DIM
    col_scale = jnp.where(jnp.arange(w_qkv.shape[1]) < H * HEAD_DIM, HEAD_DIM ** -0.5 * LOG2E, 1.0)
    qkv = _linear(h, (w_qkv * col_scale).astype(BF16), name="sb_qkv", gain=gain, out_dtype=BF16)
    o = _sb_attention(qkv.reshape(B, S, -1), H)
    return _linear(o.reshape(M, -1), w_o.astype(BF16), name="sb_out", residual=h)


def _hg_layer(h, gain, w_in, g_norm, w_o, lb, B, S):
    M, D = h.shape
    H = w_o.shape[0] // HEAD_DIM
    proj = _linear(h, w_in.astype(BF16), name="hg_in", gain=gain, out_dtype=F32)
    o = _hgrn2(proj.reshape(B, S, -1), lb, g_norm, H)
    return _linear(o.reshape(M, -1), w_o.astype(BF16), name="hg_out", residual=h)


def _mla_layer(h, positions, gain, w_dkv, q_norm, kv_norm, w_uq, w_ukv, w_o, B, S):
    M, D = h.shape
    H = w_o.shape[0] // MLA_NOPE
    qr, kvr = MLA_Q_RANK, MLA_KV_RANK
    pad_a = jnp.zeros((D, qr - kvr), F32)
    pad_b = jnp.zeros((D, HEAD_DIM - MLA_ROPE), F32)
    w_d = jnp.concatenate([w_dkv[:, qr:qr + kvr], pad_a, w_dkv[:, :qr], w_dkv[:, qr + kvr:], pad_b], axis=1)
    dkv = _linear(h, w_d.astype(BF16), name="mla_down", gain=gain, out_dtype=F32)
    kpe, cos, sin = _rope_tables(positions, dkv, (2 * qr) // HEAD_DIM)
    w_q = w_uq.reshape(qr, H, MLA_NOPE + MLA_ROPE) * ((MLA_NOPE + MLA_ROPE) ** -0.5 * LOG2E)
    w_q = jnp.concatenate([w_q, jnp.zeros((qr, H, HEAD_DIM - MLA_ROPE), F32)], axis=2).reshape(qr, H * 2 * HEAD_DIM)
    q = _linear(dkv, w_q.astype(BF16), name="mla_q", gain=q_norm, rope=(cos, sin), out_dtype=BF16, x_col=1)
    kv = _linear(dkv, w_ukv.astype(BF16), name="mla_kv", gain=kv_norm, out_dtype=BF16, x_col=0)
    o = _mla_attention(q.reshape(B, S, -1), kv.reshape(B, S, -1), kpe.reshape(B, S, -1), H)
    return _linear(o.reshape(M, -1), w_o.astype(BF16), name="mla_out", residual=h)


def kernel(x, positions, norm_mix, norm_mlp, final_norm, sb_w_qkv, sb_w_o, hg_w_in, hg_lb_logits, hg_g_norm, hg_w_o, mla_w_dkv, mla_q_norm, mla_kv_norm, mla_w_uq, mla_w_ukv, mla_w_o, mlp_w1, mlp_w2):
    B, S, D = x.shape
    depth = norm_mix.shape[0]
    assert depth >= 1
    p_lb = jax.nn.softmax(hg_lb_logits.astype(F32), axis=0)
    lb_all = jnp.cumsum(p_lb, axis=0) - p_lb[0]
    h = x.reshape(B * S, D)
    w1, w2 = mlp_w1.astype(BF16), mlp_w2.astype(BF16)
    for i in range(depth):
        m, j = i % N_MIXERS, i // N_MIXERS
        if m == 0:
            h = _sb_layer(h, norm_mix[i], sb_w_qkv[j], sb_w_o[j], B, S)
        elif m == 1:
            h = _hg_layer(h, norm_mix[i], hg_w_in[j], hg_g_norm[j], hg_w_o[j], lb_all[i], B, S)
        else:
            h = _mla_layer(h, positions, norm_mix[i], mla_w_dkv[j], mla_q_norm[j], mla_kv_norm[j],
                           mla_w_uq[j], mla_w_ukv[j], mla_w_o[j], B, S)
        h = _mlp(h, norm_mlp[i], w1, w2, i, out_gain=final_norm if i == depth - 1 else None)
    return h.reshape(B, S, D)
```

```python
import functools

import jax
import jax.numpy as jnp
from jax import lax
from jax.experimental import pallas as pl
from jax.experimental.pallas import tpu as pltpu

F32 = jnp.float32
BF16 = jnp.bfloat16

RMS_EPS = 1e-6
N_MIXERS = 3
HEAD_DIM = 128
MLA_NOPE = 128
MLA_ROPE = 64
MLA_Q_RANK = 768
MLA_KV_RANK = 512
MLA_CHUNK = 64
ROPE_THETA = 10000.0
LOG2E = 1.4426950408889634
VMEM_LIMIT = 56 * 1024 * 1024


def _params(*sem):
    return pltpu.CompilerParams(dimension_semantics=sem, vmem_limit_bytes=VMEM_LIMIT)


def _dot(a, b):
    return jnp.dot(a, b, preferred_element_type=F32)


def _dot_nt(a, b):
    return lax.dot_general(a, b, (((1,), (1,)), ((), ())), preferred_element_type=F32)


def _dot_tn(a, b):
    return lax.dot_general(a, b, (((0,), (0,)), ((), ())), preferred_element_type=F32)


def _split3(x):
    hi = x.astype(BF16)
    r1 = x - hi.astype(F32)
    mid = r1.astype(BF16)
    lo = (r1 - mid.astype(F32)).astype(BF16)
    return hi, mid, lo


def _rotate_half(x, cos, sin):
    half = MLA_ROPE // 2
    swapped = pltpu.roll(x, half, 1) + pltpu.roll(x, HEAD_DIM - half, 1)
    return x * cos + swapped * sin


def _linear_kernel(*refs, has_gain, has_res, has_rope):
    x_ref, w_ref = refs[0], refs[1]
    pos = 2
    g_ref = r_ref = cos_ref = sin_ref = None
    if has_gain:
        g_ref = refs[pos]
        pos += 1
    if has_res:
        r_ref = refs[pos]
        pos += 1
    if has_rope:
        cos_ref, sin_ref = refs[pos], refs[pos + 1]
        pos += 2
    o_ref = refs[pos]

    if has_gain:
        xn_ref = refs[pos + 1]

        @pl.when(pl.program_id(1) == 0)
        def _():
            x = x_ref[...].astype(F32)
            ms = jnp.mean(x * x, axis=-1, keepdims=True)
            xn_ref[...] = (x * lax.rsqrt(ms + RMS_EPS) * g_ref[...]).astype(BF16)

        lhs = xn_ref[...]
    else:
        lhs = x_ref[...]

    acc = _dot(lhs, w_ref[...])
    if has_res:
        acc = acc + r_ref[...]
    if has_rope:
        cos, sin = cos_ref[...], sin_ref[...]
        for c in range(0, acc.shape[1], 2 * HEAD_DIM):
            o_ref[:, c:c + HEAD_DIM] = acc[:, c:c + HEAD_DIM].astype(o_ref.dtype)
            rot = _rotate_half(acc[:, c + HEAD_DIM:c + 2 * HEAD_DIM], cos, sin)
            o_ref[:, c + HEAD_DIM:c + 2 * HEAD_DIM] = rot.astype(o_ref.dtype)
    else:
        o_ref[...] = acc.astype(o_ref.dtype)


LINEAR_VMEM_BUDGET = 44 * 1024 * 1024
RESIDENT_WEIGHT_BYTES = 8 * 1024 * 1024


def _linear_tiles(M, K, N, x_bytes, out_bytes, has_gain, has_res, has_rope):
    widths = [N] if K * N * 2 <= RESIDENT_WEIGHT_BYTES or N % 1024 else [w for w in (4096, 2048, 1024) if N % w == 0]
    for tm in (1024, 512, 256, 128):
        if M % tm:
            continue
        for tn in widths:
            need = 2 * tm * K * x_bytes + 2 * K * tn * 2 + 2 * tm * tn * out_bytes
            need += tm * K * 2 if has_gain else 0
            need += 2 * tm * tn * 4 if has_res else 0
            need += tm * tn * 4 + 4 * tm * HEAD_DIM * 4 if has_rope else 0
            if need <= LINEAR_VMEM_BUDGET:
                return tm, tn
    raise ValueError("no linear tiling fits VMEM")


def _linear(x, w, *, name, gain=None, residual=None, rope=None, out_dtype=F32, x_col=0):
    M = x.shape[0]
    K, N = w.shape
    assert gain is not None or x.dtype == BF16
    tm, tn = _linear_tiles(M, K, N, x.dtype.itemsize, jnp.dtype(out_dtype).itemsize,
                           gain is not None, residual is not None, rope is not None)
    assert rope is None or tn % (2 * HEAD_DIM) == 0
    in_specs = [pl.BlockSpec((tm, K), lambda i, j: (i, x_col)),
                pl.BlockSpec((K, tn), lambda i, j: (0, j))]
    args = [x, w]
    if gain is not None:
        in_specs.append(pl.BlockSpec((1, K), lambda i, j: (0, 0)))
        args.append(gain.reshape(1, K).astype(F32))
    if residual is not None:
        in_specs.append(pl.BlockSpec((tm, tn), lambda i, j: (i, j)))
        args.append(residual)
    if rope is not None:
        in_specs += [pl.BlockSpec((tm, HEAD_DIM), lambda i, j: (i, 0))] * 2
        args += list(rope)
    return pl.pallas_call(
        functools.partial(_linear_kernel, has_gain=gain is not None, has_res=residual is not None,
                          has_rope=rope is not None),
        grid=(M // tm, N // tn),
        in_specs=in_specs,
        out_specs=pl.BlockSpec((tm, tn), lambda i, j: (i, j)),
        out_shape=jax.ShapeDtypeStruct((M, N), out_dtype),
        scratch_shapes=[pltpu.VMEM((tm, K), BF16)] if gain is not None else [],
        compiler_params=_params("parallel", "arbitrary"),
        name=name,
    )(*args)


def _mlp_kernel(*refs, has_out_norm):
    if has_out_norm:
        x_ref, g_ref, w1_ref, w2_ref, go_ref, o_ref, xn_ref = refs
    else:
        x_ref, g_ref, w1_ref, w2_ref, o_ref, xn_ref = refs
    f = pl.program_id(1)

    @pl.when(f == 0)
    def _():
        x = x_ref[...]
        ms = jnp.mean(x * x, axis=-1, keepdims=True)
        xn_ref[...] = (x * lax.rsqrt(ms + RMS_EPS) * g_ref[...]).astype(BF16)
        o_ref[...] = x

    u = _dot(xn_ref[...], w1_ref[...])
    r = jnp.square(jnp.maximum(u, 0.0)).astype(BF16)
    o_ref[...] += _dot(r, w2_ref[...])

    if has_out_norm:
        @pl.when(f == pl.num_programs(1) - 1)
        def _():
            y = o_ref[...]
            ms = jnp.mean(y * y, axis=-1, keepdims=True)
            o_ref[...] = y * lax.rsqrt(ms + RMS_EPS) * go_ref[...]


def _mlp(h, gain, w1, w2, layer, *, out_gain=None, tm=512, tf=1024):
    M, D = h.shape
    F = w1.shape[2]
    assert M % tm == 0 and F % tf == 0
    vec = pl.BlockSpec((1, D), lambda i, f: (0, 0))
    in_specs = [pl.BlockSpec((tm, D), lambda i, f: (i, 0)), vec,
                pl.BlockSpec((None, D, tf), lambda i, f: (layer, 0, f)),
                pl.BlockSpec((None, tf, D), lambda i, f: (layer, f, 0))]
    args = [h, gain.reshape(1, D).astype(F32), w1, w2]
    if out_gain is not None:
        in_specs.append(vec)
        args.append(out_gain.reshape(1, D).astype(F32))
    return pl.pallas_call(
        functools.partial(_mlp_kernel, has_out_norm=out_gain is not None),
        grid=(M // tm, F // tf),
        in_specs=in_specs,
        out_specs=pl.BlockSpec((tm, D), lambda i, f: (i, 0)),
        out_shape=jax.ShapeDtypeStruct((M, D), F32),
        scratch_shapes=[pltpu.VMEM((tm, D), BF16)],
        compiler_params=_params("parallel", "arbitrary"),
        name="mlp",
    )(*args)


SB_SKIP = 90.0


def _sb_kernel(q_ref, k_ref, v_ref, o_ref, *, t, n_sub):
    i = pl.program_id(2)
    row = lax.broadcasted_iota(jnp.int32, (t, t), 0)
    col = lax.broadcasted_iota(jnp.int32, (t, t), 1)
    later = (row > col).astype(BF16)
    causal = col < row

    def logits(q, j):
        return _dot_nt(q, k_ref[0, pl.ds(pl.multiple_of(j * t, t), t), :])

    def log_gates(z, diagonal):
        nz = -z
        lom = jnp.minimum(nz, 0.0) - jnp.log2(1.0 + jnp.exp2(jnp.minimum(z, nz)))
        log_beta = lom + z
        if diagonal:
            lom = jnp.where(causal, lom, 0.0)
        return log_beta, lom

    def later_sums(lom):
        return _dot(lom.astype(BF16), later), jnp.sum(lom, axis=1, keepdims=True)

    def gates(q, j, diagonal):
        log_beta, lom = log_gates(logits(q, j), diagonal)
        return (log_beta,) + later_sums(lom)

    def weigh(j, log_beta, after, diagonal):
        a = jnp.exp2(log_beta + after)
        if diagonal:
            a = jnp.where(causal, a, 0.0)
        return _dot(a.astype(BF16), v_ref[0, pl.ds(pl.multiple_of(j * t, t), t), :])

    skip = -SB_SKIP * LOG2E
    tiles = []
    for a in range(n_sub):
        jd = i * n_sub + a
        tiles.append((slice(a * t, (a + 1) * t), q_ref[0, a * t:(a + 1) * t, :], jd, jnp.maximum(jd - 1, 0)))
    z_d = [logits(q, jd) for _, q, jd, _ in tiles]
    z_p = [logits(q, jp) for _, q, _, jp in tiles]
    g_d = [log_gates(z, True) for z in z_d]
    g_p = [log_gates(z, False) for z in z_p]
    s_d = [later_sums(lom) for _, lom in g_d]
    s_p = [later_sums(lom) for _, lom in g_p]
    near = []
    pending = jnp.full((t, 1), skip, F32)
    for (rows, q, jd, jp), (lb_d, _), (lb_p, _), (within_d, carry), (within_p, total_p) in zip(tiles, g_d, g_p, s_d, s_p):
        has_prev = jd > 0
        acc = weigh(jd, lb_d, within_d, True) + jnp.where(has_prev, weigh(jp, lb_p, within_p + carry, False), 0.0)
        carry = carry + jnp.where(has_prev, total_p, 0.0)
        o_ref[0, rows, :] = acc.astype(o_ref.dtype)
        near.append((rows, q, jd, carry, acc))
        pending = jnp.maximum(pending, jnp.where(jd >= 2, carry, skip))

    def cond(s):
        j, carry, _ = s
        return jnp.logical_and(j >= 0, jnp.max(carry) > skip)

    @pl.when(jnp.max(pending) > skip)
    def _():
        for rows, q, jd, carry, acc in near:
            def body(s, q=q):
                j, carry, acc = s
                log_beta, within, total = gates(q, j, False)
                return j - 1, carry + total, acc + weigh(j, log_beta, within + carry, False)

            _, _, acc = lax.while_loop(cond, body, (jd - 2, carry, acc))
            o_ref[0, rows, :] = acc.astype(o_ref.dtype)


def _sb_attention(qkv, n_heads, *, t=256, n_sub=8):
    B, S, _ = qkv.shape
    d = HEAD_DIM
    tile = t * n_sub
    assert S % tile == 0
    return pl.pallas_call(
        functools.partial(_sb_kernel, t=t, n_sub=n_sub),
        grid=(B, n_heads, S // tile),
        in_specs=[pl.BlockSpec((1, tile, d), lambda b, h, i: (b, i, h)),
                  pl.BlockSpec((1, S, d), lambda b, h, i: (b, 0, n_heads + h)),
                  pl.BlockSpec((1, S, d), lambda b, h, i: (b, 0, 2 * n_heads + h))],
        out_specs=pl.BlockSpec((1, tile, d), lambda b, h, i: (b, i, h)),
        out_shape=jax.ShapeDtypeStruct((B, S, n_heads * d), BF16),
        compiler_params=_params("parallel", "parallel", "arbitrary"),
        name="sb_attn",
    )(qkv, qkv, qkv)


HG_CHUNK = 64
HG_SUB = 16
HG_GROUP = 4
HG_SPAN = 256
HG_SAFE = 70.0


def _hg_kernel(q_ref, fz_ref, v_ref, g_ref, lb_ref, gn_ref, tril_ref, o_ref, state_ref, oc_ref, b_ref, k_ref,
               *, n_chunks):
    C, U, d = HG_CHUNK, HG_SUB, HEAD_DIM

    @pl.when(pl.program_id(2) == 0)
    def _():
        state_ref[...] = jnp.zeros_like(state_ref)

    gn = gn_ref[...]
    row = lax.broadcasted_iota(jnp.int32, (C, C), 0)
    col = lax.broadcasted_iota(jnp.int32, (C, C), 1)
    lower = col <= row
    sub_row = lax.broadcasted_iota(jnp.int32, (U, 1), 0)
    tril = tril_ref[...]

    heads = []
    for hd in range(HG_GROUP):
        lanes = slice(hd * d, (hd + 1) * d)
        lb = lb_ref[:, lanes]
        fz = fz_ref[0, :, lanes]
        e = jnp.exp(-jnp.abs(fz))
        inv = 1.0 / (1.0 + e)
        sig_pos = jnp.where(fz >= 0, inv, e * inv)
        sig_neg = jnp.where(fz >= 0, e * inv, inv)
        log_f = jnp.log(lb + (1.0 - lb) * sig_pos)
        k = (1.0 - lb) * sig_neg
        l_hi = log_f.astype(BF16)
        l_lo = (log_f - l_hi.astype(F32)).astype(BF16)
        span = tril.shape[0]
        b = jnp.concatenate([_dot(tril, l_hi[r:r + span]) + _dot(tril, l_lo[r:r + span])
                             for r in range(0, fz.shape[0], span)], axis=0)
        heads.append((lanes, k, b))

    def chunk_row(b, r):
        picked = b.reshape(n_chunks, C, d)[:, r:r + 1, :]
        return jnp.broadcast_to(picked, (n_chunks, C, d)).reshape(b.shape)

    mids = [b - chunk_row(b, C // 2 - 1) for _, _, b in heads]
    safe = functools.reduce(jnp.maximum, [jnp.max(jnp.abs(rel)) for rel in mids]) < HG_SAFE

    def epilogue(o, g):
        o = o * lax.rsqrt(jnp.mean(o * o, axis=-1, keepdims=True) + RMS_EPS) * gn
        return (o * (g / (1.0 + jnp.exp(-g)))).astype(o_ref.dtype)

    @pl.when(safe)
    def _():
        prep = []
        for (lanes, k, b), rel in zip(heads, mids):
            q = q_ref[0, :, lanes]
            v16 = v_ref[0, :, lanes].astype(BF16)
            q_in = (q * jnp.exp(rel)).astype(BF16)
            k_in = (k * jnp.exp(-rel)).astype(BF16)
            q_st = (q * jnp.exp(b)).astype(BF16)
            b_last = b.reshape(n_chunks, C, d)[:, C - 1:C, :]
            kd = (k * jnp.exp(chunk_row(b, C - 1) - b)).astype(BF16)
            prep.append((q_in, k_in, q_st, v16, kd, jnp.exp(b_last)))
        states = [state_ref[hd] for hd in range(HG_GROUP)]
        for c in range(n_chunks):
            rows = slice(c * C, (c + 1) * C)
            for hd, (q_in, k_in, q_st, v16, kd, decay) in enumerate(prep):
                s = jnp.where(lower, _dot_nt(q_in[rows], k_in[rows]), 0.0)
                oc_ref[hd, rows, :] = (_dot(s.astype(BF16), v16[rows])
                                       + _dot_nt(q_st[rows], states[hd].astype(BF16)))
                states[hd] = states[hd] * decay[c] + _dot_tn(v16[rows], kd[rows])
        for hd, (lanes, _, _) in enumerate(heads):
            state_ref[hd] = states[hd]
            o_ref[0, :, lanes] = epilogue(oc_ref[hd], g_ref[0, :, lanes])

    def head_chunk(hd, r0):
        lanes = slice(hd * d, (hd + 1) * d)
        q = q_ref[0, pl.ds(r0, C), lanes]
        v = v_ref[0, pl.ds(r0, C), lanes]
        b = b_ref[hd, pl.ds(r0, C), :]
        k = k_ref[hd, pl.ds(r0, C), :]
        log2_k = jnp.log2(k)
        state_t = state_ref[hd]
        rows = pl.ds(r0, C)
        oc_ref[hd, rows, :] = _dot_nt((q * jnp.exp(b)).astype(BF16), state_t.astype(BF16))
        v16 = v.astype(BF16)
        for J in range(C // U - 1):
            lo_r, hi_r = J * U, (J + 1) * U
            ref_b = b[hi_r - 1:hi_r, :]
            kt = (k[lo_r:hi_r] * jnp.exp(ref_b - b[lo_r:hi_r])).astype(BF16)
            qs = (q[hi_r:] * jnp.exp(b[hi_r:] - ref_b)).astype(BF16)
            s = _dot_nt(qs, kt)
            oc_ref[hd, pl.ds(r0 + hi_r, C - hi_r), :] += _dot(s.astype(BF16), v16[lo_r:hi_r])
        b2 = b * LOG2E
        key2 = log2_k - b2
        for J in range(C // U):
            lo_r, hi_r = J * U, (J + 1) * U
            qj, bj, kj, vj = q[lo_r:hi_r], b2[lo_r:hi_r], key2[lo_r:hi_r], v[lo_r:hi_r]
            acc = jnp.zeros((U, d), F32)
            for s_i in range(U):
                w = qj * jnp.exp2(jnp.minimum(bj + kj[s_i:s_i + 1], 0.0))
                sc = jnp.sum(w, axis=1, keepdims=True)
                sc = jnp.where(sub_row >= s_i, sc, 0.0)
                acc = acc + sc * vj[s_i:s_i + 1]
            oc_ref[hd, pl.ds(r0 + lo_r, U), :] += acc
        b_last = b[C - 1:C, :]
        kd = (k * jnp.exp(b_last - b)).astype(BF16)
        state_ref[hd] = state_t * jnp.exp(b_last) + _dot_tn(v16, kd)
        o_ref[0, rows, lanes] = epilogue(oc_ref[hd, rows, :], g_ref[0, rows, lanes])

    @pl.when(jnp.logical_not(safe))
    def _():
        for hd, (_, k, b) in enumerate(heads):
            b_ref[hd] = b
            k_ref[hd] = k

        def chunk(c, _):
            r0 = pl.multiple_of(c * C, C)
            for hd in range(HG_GROUP):
                head_chunk(hd, r0)
            return 0

        lax.fori_loop(0, n_chunks, chunk, 0)


def _hgrn2(proj, lb, g_norm, n_heads, *, ts=512):
    B, S, _ = proj.shape
    d, G = HEAD_DIM, HG_GROUP
    assert S % ts == 0 and ts % HG_CHUNK == 0 and n_heads % G == 0
    n_groups = n_heads // G

    def col(part):
        return pl.BlockSpec((1, ts, G * d), lambda b, h, s: (b, s, part * n_groups + h))

    span = min(ts, HG_SPAN)
    r = jnp.arange(span)
    tril = ((r[None, :] <= r[:, None]) & (r[None, :] // HG_CHUNK == r[:, None] // HG_CHUNK)).astype(BF16)
    return pl.pallas_call(
        functools.partial(_hg_kernel, n_chunks=ts // HG_CHUNK),
        grid=(B, n_groups, S // ts),
        in_specs=[col(0), col(1), col(2), col(3),
                  pl.BlockSpec((1, G * d), lambda b, h, s: (0, h)),
                  pl.BlockSpec((1, d), lambda b, h, s: (0, 0)),
                  pl.BlockSpec((span, span), lambda b, h, s: (0, 0))],
        out_specs=pl.BlockSpec((1, ts, G * d), lambda b, h, s: (b, s, h)),
        out_shape=jax.ShapeDtypeStruct((B, S, n_heads * d), BF16),
        scratch_shapes=[pltpu.VMEM((G, d, d), F32), pltpu.VMEM((G, ts, d), F32),
                        pltpu.VMEM((G, ts, d), F32), pltpu.VMEM((G, ts, d), F32)],
        compiler_params=_params("parallel", "parallel", "arbitrary"),
        name="hgrn2",
    )(proj, proj, proj, proj, lb.reshape(1, n_heads * d), g_norm.reshape(1, d), tril)


def _rope_kernel(pos_ref, invf_ref, x_ref, kpe_ref, cos_ref, sin_ref):
    half = MLA_ROPE // 2
    ang = pos_ref[...].astype(F32) * invf_ref[...]
    lane = lax.broadcasted_iota(jnp.int32, ang.shape, 1)
    c = jnp.where(lane < MLA_ROPE, jnp.cos(ang), 0.0)
    s = jnp.sin(ang)
    s = jnp.where(lane < half, -s, jnp.where(lane < MLA_ROPE, s, 0.0))
    cos_ref[...] = c
    sin_ref[...] = s
    kpe_ref[...] = _rotate_half(x_ref[...], c, s).astype(kpe_ref.dtype)


def _rope_tables(positions, dkv, kpe_col, *, tm=512):
    M = dkv.shape[0]
    half = MLA_ROPE // 2
    freq = ROPE_THETA ** (-jnp.arange(0, MLA_ROPE, 2, dtype=F32) / MLA_ROPE)
    invf = jnp.concatenate([freq, freq, jnp.zeros((HEAD_DIM - 2 * half,), F32)]).reshape(1, HEAD_DIM)
    row = pl.BlockSpec((tm, HEAD_DIM), lambda i: (i, 0))
    return pl.pallas_call(
        _rope_kernel,
        grid=(M // tm,),
        in_specs=[pl.BlockSpec((tm, 1), lambda i: (i, 0)),
                  pl.BlockSpec((1, HEAD_DIM), lambda i: (0, 0)),
                  pl.BlockSpec((tm, HEAD_DIM), lambda i: (i, kpe_col))],
        out_specs=[row, row, row],
        out_shape=[jax.ShapeDtypeStruct((M, HEAD_DIM), BF16),
                   jax.ShapeDtypeStruct((M, HEAD_DIM), F32),
                   jax.ShapeDtypeStruct((M, HEAD_DIM), F32)],
        compiler_params=_params("parallel"),
        name="mla_rope",
    )(positions.reshape(M, 1), invf, dkv)


MLA_ROW_CHUNK = 256


def _mla_kernel(q_ref, kn_ref, v_ref, kpe_ref, o_ref, kcat_ref, v1_ref, sa_ref, sb_ref, m_ref, acc_ref, *, t):
    n_q = q_ref.shape[1] // t
    n_pairs = n_q * (n_q + 1) // 2
    for n in range(n_q):
        rows = slice(n * t, (n + 1) * t)
        kcat_ref[:MLA_NOPE, rows] = kn_ref[0, rows, :].T
        kcat_ref[MLA_NOPE:, rows] = kpe_ref[0, rows, :].T
    rc = MLA_ROW_CHUNK
    row = lax.broadcasted_iota(jnp.int32, (rc, t), 0) // MLA_CHUNK
    col = lax.broadcasted_iota(jnp.int32, (rc, t), 1) // MLA_CHUNK

    def tile(n):
        return pl.ds(pl.multiple_of(n * t, t), t)

    def scores(i, j, dst_ref):
        dst_ref[...] = _dot(q_ref[0, tile(i), :], kcat_ref[:, tile(j)])

    v1_ref[:, :HEAD_DIM] = v_ref[0]
    v1_ref[:, HEAD_DIM:] = jnp.ones((v1_ref.shape[0], HEAD_DIM), BF16)

    def reset():
        m_ref[...] = jnp.full_like(m_ref, -jnp.inf)
        acc_ref[...] = jnp.zeros_like(acc_ref)

    def absorb(j, src_ref, diagonal):
        vb = v1_ref[tile(j), :]
        for r in range(t // rc):
            rows = slice(r * rc, (r + 1) * rc)
            s = src_ref[rows, :]
            if diagonal:
                s = jnp.where(col <= row + (r * rc) // MLA_CHUNK, s, -jnp.inf)
            m_old = m_ref[rows, :]
            lane_groups = [s[:, c:c + HEAD_DIM] for c in range(0, t, HEAD_DIM)]
            m_col = jnp.max(functools.reduce(jnp.maximum, lane_groups + [m_old]), axis=1, keepdims=True)
            m_new = jnp.broadcast_to(m_col, m_old.shape)
            alpha = jnp.exp2(m_old - m_new)
            p = jnp.exp2((s - m_col).astype(BF16))
            acc_ref[rows, :] = jnp.concatenate([alpha, alpha], axis=1) * acc_ref[rows, :] + _dot(p, vb)
            m_ref[rows, :] = m_new

    def successor(i, j):
        last = j == i
        return jnp.where(last, i + 1, i), jnp.where(last, 0, j + 1)

    def plain(cur_ref, nxt_ref, i, j, nxt):
        scores(*nxt, nxt_ref)
        absorb(j, cur_ref, False)

    def closing(cur_ref, nxt_ref, i, nxt):
        scores(*nxt, nxt_ref)
        absorb(i, cur_ref, True)
        o_ref[0, tile(i), :] = (acc_ref[:, :HEAD_DIM] / acc_ref[:, HEAD_DIM:]).astype(o_ref.dtype)
        reset()

    def step(cur_ref, nxt_ref, i, j):
        ni, nj = successor(i, j)
        nxt = (jnp.minimum(ni, n_q - 1), nj)

        @pl.when(j < i)
        def _():
            plain(cur_ref, nxt_ref, i, j, nxt)

        @pl.when(j == i)
        def _():
            closing(cur_ref, nxt_ref, i, nxt)

    def two_steps(i, j):
        i2, j2 = successor(*successor(i, j))
        after = (jnp.minimum(i2, n_q - 1), j2)

        @pl.when(j + 1 < i)
        def _():
            plain(sa_ref, sb_ref, i, j, (i, j + 1))
            plain(sb_ref, sa_ref, i, j + 1, after)

        @pl.when(j + 1 == i)
        def _():
            plain(sa_ref, sb_ref, i, j, (i, i))
            closing(sb_ref, sa_ref, i, after)

        @pl.when(j == i)
        def _():
            closing(sa_ref, sb_ref, i, (i + 1, 0))
            plain(sb_ref, sa_ref, i + 1, 0, after)

        return i2, j2

    def four_steps(_, ij):
        i, j = ij

        @pl.when(j + 3 < i)
        def _():
            plain(sa_ref, sb_ref, i, j, (i, j + 1))
            plain(sb_ref, sa_ref, i, j + 1, (i, j + 2))
            plain(sa_ref, sb_ref, i, j + 2, (i, j + 3))
            plain(sb_ref, sa_ref, i, j + 3, (i, j + 4))

        @pl.when(j + 3 >= i)
        def _():
            two_steps(*two_steps(i, j))

        return successor(*successor(*successor(*successor(i, j))))

    reset()
    scores(0, 0, sa_ref)
    ij = lax.fori_loop(0, n_pairs // 4, four_steps, (jnp.int32(0), jnp.int32(0)))
    if n_pairs % 4 >= 2:
        ij = two_steps(*ij)
    if n_pairs % 2:
        step(sa_ref, sb_ref, *ij)


def _mla_attention(q, kv, kpe, n_heads, *, t=512):
    B, S, _ = q.shape
    d = HEAD_DIM
    assert S % t == 0 and t % MLA_CHUNK == 0
    return pl.pallas_call(
        functools.partial(_mla_kernel, t=t),
        grid=(B, n_heads),
        in_specs=[pl.BlockSpec((1, S, 2 * d), lambda b, h: (b, 0, h)),
                  pl.BlockSpec((1, S, d), lambda b, h: (b, 0, 2 * h)),
                  pl.BlockSpec((1, S, d), lambda b, h: (b, 0, 2 * h + 1)),
                  pl.BlockSpec((1, S, d), lambda b, h: (b, 0, 0))],
        out_specs=pl.BlockSpec((1, S, d), lambda b, h: (b, 0, h)),
        out_shape=jax.ShapeDtypeStruct((B, S, n_heads * d), BF16),
        scratch_shapes=[pltpu.VMEM((2 * d, S), BF16), pltpu.VMEM((S, 2 * d), BF16),
                        pltpu.VMEM((t, t), F32), pltpu.VMEM((t, t), F32),
                        pltpu.VMEM((t, d), F32), pltpu.VMEM((t, 2 * d), F32)],
        compiler_params=_params("parallel", "parallel"),
        name="mla_attn",
    )(q, kv, kv, kpe)


def _sb_layer(h, gain, w_qkv, w_o, B, S):
    M, D = h.shape
    H = w_o.shape[0] // HEAD_DIM
    col_scale = jnp.where(jnp.arange(w_qkv.shape[1]) < H * HEAD_DIM, HEAD_DIM ** -0.5 * LOG2E, 1.0)
    qkv = _linear(h, (w_qkv * col_scale).astype(BF16), name="sb_qkv", gain=gain, out_dtype=BF16)
    o = _sb_attention(qkv.reshape(B, S, -1), H)
    return _linear(o.reshape(M, -1), w_o.astype(BF16), name="sb_out", residual=h)


def _hg_layer(h, gain, w_in, g_norm, w_o, lb, B, S):
    M, D = h.shape
    H = w_o.shape[0] // HEAD_DIM
    proj = _linear(h, w_in.astype(BF16), name="hg_in", gain=gain, out_dtype=F32)
    o = _hgrn2(proj.reshape(B, S, -1), lb, g_norm, H)
    return _linear(o.reshape(M, -1), w_o.astype(BF16), name="hg_out", residual=h)


def _mla_layer(h, positions, gain, w_dkv, q_norm, kv_norm, w_uq, w_ukv, w_o, B, S):
    M, D = h.shape
    H = w_o.shape[0] // MLA_NOPE
    qr, kvr = MLA_Q_RANK, MLA_KV_RANK
    pad_a = jnp.zeros((D, qr - kvr), F32)
    pad_b = jnp.zeros((D, HEAD_DIM - MLA_ROPE), F32)
    w_d = jnp.concatenate([w_dkv[:, qr:qr + kvr], pad_a, w_dkv[:, :qr], w_dkv[:, qr + kvr:], pad_b], axis=1)
    dkv = _linear(h, w_d.astype(BF16), name="mla_down", gain=gain, out_dtype=F32)
    kpe, cos, sin = _rope_tables(positions, dkv, (2 * qr) // HEAD_DIM)
    w_q = w_uq.reshape(qr, H, MLA_NOPE + MLA_ROPE) * ((MLA_NOPE + MLA_ROPE) ** -0.5 * LOG2E)
    w_q = jnp.concatenate([w_q, jnp.zeros((qr, H, HEAD_DIM - MLA_ROPE), F32)], axis=2).reshape(qr, H * 2 * HEAD_DIM)
    q = _linear(dkv, w_q.astype(BF16), name="mla_q", gain=q_norm, rope=(cos, sin), out_dtype=BF16, x_col=1)
    kv = _linear(dkv, w_ukv.astype(BF16), name="mla_kv", gain=kv_norm, out_dtype=BF16, x_col=0)
    o = _mla_attention(q.reshape(B, S, -1), kv.reshape(B, S, -1), kpe.reshape(B, S, -1), H)
    return _linear(o.reshape(M, -1), w_o.astype(BF16), name="mla_out", residual=h)


def kernel(x, positions, norm_mix, norm_mlp, final_norm, sb_w_qkv, sb_w_o, hg_w_in, hg_lb_logits, hg_g_norm, hg_w_o, mla_w_dkv, mla_q_norm, mla_kv_norm, mla_w_uq, mla_w_ukv, mla_w_o, mlp_w1, mlp_w2):
    B, S, D = x.shape
    depth = norm_mix.shape[0]
    assert depth >= 1
    p_lb = jax.nn.softmax(hg_lb_logits.astype(F32), axis=0)
    lb_all = jnp.cumsum(p_lb, axis=0) - p_lb[0]
    h = x.reshape(B * S, D)
    w1, w2 = mlp_w1.astype(BF16), mlp_w2.astype(BF16)
    for i in range(depth):
        m, j = i % N_MIXERS, i // N_MIXERS
        if m == 0:
            h = _sb_layer(h, norm_mix[i], sb_w_qkv[j], sb_w_o[j], B, S)
        elif m == 1:
            h = _hg_layer(h, norm_mix[i], hg_w_in[j], hg_g_norm[j], hg_w_o[j], lb_all[i], B, S)
        else:
            h = _mla_layer(h, positions, norm_mix[i], mla_w_dkv[j], mla_q_norm[j], mla_kv_norm[j],
                           mla_w_uq[j], mla_w_ukv[j], mla_w_o[j], B, S)
        h = _mlp(h, norm_mlp[i], w1, w2, i, out_gain=final_norm if i == depth - 1 else None)
    return h.reshape(B, S, D)
```

```python
import functools

import jax
import jax.numpy as jnp
from jax import lax
from jax.experimental import pallas as pl
from jax.experimental.pallas import tpu as pltpu

F32 = jnp.float32
BF16 = jnp.bfloat16

RMS_EPS = 1e-6
N_MIXERS = 3
HEAD_DIM = 128
MLA_NOPE = 128
MLA_ROPE = 64
MLA_Q_RANK = 768
MLA_KV_RANK = 512
MLA_CHUNK = 64
ROPE_THETA = 10000.0
LOG2E = 1.4426950408889634
VMEM_LIMIT = 56 * 1024 * 1024


def _params(*sem):
    return pltpu.CompilerParams(dimension_semantics=sem, vmem_limit_bytes=VMEM_LIMIT)


def _dot(a, b):
    return jnp.dot(a, b, preferred_element_type=F32)


def _dot_nt(a, b):
    return lax.dot_general(a, b, (((1,), (1,)), ((), ())), preferred_element_type=F32)


def _dot_tn(a, b):
    return lax.dot_general(a, b, (((0,), (0,)), ((), ())), preferred_element_type=F32)


def _split3(x):
    hi = x.astype(BF16)
    r1 = x - hi.astype(F32)
    mid = r1.astype(BF16)
    lo = (r1 - mid.astype(F32)).astype(BF16)
    return hi, mid, lo


def _rotate_half(x, cos, sin):
    half = MLA_ROPE // 2
    swapped = pltpu.roll(x, half, 1) + pltpu.roll(x, HEAD_DIM - half, 1)
    return x * cos + swapped * sin


def _linear_kernel(*refs, has_gain, has_res, has_rope):
    x_ref, w_ref = refs[0], refs[1]
    pos = 2
    g_ref = r_ref = cos_ref = sin_ref = None
    if has_gain:
        g_ref = refs[pos]
        pos += 1
    if has_res:
        r_ref = refs[pos]
        pos += 1
    if has_rope:
        cos_ref, sin_ref = refs[pos], refs[pos + 1]
        pos += 2
    o_ref = refs[pos]

    if has_gain:
        xn_ref = refs[pos + 1]

        @pl.when(pl.program_id(1) == 0)
        def _():
            x = x_ref[...].astype(F32)
            ms = jnp.mean(x * x, axis=-1, keepdims=True)
            xn_ref[...] = (x * lax.rsqrt(ms + RMS_EPS) * g_ref[...]).astype(BF16)

        lhs = xn_ref[...]
    else:
        lhs = x_ref[...]

    acc = _dot(lhs, w_ref[...])
    if has_res:
        acc = acc + r_ref[...]
    if has_rope:
        cos, sin = cos_ref[...], sin_ref[...]
        for c in range(0, acc.shape[1], 2 * HEAD_DIM):
            o_ref[:, c:c + HEAD_DIM] = acc[:, c:c + HEAD_DIM].astype(o_ref.dtype)
            rot = _rotate_half(acc[:, c + HEAD_DIM:c + 2 * HEAD_DIM], cos, sin)
            o_ref[:, c + HEAD_DIM:c + 2 * HEAD_DIM] = rot.astype(o_ref.dtype)
    else:
        o_ref[...] = acc.astype(o_ref.dtype)


LINEAR_VMEM_BUDGET = 44 * 1024 * 1024
RESIDENT_WEIGHT_BYTES = 8 * 1024 * 1024


def _linear_tiles(M, K, N, x_bytes, out_bytes, has_gain, has_res, has_rope):
    widths = [N] if K * N * 2 <= RESIDENT_WEIGHT_BYTES or N % 1024 else [w for w in (4096, 2048, 1024) if N % w == 0]
    for tm in (1024, 512, 256, 128):
        if M % tm:
            continue
        for tn in widths:
            need = 2 * tm * K * x_bytes + 2 * K * tn * 2 + 2 * tm * tn * out_bytes
            need += tm * K * 2 if has_gain else 0
            need += 2 * tm * tn * 4 if has_res else 0
            need += tm * tn * 4 + 4 * tm * HEAD_DIM * 4 if has_rope else 0
            if need <= LINEAR_VMEM_BUDGET:
                return tm, tn
    raise ValueError("no linear tiling fits VMEM")


def _linear(x, w, *, name, gain=None, residual=None, rope=None, out_dtype=F32, x_col=0):
    M = x.shape[0]
    K, N = w.shape
    assert gain is not None or x.dtype == BF16
    tm, tn = _linear_tiles(M, K, N, x.dtype.itemsize, jnp.dtype(out_dtype).itemsize,
                           gain is not None, residual is not None, rope is not None)
    assert rope is None or tn % (2 * HEAD_DIM) == 0
    in_specs = [pl.BlockSpec((tm, K), lambda i, j: (i, x_col)),
                pl.BlockSpec((K, tn), lambda i, j: (0, j))]
    args = [x, w]
    if gain is not None:
        in_specs.append(pl.BlockSpec((1, K), lambda i, j: (0, 0)))
        args.append(gain.reshape(1, K).astype(F32))
    if residual is not None:
        in_specs.append(pl.BlockSpec((tm, tn), lambda i, j: (i, j)))
        args.append(residual)
    if rope is not None:
        in_specs += [pl.BlockSpec((tm, HEAD_DIM), lambda i, j: (i, 0))] * 2
        args += list(rope)
    return pl.pallas_call(
        functools.partial(_linear_kernel, has_gain=gain is not None, has_res=residual is not None,
                          has_rope=rope is not None),
        grid=(M // tm, N // tn),
        in_specs=in_specs,
        out_specs=pl.BlockSpec((tm, tn), lambda i, j: (i, j)),
        out_shape=jax.ShapeDtypeStruct((M, N), out_dtype),
        scratch_shapes=[pltpu.VMEM((tm, K), BF16)] if gain is not None else [],
        compiler_params=_params("parallel", "arbitrary"),
        name=name,
    )(*args)


def _mlp_kernel(*refs, has_out_norm):
    if has_out_norm:
        x_ref, g_ref, w1_ref, w2_ref, go_ref, o_ref, xn_ref = refs
    else:
        x_ref, g_ref, w1_ref, w2_ref, o_ref, xn_ref = refs
    f = pl.program_id(1)

    @pl.when(f == 0)
    def _():
        x = x_ref[...]
        ms = jnp.mean(x * x, axis=-1, keepdims=True)
        xn_ref[...] = (x * lax.rsqrt(ms + RMS_EPS) * g_ref[...]).astype(BF16)
        o_ref[...] = x

    u = _dot(xn_ref[...], w1_ref[...])
    r = jnp.square(jnp.maximum(u, 0.0)).astype(BF16)
    o_ref[...] += _dot(r, w2_ref[...])

    if has_out_norm:
        @pl.when(f == pl.num_programs(1) - 1)
        def _():
            y = o_ref[...]
            ms = jnp.mean(y * y, axis=-1, keepdims=True)
            o_ref[...] = y * lax.rsqrt(ms + RMS_EPS) * go_ref[...]


def _mlp(h, gain, w1, w2, layer, *, out_gain=None, tm=512, tf=1024):
    M, D = h.shape
    F = w1.shape[2]
    assert M % tm == 0 and F % tf == 0
    vec = pl.BlockSpec((1, D), lambda i, f: (0, 0))
    in_specs = [pl.BlockSpec((tm, D), lambda i, f: (i, 0)), vec,
                pl.BlockSpec((None, D, tf), lambda i, f: (layer, 0, f)),
                pl.BlockSpec((None, tf, D), lambda i, f: (layer, f, 0))]
    args = [h, gain.reshape(1, D).astype(F32), w1, w2]
    if out_gain is not None:
        in_specs.append(vec)
        args.append(out_gain.reshape(1, D).astype(F32))
    return pl.pallas_call(
        functools.partial(_mlp_kernel, has_out_norm=out_gain is not None),
        grid=(M // tm, F // tf),
        in_specs=in_specs,
        out_specs=pl.BlockSpec((tm, D), lambda i, f: (i, 0)),
        out_shape=jax.ShapeDtypeStruct((M, D), F32),
        scratch_shapes=[pltpu.VMEM((tm, D), BF16)],
        compiler_params=_params("parallel", "arbitrary"),
        name="mlp",
    )(*args)


SB_SKIP = 90.0


def _sb_kernel(q_ref, k_ref, v_ref, o_ref, *, t, n_sub):
    i = pl.program_id(2)
    row = lax.broadcasted_iota(jnp.int32, (t, t), 0)
    col = lax.broadcasted_iota(jnp.int32, (t, t), 1)
    later = (row > col).astype(BF16)
    causal = col < row

    def logits(q, j):
        return _dot_nt(q, k_ref[0, pl.ds(pl.multiple_of(j * t, t), t), :])

    def log_gates(z, diagonal):
        nz = -z
        lom = jnp.minimum(nz, 0.0) - jnp.log2(1.0 + jnp.exp2(jnp.minimum(z, nz)))
        log_beta = lom + z
        if diagonal:
            lom = jnp.where(causal, lom, 0.0)
        return log_beta, lom

    def later_sums(lom):
        return _dot(lom.astype(BF16), later), jnp.sum(lom, axis=1, keepdims=True)

    def gates(q, j, diagonal):
        log_beta, lom = log_gates(logits(q, j), diagonal)
        return (log_beta,) + later_sums(lom)

    def weigh(j, log_beta, after, diagonal):
        a = jnp.exp2(log_beta + after)
        if diagonal:
            a = jnp.where(causal, a, 0.0)
        return _dot(a.astype(BF16), v_ref[0, pl.ds(pl.multiple_of(j * t, t), t), :])

    skip = -SB_SKIP * LOG2E
    tiles = []
    for a in range(n_sub):
        jd = i * n_sub + a
        tiles.append((slice(a * t, (a + 1) * t), q_ref[0, a * t:(a + 1) * t, :], jd, jnp.maximum(jd - 1, 0)))
    z_d = [logits(q, jd) for _, q, jd, _ in tiles]
    z_p = [logits(q, jp) for _, q, _, jp in tiles]
    g_d = [log_gates(z, True) for z in z_d]
    g_p = [log_gates(z, False) for z in z_p]
    s_d = [later_sums(lom) for _, lom in g_d]
    s_p = [later_sums(lom) for _, lom in g_p]
    near = []
    pending = jnp.full((t, 1), skip, F32)
    for (rows, q, jd, jp), (lb_d, _), (lb_p, _), (within_d, carry), (within_p, total_p) in zip(tiles, g_d, g_p, s_d, s_p):
        has_prev = jd > 0
        acc = weigh(jd, lb_d, within_d, True) + jnp.where(has_prev, weigh(jp, lb_p, within_p + carry, False), 0.0)
        carry = carry + jnp.where(has_prev, total_p, 0.0)
        o_ref[0, rows, :] = acc.astype(o_ref.dtype)
        near.append((rows, q, jd, carry, acc))
        pending = jnp.maximum(pending, jnp.where(jd >= 2, carry, skip))

    def cond(s):
        j, carry, _ = s
        return jnp.logical_and(j >= 0, jnp.max(carry) > skip)

    @pl.when(jnp.max(pending) > skip)
    def _():
        for rows, q, jd, carry, acc in near:
            def body(s, q=q):
                j, carry, acc = s
                log_beta, within, total = gates(q, j, False)
                return j - 1, carry + total, acc + weigh(j, log_beta, within + carry, False)

            _, _, acc = lax.while_loop(cond, body, (jd - 2, carry, acc))
            o_ref[0, rows, :] = acc.astype(o_ref.dtype)


def _sb_attention(qkv, n_heads, *, t=256, n_sub=8):
    B, S, _ = qkv.shape
    d = HEAD_DIM
    tile = t * n_sub
    assert S % tile == 0
    return pl.pallas_call(
        functools.partial(_sb_kernel, t=t, n_sub=n_sub),
        grid=(B, n_heads, S // tile),
        in_specs=[pl.BlockSpec((1, tile, d), lambda b, h, i: (b, i, h)),
                  pl.BlockSpec((1, S, d), lambda b, h, i: (b, 0, n_heads + h)),
                  pl.BlockSpec((1, S, d), lambda b, h, i: (b, 0, 2 * n_heads + h))],
        out_specs=pl.BlockSpec((1, tile, d), lambda b, h, i: (b, i, h)),
        out_shape=jax.ShapeDtypeStruct((B, S, n_heads * d), BF16),
        compiler_params=_params("parallel", "parallel", "arbitrary"),
        name="sb_attn",
    )(qkv, qkv, qkv)


HG_CHUNK = 64
HG_SUB = 16
HG_GROUP = 4
HG_SPAN = 256
HG_SAFE = 70.0


def _hg_kernel(q_ref, fz_ref, v_ref, g_ref, lb_ref, gn_ref, tril_ref, o_ref, state_ref, oc_ref, b_ref, k_ref,
               *, n_chunks):
    C, U, d = HG_CHUNK, HG_SUB, HEAD_DIM

    @pl.when(pl.program_id(2) == 0)
    def _():
        state_ref[...] = jnp.zeros_like(state_ref)

    gn = gn_ref[...]
    row = lax.broadcasted_iota(jnp.int32, (C, C), 0)
    col = lax.broadcasted_iota(jnp.int32, (C, C), 1)
    lower = col <= row
    sub_row = lax.broadcasted_iota(jnp.int32, (U, 1), 0)
    tril = tril_ref[...]

    heads = []
    for hd in range(HG_GROUP):
        lanes = slice(hd * d, (hd + 1) * d)
        lb = lb_ref[:, lanes]
        fz = fz_ref[0, :, lanes]
        e = jnp.exp(-jnp.abs(fz))
        inv = 1.0 / (1.0 + e)
        sig_pos = jnp.where(fz >= 0, inv, e * inv)
        sig_neg = jnp.where(fz >= 0, e * inv, inv)
        log_f = jnp.log(lb + (1.0 - lb) * sig_pos)
        k = (1.0 - lb) * sig_neg
        l_hi = log_f.astype(BF16)
        l_lo = (log_f - l_hi.astype(F32)).astype(BF16)
        span = tril.shape[0]
        b = jnp.concatenate([_dot(tril, l_hi[r:r + span]) + _dot(tril, l_lo[r:r + span])
                             for r in range(0, fz.shape[0], span)], axis=0)
        heads.append((lanes, k, b))

    def chunk_row(b, r):
        picked = b.reshape(n_chunks, C, d)[:, r:r + 1, :]
        return jnp.broadcast_to(picked, (n_chunks, C, d)).reshape(b.shape)

    mids = [b - chunk_row(b, C // 2 - 1) for _, _, b in heads]
    safe = functools.reduce(jnp.maximum, [jnp.max(jnp.abs(rel)) for rel in mids]) < HG_SAFE

    def epilogue(o, g):
        o = o * lax.rsqrt(jnp.mean(o * o, axis=-1, keepdims=True) + RMS_EPS) * gn
        return (o * (g / (1.0 + jnp.exp(-g)))).astype(o_ref.dtype)

    @pl.when(safe)
    def _():
        prep = []
        for (lanes, k, b), rel in zip(heads, mids):
            q = q_ref[0, :, lanes]
            v16 = v_ref[0, :, lanes].astype(BF16)
            q_in = (q * jnp.exp(rel)).astype(BF16)
            k_in = (k * jnp.exp(-rel)).astype(BF16)
            q_st = (q * jnp.exp(b)).astype(BF16)
            b_last = b.reshape(n_chunks, C, d)[:, C - 1:C, :]
            kd = (k * jnp.exp(chunk_row(b, C - 1) - b)).astype(BF16)
            prep.append((q_in, k_in, q_st, v16, kd, jnp.exp(b_last)))
        states = [state_ref[hd] for hd in range(HG_GROUP)]
        for c in range(n_chunks):
            rows = slice(c * C, (c + 1) * C)
            for hd, (q_in, k_in, q_st, v16, kd, decay) in enumerate(prep):
                s = jnp.where(lower, _dot_nt(q_in[rows], k_in[rows]), 0.0)
                oc_ref[hd, rows, :] = (_dot(s.astype(BF16), v16[rows])
                                       + _dot_nt(q_st[rows], states[hd].astype(BF16)))
                states[hd] = states[hd] * decay[c] + _dot_tn(v16[rows], kd[rows])
        for hd, (lanes, _, _) in enumerate(heads):
            state_ref[hd] = states[hd]
            o_ref[0, :, lanes] = epilogue(oc_ref[hd], g_ref[0, :, lanes])

    def head_chunk(hd, r0):
        lanes = slice(hd * d, (hd + 1) * d)
        q = q_ref[0, pl.ds(r0, C), lanes]
        v = v_ref[0, pl.ds(r0, C), lanes]
        b = b_ref[hd, pl.ds(r0, C), :]
        k = k_ref[hd, pl.ds(r0, C), :]
        log2_k = jnp.log2(k)
        state_t = state_ref[hd]
        rows = pl.ds(r0, C)
        oc_ref[hd, rows, :] = _dot_nt((q * jnp.exp(b)).astype(BF16), state_t.astype(BF16))
        v16 = v.astype(BF16)
        for J in range(C // U - 1):
            lo_r, hi_r = J * U, (J + 1) * U
            ref_b = b[hi_r - 1:hi_r, :]
            kt = (k[lo_r:hi_r] * jnp.exp(ref_b - b[lo_r:hi_r])).astype(BF16)
            qs = (q[hi_r:] * jnp.exp(b[hi_r:] - ref_b)).astype(BF16)
            s = _dot_nt(qs, kt)
            oc_ref[hd, pl.ds(r0 + hi_r, C - hi_r), :] += _dot(s.astype(BF16), v16[lo_r:hi_r])
        b2 = b * LOG2E
        key2 = log2_k - b2
        for J in range(C // U):
            lo_r, hi_r = J * U, (J + 1) * U
            qj, bj, kj, vj = q[lo_r:hi_r], b2[lo_r:hi_r], key2[lo_r:hi_r], v[lo_r:hi_r]
            acc = jnp.zeros((U, d), F32)
            for s_i in range(U):
                w = qj * jnp.exp2(jnp.minimum(bj + kj[s_i:s_i + 1], 0.0))
                sc = jnp.sum(w, axis=1, keepdims=True)
                sc = jnp.where(sub_row >= s_i, sc, 0.0)
                acc = acc + sc * vj[s_i:s_i + 1]
            oc_ref[hd, pl.ds(r0 + lo_r, U), :] += acc
        b_last = b[C - 1:C, :]
        kd = (k * jnp.exp(b_last - b)).astype(BF16)
        state_ref[hd] = state_t * jnp.exp(b_last) + _dot_tn(v16, kd)
        o_ref[0, rows, lanes] = epilogue(oc_ref[hd, rows, :], g_ref[0, rows, lanes])

    @pl.when(jnp.logical_not(safe))
    def _():
        for hd, (_, k, b) in enumerate(heads):
            b_ref[hd] = b
            k_ref[hd] = k

        def chunk(c, _):
            r0 = pl.multiple_of(c * C, C)
            for hd in range(HG_GROUP):
                head_chunk(hd, r0)
            return 0

        lax.fori_loop(0, n_chunks, chunk, 0)


def _hgrn2(proj, lb, g_norm, n_heads, *, ts=512):
    B, S, _ = proj.shape
    d, G = HEAD_DIM, HG_GROUP
    assert S % ts == 0 and ts % HG_CHUNK == 0 and n_heads % G == 0
    n_groups = n_heads // G

    def col(part):
        return pl.BlockSpec((1, ts, G * d), lambda b, h, s: (b, s, part * n_groups + h))

    span = min(ts, HG_SPAN)
    r = jnp.arange(span)
    tril = ((r[None, :] <= r[:, None]) & (r[None, :] // HG_CHUNK == r[:, None] // HG_CHUNK)).astype(BF16)
    return pl.pallas_call(
        functools.partial(_hg_kernel, n_chunks=ts // HG_CHUNK),
        grid=(B, n_groups, S // ts),
        in_specs=[col(0), col(1), col(2), col(3),
                  pl.BlockSpec((1, G * d), lambda b, h, s: (0, h)),
                  pl.BlockSpec((1, d), lambda b, h, s: (0, 0)),
                  pl.BlockSpec((span, span), lambda b, h, s: (0, 0))],
        out_specs=pl.BlockSpec((1, ts, G * d), lambda b, h, s: (b, s, h)),
        out_shape=jax.ShapeDtypeStruct((B, S, n_heads * d), BF16),
        scratch_shapes=[pltpu.VMEM((G, d, d), F32), pltpu.VMEM((G, ts, d), F32),
                        pltpu.VMEM((G, ts, d), F32), pltpu.VMEM((G, ts, d), F32)],
        compiler_params=_params("parallel", "parallel", "arbitrary"),
        name="hgrn2",
    )(proj, proj, proj, proj, lb.reshape(1, n_heads * d), g_norm.reshape(1, d), tril)


def _rope_kernel(pos_ref, invf_ref, x_ref, kpe_ref, cos_ref, sin_ref):
    half = MLA_ROPE // 2
    ang = pos_ref[...].astype(F32) * invf_ref[...]
    lane = lax.broadcasted_iota(jnp.int32, ang.shape, 1)
    c = jnp.where(lane < MLA_ROPE, jnp.cos(ang), 0.0)
    s = jnp.sin(ang)
    s = jnp.where(lane < half, -s, jnp.where(lane < MLA_ROPE, s, 0.0))
    cos_ref[...] = c
    sin_ref[...] = s
    kpe_ref[...] = _rotate_half(x_ref[...], c, s).astype(kpe_ref.dtype)


def _rope_tables(positions, dkv, kpe_col, *, tm=512):
    M = dkv.shape[0]
    half = MLA_ROPE // 2
    freq = ROPE_THETA ** (-jnp.arange(0, MLA_ROPE, 2, dtype=F32) / MLA_ROPE)
    invf = jnp.concatenate([freq, freq, jnp.zeros((HEAD_DIM - 2 * half,), F32)]).reshape(1, HEAD_DIM)
    row = pl.BlockSpec((tm, HEAD_DIM), lambda i: (i, 0))
    return pl.pallas_call(
        _rope_kernel,
        grid=(M // tm,),
        in_specs=[pl.BlockSpec((tm, 1), lambda i: (i, 0)),
                  pl.BlockSpec((1, HEAD_DIM), lambda i: (0, 0)),
                  pl.BlockSpec((tm, HEAD_DIM), lambda i: (i, kpe_col))],
        out_specs=[row, row, row],
        out_shape=[jax.ShapeDtypeStruct((M, HEAD_DIM), BF16),
                   jax.ShapeDtypeStruct((M, HEAD_DIM), F32),
                   jax.ShapeDtypeStruct((M, HEAD_DIM), F32)],
        compiler_params=_params("parallel"),
        name="mla_rope",
    )(positions.reshape(M, 1), invf, dkv)


MLA_ROW_CHUNK = 256


def _mla_kernel(q_ref, kn_ref, v_ref, kpe_ref, o_ref, kcat_ref, v1_ref, sa_ref, sb_ref, m_ref, acc_ref, *, t):
    n_q = q_ref.shape[1] // t
    n_pairs = n_q * (n_q + 1) // 2
    for n in range(n_q):
        rows = slice(n * t, (n + 1) * t)
        kcat_ref[:MLA_NOPE, rows] = kn_ref[0, rows, :].T
        kcat_ref[MLA_NOPE:, rows] = kpe_ref[0, rows, :].T
    rc = MLA_ROW_CHUNK
    row = lax.broadcasted_iota(jnp.int32, (rc, t), 0) // MLA_CHUNK
    col = lax.broadcasted_iota(jnp.int32, (rc, t), 1) // MLA_CHUNK

    def tile(n):
        return pl.ds(pl.multiple_of(n * t, t), t)

    def scores(i, j, dst_ref):
        dst_ref[...] = _dot(q_ref[0, tile(i), :], kcat_ref[:, tile(j)])

    v1_ref[:, :HEAD_DIM] = v_ref[0]
    v1_ref[:, HEAD_DIM:] = jnp.ones((v1_ref.shape[0], HEAD_DIM), BF16)

    def reset():
        m_ref[...] = jnp.full_like(m_ref, -jnp.inf)
        acc_ref[...] = jnp.zeros_like(acc_ref)

    def absorb(j, src_ref, diagonal):
        vb = v1_ref[tile(j), :]
        for r in range(t // rc):
            rows = slice(r * rc, (r + 1) * rc)
            s = src_ref[rows, :]
            if diagonal:
                s = jnp.where(col <= row + (r * rc) // MLA_CHUNK, s, -jnp.inf)
            m_old = m_ref[rows, :]
            lane_groups = [s[:, c:c + HEAD_DIM] for c in range(0, t, HEAD_DIM)]
            m_col = jnp.max(functools.reduce(jnp.maximum, lane_groups + [m_old]), axis=1, keepdims=True)
            m_new = jnp.broadcast_to(m_col, m_old.shape)
            alpha = jnp.exp2(m_old - m_new)
            p = jnp.exp2((s - m_col).astype(BF16))
            acc_ref[rows, :] = jnp.concatenate([alpha, alpha], axis=1) * acc_ref[rows, :] + _dot(p, vb)
            m_ref[rows, :] = m_new

    def successor(i, j):
        last = j == i
        return jnp.where(last, i + 1, i), jnp.where(last, 0, j + 1)

    def plain(cur_ref, nxt_ref, i, j, nxt):
        scores(*nxt, nxt_ref)
        absorb(j, cur_ref, False)

    def closing(cur_ref, nxt_ref, i, nxt):
        scores(*nxt, nxt_ref)
        absorb(i, cur_ref, True)
        o_ref[0, tile(i), :] = (acc_ref[:, :HEAD_DIM] / acc_ref[:, HEAD_DIM:]).astype(o_ref.dtype)
        reset()

    def step(cur_ref, nxt_ref, i, j):
        ni, nj = successor(i, j)
        nxt = (jnp.minimum(ni, n_q - 1), nj)

        @pl.when(j < i)
        def _():
            plain(cur_ref, nxt_ref, i, j, nxt)

        @pl.when(j == i)
        def _():
            closing(cur_ref, nxt_ref, i, nxt)

    def two_steps(i, j):
        i2, j2 = successor(*successor(i, j))
        after = (jnp.minimum(i2, n_q - 1), j2)

        @pl.when(j + 1 < i)
        def _():
            plain(sa_ref, sb_ref, i, j, (i, j + 1))
            plain(sb_ref, sa_ref, i, j + 1, after)

        @pl.when(j + 1 == i)
        def _():
            plain(sa_ref, sb_ref, i, j, (i, i))
            closing(sb_ref, sa_ref, i, after)

        @pl.when(j == i)
        def _():
            closing(sa_ref, sb_ref, i, (i + 1, 0))
            plain(sb_ref, sa_ref, i + 1, 0, after)

        return i2, j2

    def run_of_plain(i, j, count):
        for c in range(0, count, 2):
            plain(sa_ref, sb_ref, i, j + c, (i, j + c + 1))
            plain(sb_ref, sa_ref, i, j + c + 1, (i, j + c + 2))

    def advance(i, j, count):
        for _ in range(count):
            i, j = successor(i, j)
        return i, j

    def four_steps(i, j):
        @pl.when(j + 3 < i)
        def _():
            run_of_plain(i, j, 4)

        @pl.when(j + 3 >= i)
        def _():
            two_steps(*two_steps(i, j))

        return advance(i, j, 4)

    def eight_steps(_, ij):
        i, j = ij

        @pl.when(j + 7 < i)
        def _():
            run_of_plain(i, j, 8)

        @pl.when(j + 7 >= i)
        def _():
            four_steps(*four_steps(i, j))

        return advance(i, j, 8)

    reset()
    scores(0, 0, sa_ref)
    ij = lax.fori_loop(0, n_pairs // 8, eight_steps, (jnp.int32(0), jnp.int32(0)))
    if n_pairs % 8 >= 4:
        ij = four_steps(*ij)
    if n_pairs % 4 >= 2:
        ij = two_steps(*ij)
    if n_pairs % 2:
        step(sa_ref, sb_ref, *ij)


def _mla_attention(q, kv, kpe, n_heads, *, t=512):
    B, S, _ = q.shape
    d = HEAD_DIM
    assert S % t == 0 and t % MLA_CHUNK == 0
    return pl.pallas_call(
        functools.partial(_mla_kernel, t=t),
        grid=(B, n_heads),
        in_specs=[pl.BlockSpec((1, S, 2 * d), lambda b, h: (b, 0, h)),
                  pl.BlockSpec((1, S, d), lambda b, h: (b, 0, 2 * h)),
                  pl.BlockSpec((1, S, d), lambda b, h: (b, 0, 2 * h + 1)),
                  pl.BlockSpec((1, S, d), lambda b, h: (b, 0, 0))],
        out_specs=pl.BlockSpec((1, S, d), lambda b, h: (b, 0, h)),
        out_shape=jax.ShapeDtypeStruct((B, S, n_heads * d), BF16),
        scratch_shapes=[pltpu.VMEM((2 * d, S), BF16), pltpu.VMEM((S, 2 * d), BF16),
                        pltpu.VMEM((t, t), F32), pltpu.VMEM((t, t), F32),
                        pltpu.VMEM((t, d), F32), pltpu.VMEM((t, 2 * d), F32)],
        compiler_params=_params("parallel", "parallel"),
        name="mla_attn",
    )(q, kv, kv, kpe)


def _sb_layer(h, gain, w_qkv, w_o, B, S):
    M, D = h.shape
    H = w_o.shape[0] // HEAD_DIM
    col_scale = jnp.where(jnp.arange(w_qkv.shape[1]) < H * HEAD_DIM, HEAD_DIM ** -0.5 * LOG2E, 1.0)
    qkv = _linear(h, (w_qkv * col_scale).astype(BF16), name="sb_qkv", gain=gain, out_dtype=BF16)
    o = _sb_attention(qkv.reshape(B, S, -1), H)
    return _linear(o.reshape(M, -1), w_o.astype(BF16), name="sb_out", residual=h)


def _hg_layer(h, gain, w_in, g_norm, w_o, lb, B, S):
    M, D = h.shape
    H = w_o.shape[0] // HEAD_DIM
    proj = _linear(h, w_in.astype(BF16), name="hg_in", gain=gain, out_dtype=F32)
    o = _hgrn2(proj.reshape(B, S, -1), lb, g_norm, H)
    return _linear(o.reshape(M, -1), w_o.astype(BF16), name="hg_out", residual=h)


def _mla_layer(h, positions, gain, w_dkv, q_norm, kv_norm, w_uq, w_ukv, w_o, B, S):
    M, D = h.shape
    H = w_o.shape[0] // MLA_NOPE
    qr, kvr = MLA_Q_RANK, MLA_KV_RANK
    pad_a = jnp.zeros((D, qr - kvr), F32)
    pad_b = jnp.zeros((D, HEAD_DIM - MLA_ROPE), F32)
    w_d = jnp.concatenate([w_dkv[:, qr:qr + kvr], pad_a, w_dkv[:, :qr], w_dkv[:, qr + kvr:], pad_b], axis=1)
    dkv = _linear(h, w_d.astype(BF16), name="mla_down", gain=gain, out_dtype=F32)
    kpe, cos, sin = _rope_tables(positions, dkv, (2 * qr) // HEAD_DIM)
    w_q = w_uq.reshape(qr, H, MLA_NOPE + MLA_ROPE) * ((MLA_NOPE + MLA_ROPE) ** -0.5 * LOG2E)
    w_q = jnp.concatenate([w_q, jnp.zeros((qr, H, HEAD_DIM - MLA_ROPE), F32)], axis=2).reshape(qr, H * 2 * HEAD_DIM)
    q = _linear(dkv, w_q.astype(BF16), name="mla_q", gain=q_norm, rope=(cos, sin), out_dtype=BF16, x_col=1)
    kv = _linear(dkv, w_ukv.astype(BF16), name="mla_kv", gain=kv_norm, out_dtype=BF16, x_col=0)
    o = _mla_attention(q.reshape(B, S, -1), kv.reshape(B, S, -1), kpe.reshape(B, S, -1), H)
    return _linear(o.reshape(M, -1), w_o.astype(BF16), name="mla_out", residual=h)


def kernel(x, positions, norm_mix, norm_mlp, final_norm, sb_w_qkv, sb_w_o, hg_w_in, hg_lb_logits, hg_g_norm, hg_w_o, mla_w_dkv, mla_q_norm, mla_kv_norm, mla_w_uq, mla_w_ukv, mla_w_o, mlp_w1, mlp_w2):
    B, S, D = x.shape
    depth = norm_mix.shape[0]
    assert depth >= 1
    p_lb = jax.nn.softmax(hg_lb_logits.astype(F32), axis=0)
    lb_all = jnp.cumsum(p_lb, axis=0) - p_lb[0]
    h = x.reshape(B * S, D)
    w1, w2 = mlp_w1.astype(BF16), mlp_w2.astype(BF16)
    for i in range(depth):
        m, j = i % N_MIXERS, i // N_MIXERS
        if m == 0:
            h = _sb_layer(h, norm_mix[i], sb_w_qkv[j], sb_w_o[j], B, S)
        elif m == 1:
            h = _hg_layer(h, norm_mix[i], hg_w_in[j], hg_g_norm[j], hg_w_o[j], lb_all[i], B, S)
        else:
            h = _mla_layer(h, positions, norm_mix[i], mla_w_dkv[j], mla_q_norm[j], mla_kv_norm[j],
                           mla_w_uq[j], mla_w_ukv[j], mla_w_o[j], B, S)
        h = _mlp(h, norm_mlp[i], w1, w2, i, out_gain=final_norm if i == depth - 1 else None)
    return h.reshape(B, S, D)
```

```python
import functools

import jax
import jax.numpy as jnp
from jax import lax
from jax.experimental import pallas as pl
from jax.experimental.pallas import tpu as pltpu

F32 = jnp.float32
BF16 = jnp.bfloat16

RMS_EPS = 1e-6
N_MIXERS = 3
HEAD_DIM = 128
MLA_NOPE = 128
MLA_ROPE = 64
MLA_Q_RANK = 768
MLA_KV_RANK = 512
MLA_CHUNK = 64
ROPE_THETA = 10000.0
LOG2E = 1.4426950408889634
VMEM_LIMIT = 56 * 1024 * 1024


def _params(*sem):
    return pltpu.CompilerParams(dimension_semantics=sem, vmem_limit_bytes=VMEM_LIMIT)


def _dot(a, b):
    return jnp.dot(a, b, preferred_element_type=F32)


def _dot_nt(a, b):
    return lax.dot_general(a, b, (((1,), (1,)), ((), ())), preferred_element_type=F32)


def _dot_tn(a, b):
    return lax.dot_general(a, b, (((0,), (0,)), ((), ())), preferred_element_type=F32)


def _split3(x):
    hi = x.astype(BF16)
    r1 = x - hi.astype(F32)
    mid = r1.astype(BF16)
    lo = (r1 - mid.astype(F32)).astype(BF16)
    return hi, mid, lo


def _rotate_half(x, cos, sin):
    half = MLA_ROPE // 2
    swapped = pltpu.roll(x, half, 1) + pltpu.roll(x, HEAD_DIM - half, 1)
    return x * cos + swapped * sin


def _linear_kernel(*refs, has_gain, has_res, has_rope):
    x_ref, w_ref = refs[0], refs[1]
    pos = 2
    g_ref = r_ref = cos_ref = sin_ref = None
    if has_gain:
        g_ref = refs[pos]
        pos += 1
    if has_res:
        r_ref = refs[pos]
        pos += 1
    if has_rope:
        cos_ref, sin_ref = refs[pos], refs[pos + 1]
        pos += 2
    o_ref = refs[pos]

    if has_gain:
        xn_ref = refs[pos + 1]

        @pl.when(pl.program_id(1) == 0)
        def _():
            x = x_ref[...].astype(F32)
            ms = jnp.mean(x * x, axis=-1, keepdims=True)
            xn_ref[...] = (x * lax.rsqrt(ms + RMS_EPS) * g_ref[...]).astype(BF16)

        lhs = xn_ref[...]
    else:
        lhs = x_ref[...]

    acc = _dot(lhs, w_ref[...])
    if has_res:
        acc = acc + r_ref[...]
    if has_rope:
        cos, sin = cos_ref[...], sin_ref[...]
        for c in range(0, acc.shape[1], 2 * HEAD_DIM):
            o_ref[:, c:c + HEAD_DIM] = acc[:, c:c + HEAD_DIM].astype(o_ref.dtype)
            rot = _rotate_half(acc[:, c + HEAD_DIM:c + 2 * HEAD_DIM], cos, sin)
            o_ref[:, c + HEAD_DIM:c + 2 * HEAD_DIM] = rot.astype(o_ref.dtype)
    else:
        o_ref[...] = acc.astype(o_ref.dtype)


LINEAR_VMEM_BUDGET = 44 * 1024 * 1024
RESIDENT_WEIGHT_BYTES = 8 * 1024 * 1024


def _linear_tiles(M, K, N, x_bytes, out_bytes, has_gain, has_res, has_rope):
    widths = [N] if K * N * 2 <= RESIDENT_WEIGHT_BYTES or N % 1024 else [w for w in (4096, 2048, 1024) if N % w == 0]
    for tm in (1024, 512, 256, 128):
        if M % tm:
            continue
        for tn in widths:
            need = 2 * tm * K * x_bytes + 2 * K * tn * 2 + 2 * tm * tn * out_bytes
            need += tm * K * 2 if has_gain else 0
            need += 2 * tm * tn * 4 if has_res else 0
            need += tm * tn * 4 + 4 * tm * HEAD_DIM * 4 if has_rope else 0
            if need <= LINEAR_VMEM_BUDGET:
                return tm, tn
    raise ValueError("no linear tiling fits VMEM")


def _linear(x, w, *, name, gain=None, residual=None, rope=None, out_dtype=F32, x_col=0):
    M = x.shape[0]
    K, N = w.shape
    assert gain is not None or x.dtype == BF16
    tm, tn = _linear_tiles(M, K, N, x.dtype.itemsize, jnp.dtype(out_dtype).itemsize,
                           gain is not None, residual is not None, rope is not None)
    assert rope is None or tn % (2 * HEAD_DIM) == 0
    in_specs = [pl.BlockSpec((tm, K), lambda i, j: (i, x_col)),
                pl.BlockSpec((K, tn), lambda i, j: (0, j))]
    args = [x, w]
    if gain is not None:
        in_specs.append(pl.BlockSpec((1, K), lambda i, j: (0, 0)))
        args.append(gain.reshape(1, K).astype(F32))
    if residual is not None:
        in_specs.append(pl.BlockSpec((tm, tn), lambda i, j: (i, j)))
        args.append(residual)
    if rope is not None:
        in_specs += [pl.BlockSpec((tm, HEAD_DIM), lambda i, j: (i, 0))] * 2
        args += list(rope)
    return pl.pallas_call(
        functools.partial(_linear_kernel, has_gain=gain is not None, has_res=residual is not None,
                          has_rope=rope is not None),
        grid=(M // tm, N // tn),
        in_specs=in_specs,
        out_specs=pl.BlockSpec((tm, tn), lambda i, j: (i, j)),
        out_shape=jax.ShapeDtypeStruct((M, N), out_dtype),
        scratch_shapes=[pltpu.VMEM((tm, K), BF16)] if gain is not None else [],
        compiler_params=_params("parallel", "arbitrary"),
        name=name,
    )(*args)


def _mlp_kernel(*refs, has_out_norm):
    if has_out_norm:
        x_ref, g_ref, w1_ref, w2_ref, go_ref, o_ref, xn_ref = refs
    else:
        x_ref, g_ref, w1_ref, w2_ref, o_ref, xn_ref = refs
    f = pl.program_id(1)

    @pl.when(f == 0)
    def _():
        x = x_ref[...]
        ms = jnp.mean(x * x, axis=-1, keepdims=True)
        xn_ref[...] = (x * lax.rsqrt(ms + RMS_EPS) * g_ref[...]).astype(BF16)
        o_ref[...] = x

    u = _dot(xn_ref[...], w1_ref[...])
    r = jnp.square(jnp.maximum(u, 0.0)).astype(BF16)
    o_ref[...] += _dot(r, w2_ref[...])

    if has_out_norm:
        @pl.when(f == pl.num_programs(1) - 1)
        def _():
            y = o_ref[...]
            ms = jnp.mean(y * y, axis=-1, keepdims=True)
            o_ref[...] = y * lax.rsqrt(ms + RMS_EPS) * go_ref[...]


def _mlp(h, gain, w1, w2, layer, *, out_gain=None, tm=512, tf=1024):
    M, D = h.shape
    F = w1.shape[2]
    assert M % tm == 0 and F % tf == 0
    vec = pl.BlockSpec((1, D), lambda i, f: (0, 0))
    in_specs = [pl.BlockSpec((tm, D), lambda i, f: (i, 0)), vec,
                pl.BlockSpec((None, D, tf), lambda i, f: (layer, 0, f)),
                pl.BlockSpec((None, tf, D), lambda i, f: (layer, f, 0))]
    args = [h, gain.reshape(1, D).astype(F32), w1, w2]
    if out_gain is not None:
        in_specs.append(vec)
        args.append(out_gain.reshape(1, D).astype(F32))
    return pl.pallas_call(
        functools.partial(_mlp_kernel, has_out_norm=out_gain is not None),
        grid=(M // tm, F // tf),
        in_specs=in_specs,
        out_specs=pl.BlockSpec((tm, D), lambda i, f: (i, 0)),
        out_shape=jax.ShapeDtypeStruct((M, D), F32),
        scratch_shapes=[pltpu.VMEM((tm, D), BF16)],
        compiler_params=_params("parallel", "arbitrary"),
        name="mlp",
    )(*args)


SB_SKIP = 90.0


def _sb_kernel(q_ref, k_ref, v_ref, o_ref, *, t, n_sub):
    i = pl.program_id(2)
    row = lax.broadcasted_iota(jnp.int32, (t, t), 0)
    col = lax.broadcasted_iota(jnp.int32, (t, t), 1)
    later = (row > col).astype(BF16)
    causal = col < row

    def logits(q, j):
        return _dot_nt(q, k_ref[0, pl.ds(pl.multiple_of(j * t, t), t), :])

    def log_gates(z, diagonal):
        nz = -z
        lom = jnp.minimum(nz, 0.0) - jnp.log2(1.0 + jnp.exp2(jnp.minimum(z, nz)))
        log_beta = lom + z
        if diagonal:
            lom = jnp.where(causal, lom, 0.0)
        return log_beta, lom

    def later_sums(lom):
        return _dot(lom.astype(BF16), later), jnp.sum(lom, axis=1, keepdims=True)

    def gates(q, j, diagonal):
        log_beta, lom = log_gates(logits(q, j), diagonal)
        return (log_beta,) + later_sums(lom)

    def weigh(j, log_beta, after, diagonal):
        a = jnp.exp2(log_beta + after)
        if diagonal:
            a = jnp.where(causal, a, 0.0)
        return _dot(a.astype(BF16), v_ref[0, pl.ds(pl.multiple_of(j * t, t), t), :])

    skip = -SB_SKIP * LOG2E
    tiles = []
    for a in range(n_sub):
        jd = i * n_sub + a
        tiles.append((slice(a * t, (a + 1) * t), q_ref[0, a * t:(a + 1) * t, :], jd, jnp.maximum(jd - 1, 0)))
    z_d = [logits(q, jd) for _, q, jd, _ in tiles]
    z_p = [logits(q, jp) for _, q, _, jp in tiles]
    g_d = [log_gates(z, True) for z in z_d]
    g_p = [log_gates(z, False) for z in z_p]
    s_d = [later_sums(lom) for _, lom in g_d]
    s_p = [later_sums(lom) for _, lom in g_p]
    near = []
    pending = jnp.full((t, 1), skip, F32)
    for (rows, q, jd, jp), (lb_d, _), (lb_p, _), (within_d, carry), (within_p, total_p) in zip(tiles, g_d, g_p, s_d, s_p):
        has_prev = jd > 0
        acc = weigh(jd, lb_d, within_d, True) + jnp.where(has_prev, weigh(jp, lb_p, within_p + carry, False), 0.0)
        carry = carry + jnp.where(has_prev, total_p, 0.0)
        o_ref[0, rows, :] = acc.astype(o_ref.dtype)
        near.append((rows, q, jd, carry, acc))
        pending = jnp.maximum(pending, jnp.where(jd >= 2, carry, skip))

    def cond(s):
        j, carry, _ = s
        return jnp.logical_and(j >= 0, jnp.max(carry) > skip)

    @pl.when(jnp.max(pending) > skip)
    def _():
        for rows, q, jd, carry, acc in near:
            def body(s, q=q):
                j, carry, acc = s
                log_beta, within, total = gates(q, j, False)
                return j - 1, carry + total, acc + weigh(j, log_beta, within + carry, False)

            _, _, acc = lax.while_loop(cond, body, (jd - 2, carry, acc))
            o_ref[0, rows, :] = acc.astype(o_ref.dtype)


def _sb_attention(qkv, n_heads, *, t=256, n_sub=32):
    B, S, _ = qkv.shape
    d = HEAD_DIM
    n_sub = min(n_sub, S // t)
    tile = t * n_sub
    assert S % tile == 0
    return pl.pallas_call(
        functools.partial(_sb_kernel, t=t, n_sub=n_sub),
        grid=(B, n_heads, S // tile),
        in_specs=[pl.BlockSpec((1, tile, d), lambda b, h, i: (b, i, h)),
                  pl.BlockSpec((1, S, d), lambda b, h, i: (b, 0, n_heads + h)),
                  pl.BlockSpec((1, S, d), lambda b, h, i: (b, 0, 2 * n_heads + h))],
        out_specs=pl.BlockSpec((1, tile, d), lambda b, h, i: (b, i, h)),
        out_shape=jax.ShapeDtypeStruct((B, S, n_heads * d), BF16),
        compiler_params=_params("parallel", "parallel", "arbitrary"),
        name="sb_attn",
    )(qkv, qkv, qkv)


HG_CHUNK = 64
HG_SUB = 16
HG_GROUP = 4
HG_SPAN = 256
HG_SAFE = 70.0


def _hg_kernel(q_ref, fz_ref, v_ref, g_ref, lb_ref, gn_ref, tril_ref, o_ref, state_ref, oc_ref, b_ref, k_ref,
               *, n_chunks):
    C, U, d = HG_CHUNK, HG_SUB, HEAD_DIM

    @pl.when(pl.program_id(2) == 0)
    def _():
        state_ref[...] = jnp.zeros_like(state_ref)

    gn = gn_ref[...]
    row = lax.broadcasted_iota(jnp.int32, (C, C), 0)
    col = lax.broadcasted_iota(jnp.int32, (C, C), 1)
    lower = col <= row
    sub_row = lax.broadcasted_iota(jnp.int32, (U, 1), 0)
    tril = tril_ref[...]

    heads = []
    for hd in range(HG_GROUP):
        lanes = slice(hd * d, (hd + 1) * d)
        lb = lb_ref[:, lanes]
        fz = fz_ref[0, :, lanes]
        e = jnp.exp(-jnp.abs(fz))
        inv = 1.0 / (1.0 + e)
        sig_pos = jnp.where(fz >= 0, inv, e * inv)
        sig_neg = jnp.where(fz >= 0, e * inv, inv)
        log_f = jnp.log(lb + (1.0 - lb) * sig_pos)
        k = (1.0 - lb) * sig_neg
        l_hi = log_f.astype(BF16)
        l_lo = (log_f - l_hi.astype(F32)).astype(BF16)
        span = tril.shape[0]
        b = jnp.concatenate([_dot(tril, l_hi[r:r + span]) + _dot(tril, l_lo[r:r + span])
                             for r in range(0, fz.shape[0], span)], axis=0)
        heads.append((lanes, k, b))

    def chunk_row(b, r):
        picked = b.reshape(n_chunks, C, d)[:, r:r + 1, :]
        return jnp.broadcast_to(picked, (n_chunks, C, d)).reshape(b.shape)

    mids = [b - chunk_row(b, C // 2 - 1) for _, _, b in heads]
    safe = functools.reduce(jnp.maximum, [jnp.max(jnp.abs(rel)) for rel in mids]) < HG_SAFE

    def epilogue(o, g):
        o = o * lax.rsqrt(jnp.mean(o * o, axis=-1, keepdims=True) + RMS_EPS) * gn
        return (o * (g / (1.0 + jnp.exp(-g)))).astype(o_ref.dtype)

    @pl.when(safe)
    def _():
        prep = []
        for (lanes, k, b), rel in zip(heads, mids):
            q = q_ref[0, :, lanes]
            v16 = v_ref[0, :, lanes].astype(BF16)
            q_in = (q * jnp.exp(rel)).astype(BF16)
            k_in = (k * jnp.exp(-rel)).astype(BF16)
            q_st = (q * jnp.exp(b)).astype(BF16)
            b_last = b.reshape(n_chunks, C, d)[:, C - 1:C, :]
            kd = (k * jnp.exp(chunk_row(b, C - 1) - b)).astype(BF16)
            prep.append((q_in, k_in, q_st, v16, kd, jnp.exp(b_last)))
        states = [state_ref[hd] for hd in range(HG_GROUP)]
        for c in range(n_chunks):
            rows = slice(c * C, (c + 1) * C)
            for hd, (q_in, k_in, q_st, v16, kd, decay) in enumerate(prep):
                s = jnp.where(lower, _dot_nt(q_in[rows], k_in[rows]), 0.0)
                oc_ref[hd, rows, :] = (_dot(s.astype(BF16), v16[rows])
                                       + _dot_nt(q_st[rows], states[hd].astype(BF16)))
                states[hd] = states[hd] * decay[c] + _dot_tn(v16[rows], kd[rows])
        for hd, (lanes, _, _) in enumerate(heads):
            state_ref[hd] = states[hd]
            o_ref[0, :, lanes] = epilogue(oc_ref[hd], g_ref[0, :, lanes])

    def head_chunk(hd, r0):
        lanes = slice(hd * d, (hd + 1) * d)
        q = q_ref[0, pl.ds(r0, C), lanes]
        v = v_ref[0, pl.ds(r0, C), lanes]
        b = b_ref[hd, pl.ds(r0, C), :]
        k = k_ref[hd, pl.ds(r0, C), :]
        log2_k = jnp.log2(k)
        state_t = state_ref[hd]
        rows = pl.ds(r0, C)
        oc_ref[hd, rows, :] = _dot_nt((q * jnp.exp(b)).astype(BF16), state_t.astype(BF16))
        v16 = v.astype(BF16)
        for J in range(C // U - 1):
            lo_r, hi_r = J * U, (J + 1) * U
            ref_b = b[hi_r - 1:hi_r, :]
            kt = (k[lo_r:hi_r] * jnp.exp(ref_b - b[lo_r:hi_r])).astype(BF16)
            qs = (q[hi_r:] * jnp.exp(b[hi_r:] - ref_b)).astype(BF16)
            s = _dot_nt(qs, kt)
            oc_ref[hd, pl.ds(r0 + hi_r, C - hi_r), :] += _dot(s.astype(BF16), v16[lo_r:hi_r])
        b2 = b * LOG2E
        key2 = log2_k - b2
        for J in range(C // U):
            lo_r, hi_r = J * U, (J + 1) * U
            qj, bj, kj, vj = q[lo_r:hi_r], b2[lo_r:hi_r], key2[lo_r:hi_r], v[lo_r:hi_r]
            acc = jnp.zeros((U, d), F32)
            for s_i in range(U):
                w = qj * jnp.exp2(jnp.minimum(bj + kj[s_i:s_i + 1], 0.0))
                sc = jnp.sum(w, axis=1, keepdims=True)
                sc = jnp.where(sub_row >= s_i, sc, 0.0)
                acc = acc + sc * vj[s_i:s_i + 1]
            oc_ref[hd, pl.ds(r0 + lo_r, U), :] += acc
        b_last = b[C - 1:C, :]
        kd = (k * jnp.exp(b_last - b)).astype(BF16)
        state_ref[hd] = state_t * jnp.exp(b_last) + _dot_tn(v16, kd)
        o_ref[0, rows, lanes] = epilogue(oc_ref[hd, rows, :], g_ref[0, rows, lanes])

    @pl.when(jnp.logical_not(safe))
    def _():
        for hd, (_, k, b) in enumerate(heads):
            b_ref[hd] = b
            k_ref[hd] = k

        def chunk(c, _):
            r0 = pl.multiple_of(c * C, C)
            for hd in range(HG_GROUP):
                head_chunk(hd, r0)
            return 0

        lax.fori_loop(0, n_chunks, chunk, 0)


def _hgrn2(proj, lb, g_norm, n_heads, *, ts=1024):
    B, S, _ = proj.shape
    d, G = HEAD_DIM, HG_GROUP
    ts = min(ts, S)
    assert S % ts == 0 and ts % HG_CHUNK == 0 and n_heads % G == 0
    n_groups = n_heads // G

    def col(part):
        return pl.BlockSpec((1, ts, G * d), lambda b, h, s: (b, s, part * n_groups + h))

    span = min(ts, HG_SPAN)
    r = jnp.arange(span)
    tril = ((r[None, :] <= r[:, None]) & (r[None, :] // HG_CHUNK == r[:, None] // HG_CHUNK)).astype(BF16)
    return pl.pallas_call(
        functools.partial(_hg_kernel, n_chunks=ts // HG_CHUNK),
        grid=(B, n_groups, S // ts),
        in_specs=[col(0), col(1), col(2), col(3),
                  pl.BlockSpec((1, G * d), lambda b, h, s: (0, h)),
                  pl.BlockSpec((1, d), lambda b, h, s: (0, 0)),
                  pl.BlockSpec((span, span), lambda b, h, s: (0, 0))],
        out_specs=pl.BlockSpec((1, ts, G * d), lambda b, h, s: (b, s, h)),
        out_shape=jax.ShapeDtypeStruct((B, S, n_heads * d), BF16),
        scratch_shapes=[pltpu.VMEM((G, d, d), F32), pltpu.VMEM((G, ts, d), F32),
                        pltpu.VMEM((G, ts, d), F32), pltpu.VMEM((G, ts, d), F32)],
        compiler_params=_params("parallel", "parallel", "arbitrary"),
        name="hgrn2",
    )(proj, proj, proj, proj, lb.reshape(1, n_heads * d), g_norm.reshape(1, d), tril)


def _rope_kernel(pos_ref, invf_ref, x_ref, kpe_ref, cos_ref, sin_ref):
    half = MLA_ROPE // 2
    ang = pos_ref[...].astype(F32) * invf_ref[...]
    lane = lax.broadcasted_iota(jnp.int32, ang.shape, 1)
    c = jnp.where(lane < MLA_ROPE, jnp.cos(ang), 0.0)
    s = jnp.sin(ang)
    s = jnp.where(lane < half, -s, jnp.where(lane < MLA_ROPE, s, 0.0))
    cos_ref[...] = c
    sin_ref[...] = s
    kpe_ref[...] = _rotate_half(x_ref[...], c, s).astype(kpe_ref.dtype)


def _rope_tables(positions, dkv, kpe_col, *, tm=512):
    M = dkv.shape[0]
    half = MLA_ROPE // 2
    freq = ROPE_THETA ** (-jnp.arange(0, MLA_ROPE, 2, dtype=F32) / MLA_ROPE)
    invf = jnp.concatenate([freq, freq, jnp.zeros((HEAD_DIM - 2 * half,), F32)]).reshape(1, HEAD_DIM)
    row = pl.BlockSpec((tm, HEAD_DIM), lambda i: (i, 0))
    return pl.pallas_call(
        _rope_kernel,
        grid=(M // tm,),
        in_specs=[pl.BlockSpec((tm, 1), lambda i: (i, 0)),
                  pl.BlockSpec((1, HEAD_DIM), lambda i: (0, 0)),
                  pl.BlockSpec((tm, HEAD_DIM), lambda i: (i, kpe_col))],
        out_specs=[row, row, row],
        out_shape=[jax.ShapeDtypeStruct((M, HEAD_DIM), BF16),
                   jax.ShapeDtypeStruct((M, HEAD_DIM), F32),
                   jax.ShapeDtypeStruct((M, HEAD_DIM), F32)],
        compiler_params=_params("parallel"),
        name="mla_rope",
    )(positions.reshape(M, 1), invf, dkv)


MLA_ROW_CHUNK = 256


def _mla_kernel(q_ref, kn_ref, v_ref, kpe_ref, o_ref, kcat_ref, v1_ref, sa_ref, sb_ref, m_ref, acc_ref, *, t):
    n_q = q_ref.shape[1] // t
    n_pairs = n_q * (n_q + 1) // 2
    for n in range(n_q):
        rows = slice(n * t, (n + 1) * t)
        kcat_ref[:MLA_NOPE, rows] = kn_ref[0, rows, :].T
        kcat_ref[MLA_NOPE:, rows] = kpe_ref[0, rows, :].T
    rc = MLA_ROW_CHUNK
    row = lax.broadcasted_iota(jnp.int32, (rc, t), 0) // MLA_CHUNK
    col = lax.broadcasted_iota(jnp.int32, (rc, t), 1) // MLA_CHUNK

    def tile(n):
        return pl.ds(pl.multiple_of(n * t, t), t)

    def scores(i, j, dst_ref):
        dst_ref[...] = _dot(q_ref[0, tile(i), :], kcat_ref[:, tile(j)])

    v1_ref[:, :HEAD_DIM] = v_ref[0]
    v1_ref[:, HEAD_DIM:] = jnp.ones((v1_ref.shape[0], HEAD_DIM), BF16)

    def reset():
        m_ref[...] = jnp.full_like(m_ref, -jnp.inf)
        acc_ref[...] = jnp.zeros_like(acc_ref)

    def absorb(j, src_ref, diagonal):
        vb = v1_ref[tile(j), :]
        for r in range(t // rc):
            rows = slice(r * rc, (r + 1) * rc)
            s = src_ref[rows, :]
            if diagonal:
                s = jnp.where(col <= row + (r * rc) // MLA_CHUNK, s, -jnp.inf)
            m_old = m_ref[rows, :]
            lane_groups = [s[:, c:c + HEAD_DIM] for c in range(0, t, HEAD_DIM)]
            m_col = jnp.max(functools.reduce(jnp.maximum, lane_groups + [m_old]), axis=1, keepdims=True)
            m_new = jnp.broadcast_to(m_col, m_old.shape)
            alpha = jnp.exp2(m_old - m_new)
            p = jnp.exp2((s - m_col).astype(BF16))
            acc_ref[rows, :] = jnp.concatenate([alpha, alpha], axis=1) * acc_ref[rows, :] + _dot(p, vb)
            m_ref[rows, :] = m_new

    def successor(i, j):
        last = j == i
        return jnp.where(last, i + 1, i), jnp.where(last, 0, j + 1)

    def plain(cur_ref, nxt_ref, i, j, nxt):
        scores(*nxt, nxt_ref)
        absorb(j, cur_ref, False)

    def closing(cur_ref, nxt_ref, i, nxt):
        scores(*nxt, nxt_ref)
        absorb(i, cur_ref, True)
        o_ref[0, tile(i), :] = (acc_ref[:, :HEAD_DIM] / acc_ref[:, HEAD_DIM:]).astype(o_ref.dtype)
        reset()

    def step(cur_ref, nxt_ref, i, j):
        ni, nj = successor(i, j)
        nxt = (jnp.minimum(ni, n_q - 1), nj)

        @pl.when(j < i)
        def _():
            plain(cur_ref, nxt_ref, i, j, nxt)

        @pl.when(j == i)
        def _():
            closing(cur_ref, nxt_ref, i, nxt)

    def two_steps(i, j):
        i2, j2 = successor(*successor(i, j))
        after = (jnp.minimum(i2, n_q - 1), j2)

        @pl.when(j + 1 < i)
        def _():
            plain(sa_ref, sb_ref, i, j, (i, j + 1))
            plain(sb_ref, sa_ref, i, j + 1, after)

        @pl.when(j + 1 == i)
        def _():
            plain(sa_ref, sb_ref, i, j, (i, i))
            closing(sb_ref, sa_ref, i, after)

        @pl.when(j == i)
        def _():
            closing(sa_ref, sb_ref, i, (i + 1, 0))
            plain(sb_ref, sa_ref, i + 1, 0, after)

        return i2, j2

    def run_of_plain(i, j, count):
        for c in range(0, count, 2):
            plain(sa_ref, sb_ref, i, j + c, (i, j + c + 1))
            plain(sb_ref, sa_ref, i, j + c + 1, (i, j + c + 2))

    def advance(i, j, count):
        for _ in range(count):
            i, j = successor(i, j)
        return i, j

    def four_steps(i, j):
        @pl.when(j + 3 < i)
        def _():
            run_of_plain(i, j, 4)

        @pl.when(j + 3 >= i)
        def _():
            two_steps(*two_steps(i, j))

        return advance(i, j, 4)

    def eight_steps(_, ij):
        i, j = ij

        @pl.when(j + 7 < i)
        def _():
            run_of_plain(i, j, 8)

        @pl.when(j + 7 >= i)
        def _():
            four_steps(*four_steps(i, j))

        return advance(i, j, 8)

    reset()
    scores(0, 0, sa_ref)
    ij = lax.fori_loop(0, n_pairs // 8, eight_steps, (jnp.int32(0), jnp.int32(0)))
    if n_pairs % 8 >= 4:
        ij = four_steps(*ij)
    if n_pairs % 4 >= 2:
        ij = two_steps(*ij)
    if n_pairs % 2:
        step(sa_ref, sb_ref, *ij)


def _mla_attention(q, kv, kpe, n_heads, *, t=512):
    B, S, _ = q.shape
    d = HEAD_DIM
    assert S % t == 0 and t % MLA_CHUNK == 0
    return pl.pallas_call(
        functools.partial(_mla_kernel, t=t),
        grid=(B, n_heads),
        in_specs=[pl.BlockSpec((1, S, 2 * d), lambda b, h: (b, 0, h)),
                  pl.BlockSpec((1, S, d), lambda b, h: (b, 0, 2 * h)),
                  pl.BlockSpec((1, S, d), lambda b, h: (b, 0, 2 * h + 1)),
                  pl.BlockSpec((1, S, d), lambda b, h: (b, 0, 0))],
        out_specs=pl.BlockSpec((1, S, d), lambda b, h: (b, 0, h)),
        out_shape=jax.ShapeDtypeStruct((B, S, n_heads * d), BF16),
        scratch_shapes=[pltpu.VMEM((2 * d, S), BF16), pltpu.VMEM((S, 2 * d), BF16),
                        pltpu.VMEM((t, t), F32), pltpu.VMEM((t, t), F32),
                        pltpu.VMEM((t, d), F32), pltpu.VMEM((t, 2 * d), F32)],
        compiler_params=_params("parallel", "parallel"),
        name="mla_attn",
    )(q, kv, kv, kpe)


def _sb_layer(h, gain, w_qkv, w_o, B, S):
    M, D = h.shape
    H = w_o.shape[0] // HEAD_DIM
    col_scale = jnp.where(jnp.arange(w_qkv.shape[1]) < H * HEAD_DIM, HEAD_DIM ** -0.5 * LOG2E, 1.0)
    qkv = _linear(h, (w_qkv * col_scale).astype(BF16), name="sb_qkv", gain=gain, out_dtype=BF16)
    o = _sb_attention(qkv.reshape(B, S, -1), H)
    return _linear(o.reshape(M, -1), w_o.astype(BF16), name="sb_out", residual=h)


def _hg_layer(h, gain, w_in, g_norm, w_o, lb, B, S):
    M, D = h.shape
    H = w_o.shape[0] // HEAD_DIM
    proj = _linear(h, w_in.astype(BF16), name="hg_in", gain=gain, out_dtype=F32)
    o = _hgrn2(proj.reshape(B, S, -1), lb, g_norm, H)
    return _linear(o.reshape(M, -1), w_o.astype(BF16), name="hg_out", residual=h)


def _mla_layer(h, positions, gain, w_dkv, q_norm, kv_norm, w_uq, w_ukv, w_o, B, S):
    M, D = h.shape
    H = w_o.shape[0] // MLA_NOPE
    qr, kvr = MLA_Q_RANK, MLA_KV_RANK
    pad_a = jnp.zeros((D, qr - kvr), F32)
    pad_b = jnp.zeros((D, HEAD_DIM - MLA_ROPE), F32)
    w_d = jnp.concatenate([w_dkv[:, qr:qr + kvr], pad_a, w_dkv[:, :qr], w_dkv[:, qr + kvr:], pad_b], axis=1)
    dkv = _linear(h, w_d.astype(BF16), name="mla_down", gain=gain, out_dtype=F32)
    kpe, cos, sin = _rope_tables(positions, dkv, (2 * qr) // HEAD_DIM)
    w_q = w_uq.reshape(qr, H, MLA_NOPE + MLA_ROPE) * ((MLA_NOPE + MLA_ROPE) ** -0.5 * LOG2E)
    w_q = jnp.concatenate([w_q, jnp.zeros((qr, H, HEAD_DIM - MLA_ROPE), F32)], axis=2).reshape(qr, H * 2 * HEAD_DIM)
    q = _linear(dkv, w_q.astype(BF16), name="mla_q", gain=q_norm, rope=(cos, sin), out_dtype=BF16, x_col=1)
    kv = _linear(dkv, w_ukv.astype(BF16), name="mla_kv", gain=kv_norm, out_dtype=BF16, x_col=0)
    o = _mla_attention(q.reshape(B, S, -1), kv.reshape(B, S, -1), kpe.reshape(B, S, -1), H)
    return _linear(o.reshape(M, -1), w_o.astype(BF16), name="mla_out", residual=h)


def kernel(x, positions, norm_mix, norm_mlp, final_norm, sb_w_qkv, sb_w_o, hg_w_in, hg_lb_logits, hg_g_norm, hg_w_o, mla_w_dkv, mla_q_norm, mla_kv_norm, mla_w_uq, mla_w_ukv, mla_w_o, mlp_w1, mlp_w2):
    B, S, D = x.shape
    depth = norm_mix.shape[0]
    assert depth >= 1
    p_lb = jax.nn.softmax(hg_lb_logits.astype(F32), axis=0)
    lb_all = jnp.cumsum(p_lb, axis=0) - p_lb[0]
    h = x.reshape(B * S, D)
    w1, w2 = mlp_w1.astype(BF16), mlp_w2.astype(BF16)
    for i in range(depth):
        m, j = i % N_MIXERS, i // N_MIXERS
        if m == 0:
            h = _sb_layer(h, norm_mix[i], sb_w_qkv[j], sb_w_o[j], B, S)
        elif m == 1:
            h = _hg_layer(h, norm_mix[i], hg_w_in[j], hg_g_norm[j], hg_w_o[j], lb_all[i], B, S)
        else:
            h = _mla_layer(h, positions, norm_mix[i], mla_w_dkv[j], mla_q_norm[j], mla_kv_norm[j],
                           mla_w_uq[j], mla_w_ukv[j], mla_w_o[j], B, S)
        h = _mlp(h, norm_mlp[i], w1, w2, i, out_gain=final_norm if i == depth - 1 else None)
    return h.reshape(B, S, D)
```

```python
import functools

import jax
import jax.numpy as jnp
from jax import lax
from jax.experimental import pallas as pl
from jax.experimental.pallas import tpu as pltpu

F32 = jnp.float32
BF16 = jnp.bfloat16

RMS_EPS = 1e-6
N_MIXERS = 3
HEAD_DIM = 128
MLA_NOPE = 128
MLA_ROPE = 64
MLA_Q_RANK = 768
MLA_KV_RANK = 512
MLA_CHUNK = 64
ROPE_THETA = 10000.0
LOG2E = 1.4426950408889634
VMEM_LIMIT = 56 * 1024 * 1024


def _params(*sem):
    return pltpu.CompilerParams(dimension_semantics=sem, vmem_limit_bytes=VMEM_LIMIT)


def _dot(a, b):
    return jnp.dot(a, b, preferred_element_type=F32)


def _dot_nt(a, b):
    return lax.dot_general(a, b, (((1,), (1,)), ((), ())), preferred_element_type=F32)


def _dot_tn(a, b):
    return lax.dot_general(a, b, (((0,), (0,)), ((), ())), preferred_element_type=F32)


def _rotate_half(x, cos, sin):
    half = MLA_ROPE // 2
    swapped = pltpu.roll(x, half, 1) + pltpu.roll(x, HEAD_DIM - half, 1)
    return x * cos + swapped * sin


def _linear_kernel(*refs, has_gain, has_res, has_rope):
    x_ref, w_ref = refs[0], refs[1]
    pos = 2
    g_ref = r_ref = cos_ref = sin_ref = None
    if has_gain:
        g_ref = refs[pos]
        pos += 1
    if has_res:
        r_ref = refs[pos]
        pos += 1
    if has_rope:
        cos_ref, sin_ref = refs[pos], refs[pos + 1]
        pos += 2
    o_ref = refs[pos]

    if has_gain:
        xn_ref = refs[pos + 1]

        @pl.when(pl.program_id(1) == 0)
        def _():
            x = x_ref[...].astype(F32)
            ms = jnp.mean(x * x, axis=-1, keepdims=True)
            xn_ref[...] = (x * lax.rsqrt(ms + RMS_EPS) * g_ref[...]).astype(BF16)

        lhs = xn_ref[...]
    else:
        lhs = x_ref[...]

    acc = _dot(lhs, w_ref[...])
    if has_res:
        acc = acc + r_ref[...]
    if has_rope:
        cos, sin = cos_ref[...], sin_ref[...]
        for c in range(0, acc.shape[1], 2 * HEAD_DIM):
            o_ref[:, c:c + HEAD_DIM] = acc[:, c:c + HEAD_DIM].astype(o_ref.dtype)
            rot = _rotate_half(acc[:, c + HEAD_DIM:c + 2 * HEAD_DIM], cos, sin)
            o_ref[:, c + HEAD_DIM:c + 2 * HEAD_DIM] = rot.astype(o_ref.dtype)
    else:
        o_ref[...] = acc.astype(o_ref.dtype)


LINEAR_VMEM_BUDGET = 44 * 1024 * 1024
RESIDENT_WEIGHT_BYTES = 8 * 1024 * 1024


def _linear_tiles(M, K, N, x_bytes, out_bytes, has_gain, has_res, has_rope):
    widths = [N] if K * N * 2 <= RESIDENT_WEIGHT_BYTES or N % 1024 else [w for w in (4096, 2048, 1024) if N % w == 0]
    for tm in (1024, 512, 256, 128):
        if M % tm:
            continue
        for tn in widths:
            need = 2 * tm * K * x_bytes + 2 * K * tn * 2 + 2 * tm * tn * out_bytes
            need += tm * K * 2 if has_gain else 0
            need += 2 * tm * tn * 4 if has_res else 0
            need += tm * tn * 4 + 4 * tm * HEAD_DIM * 4 if has_rope else 0
            if need <= LINEAR_VMEM_BUDGET:
                return tm, tn
    raise ValueError("no linear tiling fits VMEM")


def _linear(x, w, *, name, gain=None, residual=None, rope=None, out_dtype=F32, x_col=0):
    M = x.shape[0]
    K, N = w.shape
    assert gain is not None or x.dtype == BF16
    tm, tn = _linear_tiles(M, K, N, x.dtype.itemsize, jnp.dtype(out_dtype).itemsize,
                           gain is not None, residual is not None, rope is not None)
    assert rope is None or tn % (2 * HEAD_DIM) == 0
    in_specs = [pl.BlockSpec((tm, K), lambda i, j: (i, x_col)),
                pl.BlockSpec((K, tn), lambda i, j: (0, j))]
    args = [x, w]
    if gain is not None:
        in_specs.append(pl.BlockSpec((1, K), lambda i, j: (0, 0)))
        args.append(gain.reshape(1, K).astype(F32))
    if residual is not None:
        in_specs.append(pl.BlockSpec((tm, tn), lambda i, j: (i, j)))
        args.append(residual)
    if rope is not None:
        in_specs += [pl.BlockSpec((tm, HEAD_DIM), lambda i, j: (i, 0))] * 2
        args += list(rope)
    return pl.pallas_call(
        functools.partial(_linear_kernel, has_gain=gain is not None, has_res=residual is not None,
                          has_rope=rope is not None),
        grid=(M // tm, N // tn),
        in_specs=in_specs,
        out_specs=pl.BlockSpec((tm, tn), lambda i, j: (i, j)),
        out_shape=jax.ShapeDtypeStruct((M, N), out_dtype),
        scratch_shapes=[pltpu.VMEM((tm, K), BF16)] if gain is not None else [],
        compiler_params=_params("parallel", "arbitrary"),
        name=name,
    )(*args)


def _cast_specs(grid, w1, w2, layer):
    _, D, F = w1.shape
    n_steps = 1
    for n in grid:
        n_steps *= n
    cols = F // n_steps
    assert F % n_steps == 0 and cols % HEAD_DIM == 0

    def step(*g):
        idx = 0
        for gi, n in zip(g, grid):
            idx = idx * n + gi
        return idx

    in_specs = [pl.BlockSpec((None, D, cols), lambda *g: (layer, 0, step(*g))),
                pl.BlockSpec((None, cols, D), lambda *g: (layer, step(*g), 0))]
    out_specs = [pl.BlockSpec((D, cols), lambda *g: (0, step(*g))),
                 pl.BlockSpec((cols, D), lambda *g: (step(*g), 0))]
    out_shape = [jax.ShapeDtypeStruct((D, F), BF16), jax.ShapeDtypeStruct((F, D), BF16)]
    return in_specs, out_specs, out_shape


def _with_cast(kernel_fn, n_in, n_out):
    def wrapped(*refs):
        ins, cast_in = refs[:n_in], refs[n_in:n_in + 2]
        outs, cast_out = refs[n_in + 2:n_in + 2 + n_out], refs[n_in + 2 + n_out:n_in + 4 + n_out]
        for src, dst in zip(cast_in, cast_out):
            dst[...] = src[...].astype(BF16)
        kernel_fn(*ins, *outs, *refs[n_in + 4 + n_out:])
    return wrapped


def _mlp_kernel(*refs, has_out_norm):
    if has_out_norm:
        x_ref, g_ref, w1_ref, w2_ref, go_ref, o_ref, xn_ref = refs
    else:
        x_ref, g_ref, w1_ref, w2_ref, o_ref, xn_ref = refs
    f = pl.program_id(1)

    @pl.when(f == 0)
    def _():
        x = x_ref[...]
        ms = jnp.mean(x * x, axis=-1, keepdims=True)
        xn_ref[...] = (x * lax.rsqrt(ms + RMS_EPS) * g_ref[...]).astype(BF16)
        o_ref[...] = x

    u = _dot(xn_ref[...], w1_ref[...])
    r = jnp.square(jnp.maximum(u, 0.0)).astype(BF16)
    o_ref[...] += _dot(r, w2_ref[...])

    if has_out_norm:
        @pl.when(f == pl.num_programs(1) - 1)
        def _():
            y = o_ref[...]
            ms = jnp.mean(y * y, axis=-1, keepdims=True)
            o_ref[...] = y * lax.rsqrt(ms + RMS_EPS) * go_ref[...]


def _mlp(h, gain, w1, w2, *, out_gain=None, tm=512, tf=1024):
    M, D = h.shape
    F = w1.shape[1]
    assert M % tm == 0 and F % tf == 0
    vec = pl.BlockSpec((1, D), lambda i, f: (0, 0))
    in_specs = [pl.BlockSpec((tm, D), lambda i, f: (i, 0)), vec,
                pl.BlockSpec((D, tf), lambda i, f: (0, f)),
                pl.BlockSpec((tf, D), lambda i, f: (f, 0))]
    args = [h, gain.reshape(1, D).astype(F32), w1, w2]
    if out_gain is not None:
        in_specs.append(vec)
        args.append(out_gain.reshape(1, D).astype(F32))
    return pl.pallas_call(
        functools.partial(_mlp_kernel, has_out_norm=out_gain is not None),
        grid=(M // tm, F // tf),
        in_specs=in_specs,
        out_specs=pl.BlockSpec((tm, D), lambda i, f: (i, 0)),
        out_shape=jax.ShapeDtypeStruct((M, D), F32),
        scratch_shapes=[pltpu.VMEM((tm, D), BF16)],
        compiler_params=_params("parallel", "arbitrary"),
        name="mlp",
    )(*args)


SB_SKIP = 90.0


def _sb_kernel(q_ref, k_ref, v_ref, o_ref, *, t, n_sub):
    i = pl.program_id(2)
    row = lax.broadcasted_iota(jnp.int32, (t, t), 0)
    col = lax.broadcasted_iota(jnp.int32, (t, t), 1)
    later = (row > col).astype(BF16)
    causal = col < row

    def logits(q, j):
        return _dot_nt(q, k_ref[0, pl.ds(pl.multiple_of(j * t, t), t), :])

    def log_gates(z, diagonal):
        nz = -z
        lom = jnp.minimum(nz, 0.0) - jnp.log2(1.0 + jnp.exp2(jnp.minimum(z, nz)))
        log_beta = lom + z
        if diagonal:
            lom = jnp.where(causal, lom, 0.0)
        return log_beta, lom

    def later_sums(lom):
        return _dot(lom.astype(BF16), later), jnp.sum(lom, axis=1, keepdims=True)

    def gates(q, j, diagonal):
        log_beta, lom = log_gates(logits(q, j), diagonal)
        return (log_beta,) + later_sums(lom)

    def weigh(j, log_beta, after, diagonal):
        a = jnp.exp2(log_beta + after)
        if diagonal:
            a = jnp.where(causal, a, 0.0)
        return _dot(a.astype(BF16), v_ref[0, pl.ds(pl.multiple_of(j * t, t), t), :])

    skip = -SB_SKIP * LOG2E
    tiles = []
    for a in range(n_sub):
        jd = i * n_sub + a
        tiles.append((slice(a * t, (a + 1) * t), q_ref[0, a * t:(a + 1) * t, :], jd, jnp.maximum(jd - 1, 0)))
    z_d = [logits(q, jd) for _, q, jd, _ in tiles]
    z_p = [logits(q, jp) for _, q, _, jp in tiles]
    g_d = [log_gates(z, True) for z in z_d]
    g_p = [log_gates(z, False) for z in z_p]
    s_d = [later_sums(lom) for _, lom in g_d]
    s_p = [later_sums(lom) for _, lom in g_p]
    near = []
    pending = jnp.full((t, 1), skip, F32)
    for (rows, q, jd, jp), (lb_d, _), (lb_p, _), (within_d, carry), (within_p, total_p) in zip(tiles, g_d, g_p, s_d, s_p):
        has_prev = jd > 0
        acc = weigh(jd, lb_d, within_d, True) + jnp.where(has_prev, weigh(jp, lb_p, within_p + carry, False), 0.0)
        carry = carry + jnp.where(has_prev, total_p, 0.0)
        o_ref[0, rows, :] = acc.astype(o_ref.dtype)
        near.append((rows, q, jd, carry, acc))
        pending = jnp.maximum(pending, jnp.where(jd >= 2, carry, skip))

    def cond(s):
        j, carry, _ = s
        return jnp.logical_and(j >= 0, jnp.max(carry) > skip)

    @pl.when(jnp.max(pending) > skip)
    def _():
        for rows, q, jd, carry, acc in near:
            def body(s, q=q):
                j, carry, acc = s
                log_beta, within, total = gates(q, j, False)
                return j - 1, carry + total, acc + weigh(j, log_beta, within + carry, False)

            _, _, acc = lax.while_loop(cond, body, (jd - 2, carry, acc))
            o_ref[0, rows, :] = acc.astype(o_ref.dtype)


def _sb_attention(qkv, n_heads, w1, w2, layer, *, t=256, n_sub=16):
    B, S, _ = qkv.shape
    d = HEAD_DIM
    n_sub = min(n_sub, S // t)
    tile = t * n_sub
    assert S % tile == 0
    grid = (B, n_heads, S // tile)
    cast_in, cast_out, cast_shape = _cast_specs(grid, w1, w2, layer)
    return pl.pallas_call(
        _with_cast(functools.partial(_sb_kernel, t=t, n_sub=n_sub), 3, 1),
        grid=grid,
        in_specs=[pl.BlockSpec((1, tile, d), lambda b, h, i: (b, i, h)),
                  pl.BlockSpec((1, S, d), lambda b, h, i: (b, 0, n_heads + h)),
                  pl.BlockSpec((1, S, d), lambda b, h, i: (b, 0, 2 * n_heads + h))] + cast_in,
        out_specs=[pl.BlockSpec((1, tile, d), lambda b, h, i: (b, i, h))] + cast_out,
        out_shape=[jax.ShapeDtypeStruct((B, S, n_heads * d), BF16)] + cast_shape,
        compiler_params=_params("parallel", "parallel", "arbitrary"),
        name="sb_attn",
    )(qkv, qkv, qkv, w1, w2)


HG_CHUNK = 64
HG_SUB = 16
HG_GROUP = 4
HG_SPAN = 256
HG_SAFE = 70.0


def _hg_kernel(q_ref, fz_ref, v_ref, g_ref, lb_ref, gn_ref, tril_ref, o_ref, state_ref, oc_ref, b_ref, k_ref,
               *, n_chunks):
    C, U, d = HG_CHUNK, HG_SUB, HEAD_DIM

    @pl.when(pl.program_id(2) == 0)
    def _():
        state_ref[...] = jnp.zeros_like(state_ref)

    gn = gn_ref[...]
    row = lax.broadcasted_iota(jnp.int32, (C, C), 0)
    col = lax.broadcasted_iota(jnp.int32, (C, C), 1)
    lower = col <= row
    sub_row = lax.broadcasted_iota(jnp.int32, (U, 1), 0)
    tril = tril_ref[...]

    heads = []
    for hd in range(HG_GROUP):
        lanes = slice(hd * d, (hd + 1) * d)
        lb = lb_ref[:, lanes]
        fz = fz_ref[0, :, lanes]
        e = jnp.exp(-jnp.abs(fz))
        inv = 1.0 / (1.0 + e)
        sig_pos = jnp.where(fz >= 0, inv, e * inv)
        sig_neg = jnp.where(fz >= 0, e * inv, inv)
        log_f = jnp.log(lb + (1.0 - lb) * sig_pos)
        k = (1.0 - lb) * sig_neg
        l_hi = log_f.astype(BF16)
        l_lo = (log_f - l_hi.astype(F32)).astype(BF16)
        span = tril.shape[0]
        b = jnp.concatenate([_dot(tril, l_hi[r:r + span]) + _dot(tril, l_lo[r:r + span])
                             for r in range(0, fz.shape[0], span)], axis=0)
        heads.append((lanes, k, b))

    def chunk_row(b, r):
        picked = b.reshape(n_chunks, C, d)[:, r:r + 1, :]
        return jnp.broadcast_to(picked, (n_chunks, C, d)).reshape(b.shape)

    mids = [b - chunk_row(b, C // 2 - 1) for _, _, b in heads]
    safe = functools.reduce(jnp.maximum, [jnp.max(jnp.abs(rel)) for rel in mids]) < HG_SAFE

    def epilogue(o, g):
        o = o * lax.rsqrt(jnp.mean(o * o, axis=-1, keepdims=True) + RMS_EPS) * gn
        return (o * (g / (1.0 + jnp.exp(-g)))).astype(o_ref.dtype)

    @pl.when(safe)
    def _():
        prep = []
        for (lanes, k, b), rel in zip(heads, mids):
            q = q_ref[0, :, lanes]
            v16 = v_ref[0, :, lanes].astype(BF16)
            q_in = (q * jnp.exp(rel)).astype(BF16)
            k_in = (k * jnp.exp(-rel)).astype(BF16)
            q_st = (q * jnp.exp(b)).astype(BF16)
            b_last = b.reshape(n_chunks, C, d)[:, C - 1:C, :]
            kd = (k * jnp.exp(chunk_row(b, C - 1) - b)).astype(BF16)
            prep.append((q_in, k_in, q_st, v16, kd, jnp.exp(b_last)))
        states = [state_ref[hd] for hd in range(HG_GROUP)]
        for c in range(n_chunks):
            rows = slice(c * C, (c + 1) * C)
            for hd, (q_in, k_in, q_st, v16, kd, decay) in enumerate(prep):
                s = jnp.where(lower, _dot_nt(q_in[rows], k_in[rows]), 0.0)
                oc_ref[hd, rows, :] = (_dot(s.astype(BF16), v16[rows])
                                       + _dot_nt(q_st[rows], states[hd].astype(BF16)))
                states[hd] = states[hd] * decay[c] + _dot_tn(v16[rows], kd[rows])
        for hd, (lanes, _, _) in enumerate(heads):
            state_ref[hd] = states[hd]
            o_ref[0, :, lanes] = epilogue(oc_ref[hd], g_ref[0, :, lanes])

    def head_chunk(hd, r0):
        lanes = slice(hd * d, (hd + 1) * d)
        q = q_ref[0, pl.ds(r0, C), lanes]
        v = v_ref[0, pl.ds(r0, C), lanes]
        b = b_ref[hd, pl.ds(r0, C), :]
        k = k_ref[hd, pl.ds(r0, C), :]
        log2_k = jnp.log2(k)
        state_t = state_ref[hd]
        rows = pl.ds(r0, C)
        oc_ref[hd, rows, :] = _dot_nt((q * jnp.exp(b)).astype(BF16), state_t.astype(BF16))
        v16 = v.astype(BF16)
        for J in range(C // U - 1):
            lo_r, hi_r = J * U, (J + 1) * U
            ref_b = b[hi_r - 1:hi_r, :]
            kt = (k[lo_r:hi_r] * jnp.exp(ref_b - b[lo_r:hi_r])).astype(BF16)
            qs = (q[hi_r:] * jnp.exp(b[hi_r:] - ref_b)).astype(BF16)
            s = _dot_nt(qs, kt)
            oc_ref[hd, pl.ds(r0 + hi_r, C - hi_r), :] += _dot(s.astype(BF16), v16[lo_r:hi_r])
        b2 = b * LOG2E
        key2 = log2_k - b2
        for J in range(C // U):
            lo_r, hi_r = J * U, (J + 1) * U
            qj, bj, kj, vj = q[lo_r:hi_r], b2[lo_r:hi_r], key2[lo_r:hi_r], v[lo_r:hi_r]
            acc = jnp.zeros((U, d), F32)
            for s_i in range(U):
                w = qj * jnp.exp2(jnp.minimum(bj + kj[s_i:s_i + 1], 0.0))
                sc = jnp.sum(w, axis=1, keepdims=True)
                sc = jnp.where(sub_row >= s_i, sc, 0.0)
                acc = acc + sc * vj[s_i:s_i + 1]
            oc_ref[hd, pl.ds(r0 + lo_r, U), :] += acc
        b_last = b[C - 1:C, :]
        kd = (k * jnp.exp(b_last - b)).astype(BF16)
        state_ref[hd] = state_t * jnp.exp(b_last) + _dot_tn(v16, kd)
        o_ref[0, rows, lanes] = epilogue(oc_ref[hd, rows, :], g_ref[0, rows, lanes])

    @pl.when(jnp.logical_not(safe))
    def _():
        for hd, (_, k, b) in enumerate(heads):
            b_ref[hd] = b
            k_ref[hd] = k

        def chunk(c, _):
            r0 = pl.multiple_of(c * C, C)
            for hd in range(HG_GROUP):
                head_chunk(hd, r0)
            return 0

        lax.fori_loop(0, n_chunks, chunk, 0)


def _hgrn2(proj, lb, g_norm, n_heads, w1, w2, layer, *, ts=1024):
    B, S, _ = proj.shape
    d, G = HEAD_DIM, HG_GROUP
    ts = min(ts, S)
    assert S % ts == 0 and ts % HG_CHUNK == 0 and n_heads % G == 0
    n_groups = n_heads // G

    def col(part):
        return pl.BlockSpec((1, ts, G * d), lambda b, h, s: (b, s, part * n_groups + h))

    span = min(ts, HG_SPAN)
    r = jnp.arange(span)
    tril = ((r[None, :] <= r[:, None]) & (r[None, :] // HG_CHUNK == r[:, None] // HG_CHUNK)).astype(BF16)
    grid = (B, n_groups, S // ts)
    cast_in, cast_out, cast_shape = _cast_specs(grid, w1, w2, layer)
    return pl.pallas_call(
        _with_cast(functools.partial(_hg_kernel, n_chunks=ts // HG_CHUNK), 7, 1),
        grid=grid,
        in_specs=[col(0), col(1), col(2), col(3),
                  pl.BlockSpec((1, G * d), lambda b, h, s: (0, h)),
                  pl.BlockSpec((1, d), lambda b, h, s: (0, 0)),
                  pl.BlockSpec((span, span), lambda b, h, s: (0, 0))] + cast_in,
        out_specs=[pl.BlockSpec((1, ts, G * d), lambda b, h, s: (b, s, h))] + cast_out,
        out_shape=[jax.ShapeDtypeStruct((B, S, n_heads * d), BF16)] + cast_shape,
        scratch_shapes=[pltpu.VMEM((G, d, d), F32), pltpu.VMEM((G, ts, d), F32),
                        pltpu.VMEM((G, ts, d), F32), pltpu.VMEM((G, ts, d), F32)],
        compiler_params=_params("parallel", "parallel", "arbitrary"),
        name="hgrn2",
    )(proj, proj, proj, proj, lb.reshape(1, n_heads * d), g_norm.reshape(1, d), tril, w1, w2)


def _rope_kernel(pos_ref, invf_ref, x_ref, kpe_ref, cos_ref, sin_ref):
    half = MLA_ROPE // 2
    ang = pos_ref[...].astype(F32) * invf_ref[...]
    lane = lax.broadcasted_iota(jnp.int32, ang.shape, 1)
    c = jnp.where(lane < MLA_ROPE, jnp.cos(ang), 0.0)
    s = jnp.sin(ang)
    s = jnp.where(lane < half, -s, jnp.where(lane < MLA_ROPE, s, 0.0))
    cos_ref[...] = c
    sin_ref[...] = s
    kpe_ref[...] = _rotate_half(x_ref[...], c, s).astype(kpe_ref.dtype)


def _rope_tables(positions, dkv, kpe_col, *, tm=512):
    M = dkv.shape[0]
    half = MLA_ROPE // 2
    freq = ROPE_THETA ** (-jnp.arange(0, MLA_ROPE, 2, dtype=F32) / MLA_ROPE)
    invf = jnp.concatenate([freq, freq, jnp.zeros((HEAD_DIM - 2 * half,), F32)]).reshape(1, HEAD_DIM)
    row = pl.BlockSpec((tm, HEAD_DIM), lambda i: (i, 0))
    return pl.pallas_call(
        _rope_kernel,
        grid=(M // tm,),
        in_specs=[pl.BlockSpec((tm, 1), lambda i: (i, 0)),
                  pl.BlockSpec((1, HEAD_DIM), lambda i: (0, 0)),
                  pl.BlockSpec((tm, HEAD_DIM), lambda i: (i, kpe_col))],
        out_specs=[row, row, row],
        out_shape=[jax.ShapeDtypeStruct((M, HEAD_DIM), BF16),
                   jax.ShapeDtypeStruct((M, HEAD_DIM), F32),
                   jax.ShapeDtypeStruct((M, HEAD_DIM), F32)],
        compiler_params=_params("parallel"),
        name="mla_rope",
    )(positions.reshape(M, 1), invf, dkv)


MLA_ROW_CHUNK = 256


def _mla_kernel(q_ref, kn_ref, v_ref, kpe_ref, o_ref, kcat_ref, v1_ref, sa_ref, sb_ref, m_ref, acc_ref, *, t):
    n_q = q_ref.shape[1] // t
    n_pairs = n_q * (n_q + 1) // 2
    for n in range(n_q):
        rows = slice(n * t, (n + 1) * t)
        kcat_ref[:MLA_NOPE, rows] = kn_ref[0, rows, :].T
        kcat_ref[MLA_NOPE:, rows] = kpe_ref[0, rows, :].T
    rc = MLA_ROW_CHUNK
    row = lax.broadcasted_iota(jnp.int32, (rc, t), 0) // MLA_CHUNK
    col = lax.broadcasted_iota(jnp.int32, (rc, t), 1) // MLA_CHUNK

    def tile(n):
        return pl.ds(pl.multiple_of(n * t, t), t)

    def scores(i, j, dst_ref):
        dst_ref[...] = _dot(q_ref[0, tile(i), :], kcat_ref[:, tile(j)])

    v1_ref[:, :HEAD_DIM] = v_ref[0]
    v1_ref[:, HEAD_DIM:] = jnp.ones((v1_ref.shape[0], HEAD_DIM), BF16)

    def reset():
        m_ref[...] = jnp.full_like(m_ref, -jnp.inf)
        acc_ref[...] = jnp.zeros_like(acc_ref)

    def absorb(j, src_ref, diagonal):
        vb = v1_ref[tile(j), :]
        for r in range(t // rc):
            rows = slice(r * rc, (r + 1) * rc)
            s = src_ref[rows, :]
            if diagonal:
                s = jnp.where(col <= row + (r * rc) // MLA_CHUNK, s, -jnp.inf)
            m_old = m_ref[rows, :]
            lane_groups = [s[:, c:c + HEAD_DIM] for c in range(0, t, HEAD_DIM)]
            m_col = jnp.max(functools.reduce(jnp.maximum, lane_groups + [m_old]), axis=1, keepdims=True)
            m_new = jnp.broadcast_to(m_col, m_old.shape)
            alpha = jnp.exp2(m_old - m_new)
            p = jnp.exp2((s - m_col).astype(BF16))
            acc_ref[rows, :] = jnp.concatenate([alpha, alpha], axis=1) * acc_ref[rows, :] + _dot(p, vb)
            m_ref[rows, :] = m_new

    def successor(i, j):
        last = j == i
        return jnp.where(last, i + 1, i), jnp.where(last, 0, j + 1)

    def plain(cur_ref, nxt_ref, i, j, nxt):
        scores(*nxt, nxt_ref)
        absorb(j, cur_ref, False)

    def closing(cur_ref, nxt_ref, i, nxt):
        scores(*nxt, nxt_ref)
        absorb(i, cur_ref, True)
        o_ref[0, tile(i), :] = (acc_ref[:, :HEAD_DIM] / acc_ref[:, HEAD_DIM:]).astype(o_ref.dtype)
        reset()

    def step(cur_ref, nxt_ref, i, j):
        ni, nj = successor(i, j)
        nxt = (jnp.minimum(ni, n_q - 1), nj)

        @pl.when(j < i)
        def _():
            plain(cur_ref, nxt_ref, i, j, nxt)

        @pl.when(j == i)
        def _():
            closing(cur_ref, nxt_ref, i, nxt)

    def two_steps(i, j):
        i2, j2 = successor(*successor(i, j))
        after = (jnp.minimum(i2, n_q - 1), j2)

        @pl.when(j + 1 < i)
        def _():
            plain(sa_ref, sb_ref, i, j, (i, j + 1))
            plain(sb_ref, sa_ref, i, j + 1, after)

        @pl.when(j + 1 == i)
        def _():
            plain(sa_ref, sb_ref, i, j, (i, i))
            closing(sb_ref, sa_ref, i, after)

        @pl.when(j == i)
        def _():
            closing(sa_ref, sb_ref, i, (i + 1, 0))
            plain(sb_ref, sa_ref, i + 1, 0, after)

        return i2, j2

    def run_of_plain(i, j, count):
        for c in range(0, count, 2):
            plain(sa_ref, sb_ref, i, j + c, (i, j + c + 1))
            plain(sb_ref, sa_ref, i, j + c + 1, (i, j + c + 2))

    def advance(i, j, count):
        for _ in range(count):
            i, j = successor(i, j)
        return i, j

    def four_steps(i, j):
        @pl.when(j + 3 < i)
        def _():
            run_of_plain(i, j, 4)

        @pl.when(j + 3 >= i)
        def _():
            two_steps(*two_steps(i, j))

        return advance(i, j, 4)

    def eight_steps(_, ij):
        i, j = ij

        @pl.when(j + 7 < i)
        def _():
            run_of_plain(i, j, 8)

        @pl.when(j + 7 >= i)
        def _():
            four_steps(*four_steps(i, j))

        return advance(i, j, 8)

    reset()
    scores(0, 0, sa_ref)
    ij = lax.fori_loop(0, n_pairs // 8, eight_steps, (jnp.int32(0), jnp.int32(0)))
    if n_pairs % 8 >= 4:
        ij = four_steps(*ij)
    if n_pairs % 4 >= 2:
        ij = two_steps(*ij)
    if n_pairs % 2:
        step(sa_ref, sb_ref, *ij)


def _mla_attention(q, kv, kpe, n_heads, w1, w2, layer, *, t=512):
    B, S, _ = q.shape
    d = HEAD_DIM
    assert S % t == 0 and t % MLA_CHUNK == 0
    grid = (B, n_heads)
    cast_in, cast_out, cast_shape = _cast_specs(grid, w1, w2, layer)
    return pl.pallas_call(
        _with_cast(functools.partial(_mla_kernel, t=t), 4, 1),
        grid=grid,
        in_specs=[pl.BlockSpec((1, S, 2 * d), lambda b, h: (b, 0, h)),
                  pl.BlockSpec((1, S, d), lambda b, h: (b, 0, 2 * h)),
                  pl.BlockSpec((1, S, d), lambda b, h: (b, 0, 2 * h + 1)),
                  pl.BlockSpec((1, S, d), lambda b, h: (b, 0, 0))] + cast_in,
        out_specs=[pl.BlockSpec((1, S, d), lambda b, h: (b, 0, h))] + cast_out,
        out_shape=[jax.ShapeDtypeStruct((B, S, n_heads * d), BF16)] + cast_shape,
        scratch_shapes=[pltpu.VMEM((2 * d, S), BF16), pltpu.VMEM((S, 2 * d), BF16),
                        pltpu.VMEM((t, t), F32), pltpu.VMEM((t, t), F32),
                        pltpu.VMEM((t, d), F32), pltpu.VMEM((t, 2 * d), F32)],
        compiler_params=_params("parallel", "parallel"),
        name="mla_attn",
    )(q, kv, kv, kpe, w1, w2)


def _sb_layer(h, gain, w_qkv, w_o, B, S, mlp_w):
    M, D = h.shape
    H = w_o.shape[0] // HEAD_DIM
    col_scale = jnp.where(jnp.arange(w_qkv.shape[1]) < H * HEAD_DIM, HEAD_DIM ** -0.5 * LOG2E, 1.0)
    qkv = _linear(h, (w_qkv * col_scale).astype(BF16), name="sb_qkv", gain=gain, out_dtype=BF16)
    o, w1, w2 = _sb_attention(qkv.reshape(B, S, -1), H, *mlp_w)
    return _linear(o.reshape(M, -1), w_o.astype(BF16), name="sb_out", residual=h), w1, w2


def _hg_layer(h, gain, w_in, g_norm, w_o, lb, B, S, mlp_w):
    M, D = h.shape
    H = w_o.shape[0] // HEAD_DIM
    proj = _linear(h, w_in.astype(BF16), name="hg_in", gain=gain, out_dtype=F32)
    o, w1, w2 = _hgrn2(proj.reshape(B, S, -1), lb, g_norm, H, *mlp_w)
    return _linear(o.reshape(M, -1), w_o.astype(BF16), name="hg_out", residual=h), w1, w2


def _mla_layer(h, positions, gain, w_dkv, q_norm, kv_norm, w_uq, w_ukv, w_o, B, S, mlp_w):
    M, D = h.shape
    H = w_o.shape[0] // MLA_NOPE
    qr, kvr = MLA_Q_RANK, MLA_KV_RANK
    pad_a = jnp.zeros((D, qr - kvr), F32)
    pad_b = jnp.zeros((D, HEAD_DIM - MLA_ROPE), F32)
    w_d = jnp.concatenate([w_dkv[:, qr:qr + kvr], pad_a, w_dkv[:, :qr], w_dkv[:, qr + kvr:], pad_b], axis=1)
    dkv = _linear(h, w_d.astype(BF16), name="mla_down", gain=gain, out_dtype=F32)
    kpe, cos, sin = _rope_tables(positions, dkv, (2 * qr) // HEAD_DIM)
    w_q = w_uq.reshape(qr, H, MLA_NOPE + MLA_ROPE) * ((MLA_NOPE + MLA_ROPE) ** -0.5 * LOG2E)
    w_q = jnp.concatenate([w_q, jnp.zeros((qr, H, HEAD_DIM - MLA_ROPE), F32)], axis=2).reshape(qr, H * 2 * HEAD_DIM)
    q = _linear(dkv, w_q.astype(BF16), name="mla_q", gain=q_norm, rope=(cos, sin), out_dtype=BF16, x_col=1)
    kv = _linear(dkv, w_ukv.astype(BF16), name="mla_kv", gain=kv_norm, out_dtype=BF16, x_col=0)
    o, w1, w2 = _mla_attention(q.reshape(B, S, -1), kv.reshape(B, S, -1), kpe.reshape(B, S, -1), H, *mlp_w)
    return _linear(o.reshape(M, -1), w_o.astype(BF16), name="mla_out", residual=h), w1, w2


def kernel(x, positions, norm_mix, norm_mlp, final_norm, sb_w_qkv, sb_w_o, hg_w_in, hg_lb_logits, hg_g_norm, hg_w_o, mla_w_dkv, mla_q_norm, mla_kv_norm, mla_w_uq, mla_w_ukv, mla_w_o, mlp_w1, mlp_w2):
    B, S, D = x.shape
    depth = norm_mix.shape[0]
    assert depth >= 1
    p_lb = jax.nn.softmax(hg_lb_logits.astype(F32), axis=0)
    lb_all = jnp.cumsum(p_lb, axis=0) - p_lb[0]
    h = x.reshape(B * S, D)
    for i in range(depth):
        m, j = i % N_MIXERS, i // N_MIXERS
        mlp_w = (mlp_w1, mlp_w2, i)
        if m == 0:
            h, w1, w2 = _sb_layer(h, norm_mix[i], sb_w_qkv[j], sb_w_o[j], B, S, mlp_w)
        elif m == 1:
            h, w1, w2 = _hg_layer(h, norm_mix[i], hg_w_in[j], hg_g_norm[j], hg_w_o[j], lb_all[i], B, S, mlp_w)
        else:
            h, w1, w2 = _mla_layer(h, positions, norm_mix[i], mla_w_dkv[j], mla_q_norm[j], mla_kv_norm[j],
                                   mla_w_uq[j], mla_w_ukv[j], mla_w_o[j], B, S, mlp_w)
        h = _mlp(h, norm_mlp[i], w1, w2, out_gain=final_norm if i == depth - 1 else None)
    return h.reshape(B, S, D)
```
